```python
import jax, jax.numpy as jnp
from jax import lax
import numpy as np

D_MODEL = 1024
BATCH = 4
SEQ = 4096
DEPTH = 2
DEC_BATCH = 128
DEC_SEQ = 4
PAST_LEN = 8192
PAGE_SIZE = 128

HEAD_DIM = 64
N_HEADS_A = 8
N_KV_A = 2
KV_REP = N_HEADS_A // N_KV_A
WINDOW = 128
ROT_DIM = HEAD_DIM // 4
ROPE_THETA = 500000.0
ATTN_SCALE = HEAD_DIM ** -0.5
NEG_INF = -1e30
N_HEADS_B = 4
DK_B = 128
DV_B = 128
HGRN_CHUNK = 64
N_B_LAYERS = (DEPTH + 1) // 2
CONV_W = 31
N_KEYS = 128
N_EXPERTS = N_KEYS * N_KEYS
PEER_HEADS = 8
PEER_TOPK = 16
PEER_DKH = 128
PEER_BLOCK = 256
NORM_EPS = 1e-6

QA = N_HEADS_A * HEAD_DIM
KVA = N_KV_A * HEAD_DIM
QB = N_HEADS_B * DK_B
VB = N_HEADS_B * DV_B
IN0_SPLITS = (QA, QA + KVA, QA + 2 * KVA, QA + 2 * KVA + QB, QA + 2 * KVA + 2 * QB, QA + 2 * KVA + 2 * QB + VB)
IN0_WIDTH = QA + 2 * KVA + 2 * QB + 2 * VB
MIX0_WIDTH = QA + VB

kernel_name = 'hybrid_swa_hgrn2_conformer_peer_step'


def rms_norm(x, g):
    xf = x.astype(jnp.float32)
    y = xf * lax.rsqrt(jnp.mean(xf * xf, axis=-1, keepdims=True) + NORM_EPS)
    return (y * g.astype(jnp.float32)).astype(x.dtype)


def rope_partial(x, pos):
    inv = ROPE_THETA ** (-jnp.arange(0, ROT_DIM, 2, dtype=jnp.float32) / ROT_DIM)
    ang = pos[:, None] * inv[None, :]
    c = jnp.cos(ang)[:, None, :]
    s = jnp.sin(ang)[:, None, :]
    xf = x.astype(jnp.float32)
    half = ROT_DIM // 2
    x1 = xf[..., :half]
    x2 = xf[..., half:ROT_DIM]
    out = jnp.concatenate([x1 * c - x2 * s, x2 * c + x1 * s, xf[..., ROT_DIM:]], axis=-1)
    return out.astype(x.dtype)


def sink_softmax(s, mask, sinks):
    s = jnp.where(mask, s, NEG_INF)
    sk = jnp.broadcast_to(sinks.astype(jnp.float32).reshape(N_KV_A, KV_REP, 1, 1), s.shape[:-1] + (1,))
    p = jax.nn.softmax(jnp.concatenate([s, sk], axis=-1), axis=-1)
    return p[..., :-1]


def swa_prompt(q, k, v, sinks):
    B, T = q.shape[0], q.shape[1]
    nb = T // WINDOW
    qb = q.reshape(B, nb, WINDOW, N_KV_A, KV_REP, HEAD_DIM)

    def with_prev(xb):
        prev = jnp.concatenate([jnp.zeros_like(xb[:, :1]), xb[:, :-1]], axis=1)
        return jnp.concatenate([prev, xb], axis=2)

    kk = with_prev(k.reshape(B, nb, WINDOW, N_KV_A, HEAD_DIM))
    vv = with_prev(v.reshape(B, nb, WINDOW, N_KV_A, HEAD_DIM))
    s = jnp.einsum('bnqgrd,bnkgd->bngrqk', qb, kk, preferred_element_type=jnp.float32) * ATTN_SCALE
    qi = jnp.arange(WINDOW)[:, None]
    ki = jnp.arange(2 * WINDOW)[None, :]
    dist = qi + WINDOW - ki
    blk = jnp.arange(nb)[:, None, None]
    mask = (dist >= 0) & (dist < WINDOW) & ((blk > 0) | (ki >= WINDOW))
    p = sink_softmax(s, mask[None, :, None, None], sinks)
    o = jnp.einsum('bngrqk,bnkgd->bnqgrd', p.astype(v.dtype), vv)
    return o.reshape(B, T, QA)


def swa_decode(q, k, v, k_past, v_past, sinks):
    B, T = q.shape[0], q.shape[1]
    wb = k_past.shape[1]
    kk = jnp.concatenate([k_past.astype(k.dtype), k], axis=1)
    vv = jnp.concatenate([v_past.astype(v.dtype), v], axis=1)
    qg = q.reshape(B, T, N_KV_A, KV_REP, HEAD_DIM)
    s = jnp.einsum('btgrd,bkgd->bgrtk', qg, kk, preferred_element_type=jnp.float32) * ATTN_SCALE
    dist = (wb + jnp.arange(T))[:, None] - jnp.arange(wb + T)[None, :]
    mask = (dist >= 0) & (dist < WINDOW)
    p = sink_softmax(s, mask, sinks)
    o = jnp.einsum('bgrtk,bkgd->btgrd', p.astype(v.dtype), vv).reshape(B, T, QA)
    return o, kk[:, -wb:], vv[:, -wb:]


def hgrn_scan(q, k, i, logf, s0):
    B, H, T = q.shape[0], q.shape[1], q.shape[2]
    C = min(HGRN_CHUNK, T)
    n = -(-T // C)
    pad = n * C - T

    def prep(x):
        x = jnp.pad(x, ((0, 0), (0, 0), (0, pad), (0, 0)))
        return jnp.moveaxis(x.reshape(B, H, n, C, x.shape[-1]), 2, 0)

    tril = jnp.tril(jnp.ones((C, C), dtype=bool))[:, :, None]

    def step(S, xs):
        qc, kc, ic, lc = xs
        b = jnp.cumsum(lc, axis=2)
        o = jnp.einsum('bhtk,bhkv->bhtv', qc * jnp.exp(b), S)
        diff = b[:, :, :, None, :] - b[:, :, None, :, :]
        decay = jnp.where(tril, jnp.exp(jnp.where(tril, diff, 0.0)), 0.0)
        a = jnp.einsum('bhtk,bhtsk,bhsk->bhts', qc, decay, kc)
        o = o + jnp.einsum('bhts,bhsv->bhtv', a, ic)
        bl = b[:, :, -1:, :]
        S = jnp.exp(bl[:, :, 0, :, None]) * S + jnp.einsum('bhsk,bhsv->bhkv', kc * jnp.exp(bl - b), ic)
        return S, o

    S, o = lax.scan(step, s0, (prep(q), prep(k), prep(i), prep(logf)))
    o = jnp.moveaxis(o, 0, 2).reshape(B, H, n * C, i.shape[-1])[:, :, :T]
    return o, S


def even_mixer(h, pos, k_past, v_past, s_past, b_idx, w_in0, attn_q_norm_g, attn_k_norm_g, attn_sinks,
               hgrn_lb, hgrn_o_norm_g, w_out0):
    B, T = h.shape[0], h.shape[1]
    z = h @ w_in0
    qa, ka, va, qb, fb, ib, gb = jnp.split(z, IN0_SPLITS, axis=-1)
    q = rope_partial(rms_norm(qa.reshape(B, T, N_HEADS_A, HEAD_DIM), attn_q_norm_g), pos)
    k = rope_partial(rms_norm(ka.reshape(B, T, N_KV_A, HEAD_DIM), attn_k_norm_g), pos)
    v = va.reshape(B, T, N_KV_A, HEAD_DIM)
    if k_past is None:
        attn = swa_prompt(q, k, v, attn_sinks)
        k_win, v_win = k[:, -WINDOW:], v[:, -WINDOW:]
    else:
        attn, k_win, v_win = swa_decode(q, k, v, k_past, v_past, attn_sinks)
    lb = jnp.cumsum(jax.nn.softmax(hgrn_lb.astype(jnp.float32), axis=0), axis=0)[b_idx].reshape(N_HEADS_B, DK_B)
    fr = fb.astype(jnp.float32).reshape(B, T, N_HEADS_B, DK_B)
    logf = jnp.log(lb + (1.0 - lb) * jax.nn.sigmoid(fr))
    kb = (1.0 - lb) * jax.nn.sigmoid(-fr)
    qh = jax.nn.silu(qb.astype(jnp.float32)).reshape(B, T, N_HEADS_B, DK_B)
    ih = ib.astype(jnp.float32).reshape(B, T, N_HEADS_B, DV_B)
    s0 = jnp.zeros((B, N_HEADS_B, DK_B, DV_B), jnp.float32) if s_past is None else s_past.astype(jnp.float32)
    tr = lambda x: x.transpose(0, 2, 1, 3)
    o, s_new = hgrn_scan(tr(qh), tr(kb), tr(ih), tr(logf), s0)
    o = rms_norm(tr(o), hgrn_o_norm_g) * jax.nn.silu(gb.astype(jnp.float32).reshape(B, T, N_HEADS_B, DV_B))
    hg = o.reshape(B, T, VB).astype(h.dtype)
    out = jnp.concatenate([attn, hg], axis=-1) @ w_out0
    return out, k_win, v_win, s_new.astype(h.dtype)


def conv_mixer(h, buf, conv_w_pw1, conv_b_pw1, conv_w_dw, conv_b_dw, conv_ln_g, conv_ln_b, conv_w_pw2):
    B = h.shape[0]
    a = h @ conv_w_pw1 + conv_b_pw1
    u = a[..., :D_MODEL] * jax.nn.sigmoid(a[..., D_MODEL:])
    if buf is None:
        buf = jnp.zeros((B, CONV_W - 1, D_MODEL), u.dtype)
    up = jnp.concatenate([buf.astype(u.dtype), u], axis=1)
    c = lax.conv_general_dilated(up, conv_w_dw[:, None, :].astype(up.dtype), (1,), 'VALID',
                                 dimension_numbers=('NWC', 'WIO', 'NWC'),
                                 feature_group_count=D_MODEL) + conv_b_dw
    cf = c.astype(jnp.float32)
    mu = jnp.mean(cf, axis=-1, keepdims=True)
    var = jnp.mean(jnp.square(cf - mu), axis=-1, keepdims=True)
    ln = (cf - mu) * lax.rsqrt(var + NORM_EPS) * conv_ln_g.astype(jnp.float32) + conv_ln_b.astype(jnp.float32)
    y = jax.nn.silu(ln).astype(h.dtype) @ conv_w_pw2
    return y, up[:, -(CONV_W - 1):]


def peer(h, w_q, sub_keys, u_tab, v_tab):
    B, T, D = h.shape
    n = B * T
    blk = min(PEER_BLOCK, n)
    nb = -(-n // blk)
    xs = jnp.pad(h.reshape(n, D), ((0, nb * blk - n), (0, 0))).reshape(nb, blk, D)

    def block(xb):
        q = (xb @ w_q).reshape(blk, PEER_HEADS, 2, PEER_DKH)
        sc = jnp.einsum('nhpd,hpkd->nhpk', q, sub_keys, preferred_element_type=jnp.float32)
        sv, si = lax.top_k(sc, PEER_TOPK)
        cand = (sv[:, :, 0, :, None] + sv[:, :, 1, None, :]).reshape(blk, PEER_HEADS, PEER_TOPK * PEER_TOPK)
        cidx = (si[:, :, 0, :, None] * N_KEYS + si[:, :, 1, None, :]).reshape(blk, PEER_HEADS, PEER_TOPK * PEER_TOPK)
        fv, fi = lax.top_k(cand, PEER_TOPK)
        eidx = jnp.take_along_axis(cidx, fi, axis=-1)
        gate = jax.nn.softmax(fv, axis=-1)
        uu = jnp.take(u_tab, eidx, axis=0)
        act = jax.nn.gelu(jnp.einsum('nd,nhed->nhe', xb, uu).astype(jnp.float32), approximate=False)
        vv = jnp.take(v_tab, eidx, axis=0)
        return jnp.einsum('nhe,nhed->nd', (gate * act).astype(xb.dtype), vv)

    y = lax.map(block, xs).reshape(nb * blk, D)[:n]
    return y.reshape(B, T, D)


def trunk(x, pos, k_past, v_past, s_past, conv_past, norm_mix_g, norm_ffn_g, w_in0, attn_q_norm_g,
          attn_k_norm_g, attn_sinks, hgrn_lb, hgrn_o_norm_g, w_out0, conv_w_pw1, conv_b_pw1, conv_w_dw,
          conv_b_dw, conv_ln_g, conv_ln_b, conv_w_pw2, peer_w_q, peer_keys, peer_u, peer_v):
    for l in range(DEPTH):
        h = rms_norm(x, norm_mix_g[l])
        if l % 2 == 0:
            m, k_win, v_win, s_new = even_mixer(h, pos, k_past, v_past, s_past, l // 2, w_in0, attn_q_norm_g,
                                                attn_k_norm_g, attn_sinks, hgrn_lb, hgrn_o_norm_g, w_out0)
        else:
            m, conv_new = conv_mixer(h, conv_past, conv_w_pw1, conv_b_pw1, conv_w_dw, conv_b_dw,
                                     conv_ln_g, conv_ln_b, conv_w_pw2)
        x = x + m.astype(x.dtype)
        x = x + peer(rms_norm(x, norm_ffn_g[l]), peer_w_q[l], peer_keys[l], peer_u[l], peer_v[l]).astype(x.dtype)
    return x, k_win, v_win, s_new, conv_new


def setup_inputs(seed: int = 0) -> dict:
    key = jax.random.key(seed)
    ks = jax.random.split(key, 26)
    f32 = jnp.float32
    nrm = lambda k, shape, scale: jax.random.normal(k, shape, f32) * scale
    return {
        'x_prompt': nrm(ks[0], (BATCH, SEQ, D_MODEL), 1.0),
        'x_sample': nrm(ks[1], (DEC_BATCH, DEC_SEQ, D_MODEL), 1.0),
        'cache_swa_k': nrm(ks[2], (DEC_BATCH, WINDOW, N_KV_A, HEAD_DIM), 1.0),
        'cache_swa_v': nrm(ks[3], (DEC_BATCH, WINDOW, N_KV_A, HEAD_DIM), 1.0),
        'state_hgrn': nrm(ks[4], (DEC_BATCH, N_HEADS_B, DK_B, DV_B), 0.5),
        'state_conv': nrm(ks[5], (DEC_BATCH, CONV_W - 1, D_MODEL), 0.5),
        'norm_mix_g': 1.0 + nrm(ks[6], (DEPTH, D_MODEL), 0.05),
        'norm_ffn_g': 1.0 + nrm(ks[7], (DEPTH, D_MODEL), 0.05),
        'w_in0': nrm(ks[8], (D_MODEL, IN0_WIDTH), D_MODEL ** -0.5),
        'attn_q_norm_g': 1.0 + nrm(ks[9], (HEAD_DIM,), 0.05),
        'attn_k_norm_g': 1.0 + nrm(ks[10], (HEAD_DIM,), 0.05),
        'attn_sinks': nrm(ks[11], (N_HEADS_A,), 0.5),
        'hgrn_lb': nrm(ks[12], (N_B_LAYERS + 1, QB), 0.5),
        'hgrn_o_norm_g': 1.0 + nrm(ks[13], (DV_B,), 0.05),
        'w_out0': nrm(ks[14], (MIX0_WIDTH, D_MODEL), MIX0_WIDTH ** -0.5),
        'conv_w_pw1': nrm(ks[15], (D_MODEL, 2 * D_MODEL), D_MODEL ** -0.5),
        'conv_b_pw1': nrm(ks[16], (2 * D_MODEL,), 0.02),
        'conv_w_dw': nrm(ks[17], (CONV_W, D_MODEL), CONV_W ** -0.5),
        'conv_b_dw': nrm(ks[18], (D_MODEL,), 0.02),
        'conv_ln_g': 1.0 + nrm(ks[19], (D_MODEL,), 0.05),
        'conv_ln_b': nrm(ks[20], (D_MODEL,), 0.02),
        'conv_w_pw2': nrm(ks[21], (D_MODEL, D_MODEL), D_MODEL ** -0.5),
        'peer_w_q': nrm(ks[22], (DEPTH, D_MODEL, PEER_HEADS * 2 * PEER_DKH), D_MODEL ** -0.5),
        'peer_keys': nrm(ks[23], (DEPTH, PEER_HEADS, 2, N_KEYS, PEER_DKH), PEER_DKH ** -0.5),
        'peer_u': nrm(ks[24], (DEPTH, N_EXPERTS, D_MODEL), D_MODEL ** -0.5),
        'peer_v': nrm(ks[25], (DEPTH, N_EXPERTS, D_MODEL), PEER_HEADS ** -0.5),
    }


def reference(x_prompt, x_sample, cache_swa_k, cache_swa_v, state_hgrn, state_conv, norm_mix_g, norm_ffn_g,
              w_in0, attn_q_norm_g, attn_k_norm_g, attn_sinks, hgrn_lb, hgrn_o_norm_g, w_out0, conv_w_pw1,
              conv_b_pw1, conv_w_dw, conv_b_dw, conv_ln_g, conv_ln_b, conv_w_pw2, peer_w_q, peer_keys,
              peer_u, peer_v):
    pos_p = jnp.arange(x_prompt.shape[1], dtype=jnp.float32)
    pos_s = PAST_LEN + jnp.arange(x_sample.shape[1], dtype=jnp.float32)
    y_prompt, k_win_prompt, v_win_prompt, s_hgrn_prompt, conv_buf_prompt = trunk(
        x_prompt, pos_p, None, None, None, None, norm_mix_g, norm_ffn_g, w_in0, attn_q_norm_g,
        attn_k_norm_g, attn_sinks, hgrn_lb, hgrn_o_norm_g, w_out0, conv_w_pw1, conv_b_pw1, conv_w_dw,
        conv_b_dw, conv_ln_g, conv_ln_b, conv_w_pw2, peer_w_q, peer_keys, peer_u, peer_v)
    y_sample, k_win_sample, v_win_sample, s_hgrn_sample, conv_buf_sample = trunk(
        x_sample, pos_s, cache_swa_k, cache_swa_v, state_hgrn, state_conv, norm_mix_g, norm_ffn_g, w_in0,
        attn_q_norm_g, attn_k_norm_g, attn_sinks, hgrn_lb, hgrn_o_norm_g, w_out0, conv_w_pw1, conv_b_pw1,
        conv_w_dw, conv_b_dw, conv_ln_g, conv_ln_b, conv_w_pw2, peer_w_q, peer_keys, peer_u, peer_v)
    return (y_prompt, y_sample, k_win_prompt, v_win_prompt, s_hgrn_prompt, conv_buf_prompt,
            k_win_sample, v_win_sample, s_hgrn_sample, conv_buf_sample)
```

```python
import functools

import numpy as np
import jax
import jax.numpy as jnp
from jax import lax
from jax.experimental import pallas as pl
from jax.experimental.pallas import tpu as pltpu

F32 = jnp.float32
BF16 = jnp.bfloat16
HI = lax.Precision.HIGHEST

D_MODEL = 1024
PAST_LEN = 8192
HEAD_DIM = 64
N_HEADS_A = 8
N_KV_A = 2
KV_REP = N_HEADS_A // N_KV_A
WINDOW = 128
ROT_DIM = HEAD_DIM // 4
ROPE_THETA = 500000.0
ATTN_SCALE = HEAD_DIM ** -0.5
NEG_INF = -1e30
N_HEADS_B = 4
DK_B = 128
DV_B = 128
CONV_W = 31
N_KEYS = 128
PEER_HEADS = 8
PEER_TOPK = 16
PEER_DKH = 128
NORM_EPS = 1e-6

QA = N_HEADS_A * HEAD_DIM
KVA = N_KV_A * HEAD_DIM
QB = N_HEADS_B * DK_B
VB = N_HEADS_B * DV_B
IN0_WIDTH = QA + 2 * KVA + 2 * QB + 2 * VB
COL_K = QA // 128
COL_V = (QA + KVA) // 128
COL_QB = (QA + 2 * KVA) // 128
COL_FB = COL_QB + QB // 128
COL_IB = COL_FB + QB // 128
COL_GB = COL_IB + VB // 128

LANES = 128
TOKEN_TILE = 512
SEL_TILE = 256
HGRN_CHUNK = 64
HGRN_SUB = 16
HGRN_ROWS = 256
CONV_ROWS = 512
CONV_HALO = 32
PEER_ROWS = 8
DEC_GROUP = 8
VMEM_LIMIT = 48 * 1024 * 1024


def _dot(a, b, prec=None):
    return jnp.dot(a, b, preferred_element_type=F32, precision=prec)


def _dot_nt(a, b, prec=None):
    return lax.dot_general(a, b, (((1,), (1,)), ((), ())), preferred_element_type=F32, precision=prec)


def _rms(x, g):
    return x * lax.rsqrt(jnp.mean(x * x, axis=-1, keepdims=True) + NORM_EPS) * g


def _silu(x):
    return x * jax.nn.sigmoid(x)


def _gelu(x):
    return 0.5 * x * (1.0 + lax.erf(x * np.float32(0.7071067811865476)))


def _params(sem):
    return pltpu.CompilerParams(dimension_semantics=sem, vmem_limit_bytes=VMEM_LIMIT)


def _full(shape):
    n = len(shape)
    return pl.BlockSpec(shape, lambda *_: (0,) * n)


def _in0_body(x_ref, g_ref, w_ref, z_ref):
    h = _rms(x_ref[...], g_ref[...]).astype(BF16)
    z_ref[...] = _dot(h, w_ref[...])


def _in0(x, g, w):
    n = x.shape[0]
    width = w.shape[1]
    return pl.pallas_call(
        _in0_body,
        grid=(n // TOKEN_TILE,),
        in_specs=[pl.BlockSpec((TOKEN_TILE, D_MODEL), lambda i: (i, 0)), _full((1, D_MODEL)),
                  _full((D_MODEL, width))],
        out_specs=pl.BlockSpec((TOKEN_TILE, width), lambda i: (i, 0)),
        out_shape=jax.ShapeDtypeStruct((n, width), F32),
        compiler_params=_params(("arbitrary",)),
        name="in0_proj",
    )(x, g, w)


def _glu_body(x_ref, g_ref, w_ref, b_ref, u_ref):
    h = _rms(x_ref[...], g_ref[...]).astype(BF16)
    a = _dot(h, w_ref[...]) + b_ref[...]
    u_ref[...] = a[:, :D_MODEL] * jax.nn.sigmoid(a[:, D_MODEL:])


def _glu(x, g, w, b):
    n = x.shape[0]
    return pl.pallas_call(
        _glu_body,
        grid=(n // TOKEN_TILE,),
        in_specs=[pl.BlockSpec((TOKEN_TILE, D_MODEL), lambda i: (i, 0)), _full((1, D_MODEL)),
                  _full((D_MODEL, 2 * D_MODEL)), _full((1, 2 * D_MODEL))],
        out_specs=pl.BlockSpec((TOKEN_TILE, D_MODEL), lambda i: (i, 0)),
        out_shape=jax.ShapeDtypeStruct((n, D_MODEL), F32),
        compiler_params=_params(("arbitrary",)),
        name="conv_glu",
    )(x, g, w, b)


def _ffn_query(x1, gf_ref, wq_ref, x1_ref, ht_ref, q_ref):
    x1_ref[...] = x1
    h2 = _rms(x1, gf_ref[...])
    ht_ref[...] = h2.T.astype(BF16)
    q_ref[...] = _dot(h2.astype(BF16), wq_ref[...])


def _out0_body(a_ref, hg_ref, x_ref, w_ref, gf_ref, wq_ref, x1_ref, ht_ref, q_ref):
    m = _dot(a_ref[...].astype(BF16), w_ref[:QA, :]) + _dot(hg_ref[...].astype(BF16), w_ref[QA:, :])
    _ffn_query(x_ref[...] + m, gf_ref, wq_ref, x1_ref, ht_ref, q_ref)


def _post1_body(c_ref, x_ref, lg_ref, lb_ref, w_ref, gf_ref, wq_ref, x1_ref, ht_ref, q_ref):
    c = c_ref[...]
    mu = jnp.mean(c, axis=-1, keepdims=True)
    d = c - mu
    var = jnp.mean(d * d, axis=-1, keepdims=True)
    ln = d * lax.rsqrt(var + NORM_EPS) * lg_ref[...] + lb_ref[...]
    y = _dot(_silu(ln).astype(BF16), w_ref[...])
    _ffn_query(x_ref[...] + y, gf_ref, wq_ref, x1_ref, ht_ref, q_ref)


def _mix_out(body, acts, x, consts, gf, wq, name):
    n = x.shape[0]
    qw = wq.shape[1]
    row = lambda w: pl.BlockSpec((TOKEN_TILE, w), lambda i: (i, 0))
    return pl.pallas_call(
        body,
        grid=(n // TOKEN_TILE,),
        in_specs=[row(a.shape[1]) for a in acts] + [row(D_MODEL)] + [_full(c.shape) for c in consts]
        + [_full(gf.shape), _full(wq.shape)],
        out_specs=[row(D_MODEL), pl.BlockSpec((D_MODEL, TOKEN_TILE), lambda i: (0, i)), row(qw)],
        out_shape=[jax.ShapeDtypeStruct((n, D_MODEL), F32), jax.ShapeDtypeStruct((D_MODEL, n), BF16),
                   jax.ShapeDtypeStruct((n, qw), F32)],
        compiler_params=_params(("arbitrary",)),
        name=name,
    )(*acts, x, *consts, gf, wq)


def _head_norm(x, g, seg):
    ms = _dot(x * x, seg, HI)
    return x * lax.rsqrt(ms + NORM_EPS) * g


def _rope(x, cos, sin, first_half):
    half = ROT_DIM // 2
    width = x.shape[1]
    up = pltpu.roll(x, width - half, axis=1)
    dn = pltpu.roll(x, half, axis=1)
    return x * cos + jnp.where(first_half, -up, dn) * sin


def _rope_tables(pos, inv_ref, reps):
    ang = pos * inv_ref[...]
    c, s = jnp.cos(ang), jnp.sin(ang)
    if reps > 1:
        c, s = (jnp.concatenate([t] * reps, axis=1) for t in (c, s))
    lane = lax.broadcasted_iota(jnp.int32, c.shape, 1)
    return c, s, (lane % HEAD_DIM) < (ROT_DIM // 2)


def _stack_heads(x, g):
    return jnp.concatenate(
        [x[:, (g * KV_REP + r) * HEAD_DIM:(g * KV_REP + r + 1) * HEAD_DIM] for r in range(KV_REP)], axis=0)


def _attn_prompt_body(q_ref, k_ref, v_ref, gq_ref, gk_ref, inv_ref, segq_ref, segk_ref, sink_ref,
                      o_ref, kw_ref, vw_ref, kprev_ref, vprev_ref):
    j = pl.program_id(1)
    w = WINDOW

    @pl.when(j == 0)
    def _():
        kprev_ref[...] = jnp.zeros_like(kprev_ref)
        vprev_ref[...] = jnp.zeros_like(vprev_ref)

    rows = lax.broadcasted_iota(jnp.int32, (w, LANES), 0)
    pos = (j * w + rows).astype(F32)
    ck, sk, fk = _rope_tables(pos, inv_ref, 1)
    cq, sq, fq = _rope_tables(pos, inv_ref, QA // LANES)
    q = _rope(_head_norm(q_ref[...], gq_ref[...], segq_ref[...]), cq, sq, fq)
    k = _rope(_head_norm(k_ref[...], gk_ref[...], segk_ref[...]), ck, sk, fk)
    v = v_ref[...]
    kp = kprev_ref[...]
    vp = vprev_ref[...]

    qi = lax.broadcasted_iota(jnp.int32, (KV_REP * w, w), 0) % w
    ki = lax.broadcasted_iota(jnp.int32, (KV_REP * w, w), 1)
    m_own = ki <= qi
    m_prev = (ki > qi) & (j > 0)
    rep = lax.broadcasted_iota(jnp.int32, (KV_REP * w, 1), 0) // w
    for g in range(N_KV_A):
        sl = slice(g * HEAD_DIM, (g + 1) * HEAD_DIM)
        qg = _stack_heads(q, g).astype(BF16)
        s_own = jnp.where(m_own, _dot_nt(qg, k[:, sl].astype(BF16)) * ATTN_SCALE, NEG_INF)
        s_prev = jnp.where(m_prev, _dot_nt(qg, kp[:, sl].astype(BF16)) * ATTN_SCALE, NEG_INF)
        sink = jnp.zeros((KV_REP * w, 1), F32)
        for r in range(KV_REP):
            sink = jnp.where(rep == r, sink_ref[g * KV_REP + r], sink)
        mx = jnp.maximum(jnp.maximum(jnp.max(s_own, axis=-1, keepdims=True),
                                     jnp.max(s_prev, axis=-1, keepdims=True)), sink)
        e_own = jnp.exp(s_own - mx)
        e_prev = jnp.exp(s_prev - mx)
        den = (jnp.sum(e_own, axis=-1, keepdims=True) + jnp.sum(e_prev, axis=-1, keepdims=True)
               + jnp.exp(sink - mx))
        o = (_dot(e_own.astype(BF16), v[:, sl].astype(BF16))
             + _dot(e_prev.astype(BF16), vp[:, sl].astype(BF16))) / den
        for r in range(KV_REP):
            hq = g * KV_REP + r
            o_ref[:, hq * HEAD_DIM:(hq + 1) * HEAD_DIM] = o[r * w:(r + 1) * w]

    kprev_ref[...] = k
    vprev_ref[...] = v

    @pl.when(j == pl.num_programs(1) - 1)
    def _():
        kw_ref[0] = k
        vw_ref[0] = v


def _attn_prompt(z, batch, seq, gq, gk, inv, segq, segk, sinks):
    nb = seq // WINDOW
    blk = lambda width, col: pl.BlockSpec((WINDOW, width), lambda b, j: (b * nb + j, col))
    return pl.pallas_call(
        _attn_prompt_body,
        grid=(batch, nb),
        in_specs=[blk(QA, 0), blk(KVA, COL_K), blk(KVA, COL_V), _full(gq.shape), _full(gk.shape),
                  _full(inv.shape), _full(segq.shape), _full(segk.shape),
                  pl.BlockSpec(memory_space=pltpu.SMEM)],
        out_specs=[pl.BlockSpec((WINDOW, QA), lambda b, j: (b * nb + j, 0)),
                   pl.BlockSpec((1, WINDOW, KVA), lambda b, j: (b, 0, 0)),
                   pl.BlockSpec((1, WINDOW, KVA), lambda b, j: (b, 0, 0))],
        out_shape=[jax.ShapeDtypeStruct((batch * seq, QA), F32),
                   jax.ShapeDtypeStruct((batch, WINDOW, KVA), F32),
                   jax.ShapeDtypeStruct((batch, WINDOW, KVA), F32)],
        scratch_shapes=[pltpu.VMEM((WINDOW, KVA), F32), pltpu.VMEM((WINDOW, KVA), F32)],
        compiler_params=_params(("arbitrary", "arbitrary")),
        name="swa_prompt",
    )(z, z, z, gq, gk, inv, segq, segk, sinks)


def _attn_decode_body(q_ref, k_ref, v_ref, kc_ref, vc_ref, gq_ref, gk_ref, inv_ref, segq_ref, segk_ref,
                      sink_ref, o_ref, kn_ref, vn_ref, *, dec_seq):
    rows_n = DEC_GROUP * dec_seq
    rows = lax.broadcasted_iota(jnp.int32, (rows_n, LANES), 0)
    pos = (PAST_LEN + rows % dec_seq).astype(F32)
    ck, sk, fk = _rope_tables(pos, inv_ref, 1)
    cq, sq, fq = _rope_tables(pos, inv_ref, QA // LANES)
    q = _rope(_head_norm(q_ref[...], gq_ref[...], segq_ref[...]), cq, sq, fq)
    k = _rope(_head_norm(k_ref[...], gk_ref[...], segk_ref[...]), ck, sk, fk)
    v = v_ref[...]
    kn_ref[...] = k
    vn_ref[...] = v

    nq = KV_REP * rows_n
    nc = DEC_GROUP * WINDOW

    def qrow(shape):
        r = lax.broadcasted_iota(jnp.int32, shape, 0) % rows_n
        return r // dec_seq, r % dec_seq

    bq, tq = qrow((nq, nc))
    col = lax.broadcasted_iota(jnp.int32, (nq, nc), 1)
    m_cache = (col // WINDOW == bq) & (col % WINDOW > tq)
    bq, tq = qrow((nq, rows_n))
    col = lax.broadcasted_iota(jnp.int32, (nq, rows_n), 1)
    m_new = (col // dec_seq == bq) & (col % dec_seq <= tq)
    rep = lax.broadcasted_iota(jnp.int32, (nq, 1), 0) // rows_n
    for g in range(N_KV_A):
        sl = slice(g * HEAD_DIM, (g + 1) * HEAD_DIM)
        qg = _stack_heads(q, g).astype(BF16)
        s_c = jnp.where(m_cache, _dot_nt(qg, kc_ref[:, sl].astype(BF16)) * ATTN_SCALE, NEG_INF)
        s_n = jnp.where(m_new, _dot_nt(qg, k[:, sl].astype(BF16)) * ATTN_SCALE, NEG_INF)
        sink = jnp.zeros((nq, 1), F32)
        for r in range(KV_REP):
            sink = jnp.where(rep == r, sink_ref[g * KV_REP + r], sink)
        mx = jnp.maximum(jnp.maximum(jnp.max(s_c, axis=-1, keepdims=True),
                                     jnp.max(s_n, axis=-1, keepdims=True)), sink)
        e_c = jnp.exp(s_c - mx)
        e_n = jnp.exp(s_n - mx)
        den = jnp.sum(e_c, axis=-1, keepdims=True) + jnp.sum(e_n, axis=-1, keepdims=True) + jnp.exp(sink - mx)
        o = (_dot(e_c.astype(BF16), vc_ref[:, sl].astype(BF16))
             + _dot(e_n.astype(BF16), v[:, sl].astype(BF16))) / den
        for r in range(KV_REP):
            hq = g * KV_REP + r
            o_ref[:, hq * HEAD_DIM:(hq + 1) * HEAD_DIM] = o[r * rows_n:(r + 1) * rows_n]


def _attn_decode(z, row0, dec_batch, dec_seq, kc, vc, gq, gk, inv, segq, segk, sinks):
    rows_n = DEC_GROUP * dec_seq
    r0 = row0 // rows_n
    blk = lambda width, col: pl.BlockSpec((rows_n, width), lambda i: (r0 + i, col))
    cache = pl.BlockSpec((DEC_GROUP * WINDOW, KVA), lambda i: (i, 0))
    n = dec_batch * dec_seq
    return pl.pallas_call(
        functools.partial(_attn_decode_body, dec_seq=dec_seq),
        grid=(dec_batch // DEC_GROUP,),
        in_specs=[blk(QA, 0), blk(KVA, COL_K), blk(KVA, COL_V), cache, cache, _full(gq.shape),
                  _full(gk.shape), _full(inv.shape), _full(segq.shape), _full(segk.shape),
                  pl.BlockSpec(memory_space=pltpu.SMEM)],
        out_specs=[pl.BlockSpec((rows_n, QA), lambda i: (i, 0)), pl.BlockSpec((rows_n, KVA), lambda i: (i, 0)),
                   pl.BlockSpec((rows_n, KVA), lambda i: (i, 0))],
        out_shape=[jax.ShapeDtypeStruct((n, QA), F32), jax.ShapeDtypeStruct((n, KVA), F32),
                   jax.ShapeDtypeStruct((n, KVA), F32)],
        compiler_params=_params(("arbitrary",)),
        name="swa_decode",
    )(z, z, z, kc, vc, gq, gk, inv, segq, segk, sinks)


def _hgrn_lower_bound(lb_ref):
    l = lb_ref[...]
    e = jnp.exp(l - jnp.max(l, axis=0, keepdims=True))
    return e[0:1] / jnp.sum(e, axis=0, keepdims=True)


def _hgrn_gates(qb, fb, lb):
    logf = jnp.log(lb + (1.0 - lb) * jax.nn.sigmoid(fb))
    kb = (1.0 - lb) * jax.nn.sigmoid(-fb)
    return _silu(qb), kb, logf


def _hgrn_out(o, g, gate):
    return _rms(o, g) * _silu(gate)


def _hgrn_chunk(qh, kb, ih, logf, st, tril):
    c = qh.shape[0]
    b = _dot(tril, logf, HI)
    o_parts = []
    row = lax.broadcasted_iota(jnp.int32, (c, 1), 0)
    trow = lax.broadcasted_iota(jnp.int32, (HGRN_SUB, 1), 0)
    for blk in range(c // HGRN_SUB):
        t0 = blk * HGRN_SUB
        bi = b[t0:t0 + HGRN_SUB]
        qi = qh[t0:t0 + HGRN_SUB]
        if blk == 0:
            oi = _dot_nt(qi * jnp.exp(bi), st, HI)
        else:
            base = b[t0 - 1:t0]
            qd = qi * jnp.exp(bi - base)
            kd = jnp.where(row < t0, kb * jnp.exp(jnp.minimum(base - b, 0.0)), 0.0)
            oi = _dot_nt(qd * jnp.exp(base), st, HI) + _dot(_dot_nt(qd, kd, HI), ih, HI)
        for s in range(HGRN_SUB):
            live = trow >= s
            e = jnp.exp(jnp.where(live, bi - bi[s:s + 1], 0.0))
            a = jnp.sum(qi * e * kb[t0 + s:t0 + s + 1], axis=-1, keepdims=True)
            oi = oi + jnp.where(live, a, 0.0) * ih[t0 + s:t0 + s + 1]
        o_parts.append(oi)
    last = b[c - 1:c]
    st_new = st * jnp.exp(last) + _dot(ih.T, kb * jnp.exp(last - b), HI)
    return jnp.concatenate(o_parts, axis=0), st_new


def _hgrn_prompt_body(qb_ref, fb_ref, ib_ref, gb_ref, lb_ref, go_ref, tril_ref, o_ref, s_ref, st_ref):
    tb = pl.program_id(2)

    @pl.when(tb == 0)
    def _():
        st_ref[...] = jnp.zeros_like(st_ref)

    lb = _hgrn_lower_bound(lb_ref)
    tril = tril_ref[...]
    st = st_ref[...]
    for c in range(HGRN_ROWS // HGRN_CHUNK):
        rs = slice(c * HGRN_CHUNK, (c + 1) * HGRN_CHUNK)
        qh, kb, logf = _hgrn_gates(qb_ref[rs, :], fb_ref[rs, :], lb)
        o, st = _hgrn_chunk(qh, kb, ib_ref[rs, :], logf, st, tril)
        o_ref[rs, :] = _hgrn_out(o, go_ref[...], gb_ref[rs, :])
    st_ref[...] = st

    @pl.when(tb == pl.num_programs(2) - 1)
    def _():
        s_ref[0, 0] = st.T


def _hgrn_prompt(z, batch, seq, lb, go, tril):
    nt = seq // HGRN_ROWS
    blk = lambda col: pl.BlockSpec((HGRN_ROWS, DK_B), lambda b, h, t: (b * nt + t, col + h))
    return pl.pallas_call(
        _hgrn_prompt_body,
        grid=(batch, N_HEADS_B, nt),
        in_specs=[blk(COL_QB), blk(COL_FB), blk(COL_IB), blk(COL_GB),
                  pl.BlockSpec((lb.shape[0], DK_B), lambda b, h, t: (0, h)), _full(go.shape), _full(tril.shape)],
        out_specs=[pl.BlockSpec((HGRN_ROWS, DV_B), lambda b, h, t: (b * nt + t, h)),
                   pl.BlockSpec((1, 1, DK_B, DV_B), lambda b, h, t: (b, h, 0, 0))],
        out_shape=[jax.ShapeDtypeStruct((batch * seq, VB), F32),
                   jax.ShapeDtypeStruct((batch, N_HEADS_B, DK_B, DV_B), F32)],
        scratch_shapes=[pltpu.VMEM((DV_B, DK_B), F32)],
        compiler_params=_params(("arbitrary", "arbitrary", "arbitrary")),
        name="hgrn_prompt",
    )(z, z, z, z, lb, go, tril)


def _pad_rows(x, rows):
    return jnp.concatenate([x, jnp.zeros((rows - x.shape[0], x.shape[1]), x.dtype)], axis=0)


def _hgrn_decode_body(qb_ref, fb_ref, ib_ref, gb_ref, s0_ref, lb_ref, go_ref, tril_ref, o_ref, s_ref, *, dec_seq):
    rows_n = qb_ref.shape[0]
    groups = rows_n // dec_seq
    lb = _hgrn_lower_bound(lb_ref)
    qh, kb, logf = _hgrn_gates(qb_ref[...], fb_ref[...], lb)
    ih = ib_ref[...]
    b = _dot(tril_ref[...], _pad_rows(logf, LANES), HI)[:rows_n]
    b_t = _pad_rows(b, LANES).T
    ih_pad = _pad_rows(ih, LANES)
    row = lax.broadcasted_iota(jnp.int32, (rows_n, 1), 0)
    o = jnp.zeros((rows_n, DV_B), F32)
    for e in range(groups):
        mine = (row // dec_seq) == e
        s0 = s0_ref[e, 0]
        o = o + _dot(jnp.where(mine, qh * jnp.exp(b), 0.0), s0, HI)
        for s in range(dec_seq):
            r = e * dec_seq + s
            live = mine & (row >= r)
            ex = jnp.exp(jnp.where(live, b - b[r:r + 1], 0.0))
            a = jnp.sum(qh * ex * kb[r:r + 1], axis=-1, keepdims=True)
            o = o + jnp.where(live, a, 0.0) * ih[r:r + 1]
        r_last = (e + 1) * dec_seq - 1
        last = b[r_last:r_last + 1]
        kd = jnp.where(mine, kb * jnp.exp(jnp.minimum(last - b, 0.0)), 0.0)
        s_ref[e, 0] = s0 * jnp.exp(b_t[:, r_last:r_last + 1]) + _dot(_pad_rows(kd, LANES).T, ih_pad, HI)
    o_ref[...] = _hgrn_out(o, go_ref[...], gb_ref[...])


def _hgrn_decode(z, row0, dec_batch, dec_seq, s0, lb, go, tril):
    groups = 8 // dec_seq
    rows_n = groups * dec_seq
    r0 = row0 // rows_n
    blk = lambda col: pl.BlockSpec((rows_n, DK_B), lambda i, h: (r0 + i, col + h))
    st = pl.BlockSpec((groups, 1, DK_B, DV_B), lambda i, h: (i, h, 0, 0))
    return pl.pallas_call(
        functools.partial(_hgrn_decode_body, dec_seq=dec_seq),
        grid=(dec_batch // groups, N_HEADS_B),
        in_specs=[blk(COL_QB), blk(COL_FB), blk(COL_IB), blk(COL_GB), st,
                  pl.BlockSpec((lb.shape[0], DK_B), lambda i, h: (0, h)), _full(go.shape), _full(tril.shape)],
        out_specs=[pl.BlockSpec((rows_n, DV_B), lambda i, h: (i, h)), st],
        out_shape=[jax.ShapeDtypeStruct((dec_batch * dec_seq, VB), F32),
                   jax.ShapeDtypeStruct(s0.shape, F32)],
        compiler_params=_params(("arbitrary", "arbitrary")),
        name="hgrn_decode",
    )(z, z, z, z, s0, lb, go, tril)


def _conv_prompt_body(cur_ref, prev_ref, w_ref, b_ref, c_ref, ext_ref):
    t = pl.program_id(1)
    ext_ref[:CONV_HALO, :] = jnp.where(t > 0, prev_ref[...], 0.0)
    ext_ref[CONV_HALO:, :] = cur_ref[...]
    lead = CONV_HALO - (CONV_W - 1)
    chunk = 128

    def cols(ci, carry):
        cs = pl.ds(pl.multiple_of(ci * LANES, LANES), LANES)
        for r in range(CONV_ROWS // chunk):
            acc = jnp.zeros((chunk, LANES), F32) + b_ref[:, cs]
            for w in range(CONV_W):
                acc = acc + ext_ref[pl.ds(r * chunk + lead + w, chunk), cs] * w_ref[pl.ds(w, 1), cs]
            c_ref[pl.ds(r * chunk, chunk), cs] = acc
        return carry

    lax.fori_loop(0, D_MODEL // LANES, cols, 0)


def _conv_prompt(u, batch, seq, w, b):
    nt = seq // CONV_ROWS
    per = CONV_ROWS // CONV_HALO
    return pl.pallas_call(
        _conv_prompt_body,
        grid=(batch, nt),
        in_specs=[pl.BlockSpec((CONV_ROWS, D_MODEL), lambda bi, t: (bi * nt + t, 0)),
                  pl.BlockSpec((CONV_HALO, D_MODEL), lambda bi, t: (jnp.maximum((bi * nt + t) * per - 1, 0), 0)),
                  _full(w.shape), _full(b.shape)],
        out_specs=pl.BlockSpec((CONV_ROWS, D_MODEL), lambda bi, t: (bi * nt + t, 0)),
        out_shape=jax.ShapeDtypeStruct((batch * seq, D_MODEL), F32),
        scratch_shapes=[pltpu.VMEM((CONV_HALO + CONV_ROWS, D_MODEL), F32)],
        compiler_params=_params(("arbitrary", "arbitrary")),
        name="conv_prompt",
    )(u, u, w, b)


def _conv_decode_body(u_ref, st_ref, w_ref, wshift_ref, b_ref, c_ref, *, dec_seq):
    for e in range(DEC_GROUP):
        past = st_ref[e]
        for t in range(dec_seq):
            acc = jnp.sum(past * wshift_ref[t], axis=0, keepdims=True) + b_ref[...]
            for t2 in range(t + 1):
                wi = CONV_W - 1 - t + t2
                acc = acc + u_ref[e * dec_seq + t2:e * dec_seq + t2 + 1, :] * w_ref[wi:wi + 1, :]
            c_ref[e * dec_seq + t:e * dec_seq + t + 1, :] = acc


def _conv_decode(u, row0, dec_batch, dec_seq, state, w, wshift, b):
    rows_n = DEC_GROUP * dec_seq
    r0 = row0 // rows_n
    return pl.pallas_call(
        functools.partial(_conv_decode_body, dec_seq=dec_seq),
        grid=(dec_batch // DEC_GROUP,),
        in_specs=[pl.BlockSpec((rows_n, D_MODEL), lambda i: (r0 + i, 0)),
                  pl.BlockSpec((DEC_GROUP, CONV_W - 1, D_MODEL), lambda i: (i, 0, 0)),
                  _full(w.shape), _full(wshift.shape), _full(b.shape)],
        out_specs=pl.BlockSpec((rows_n, D_MODEL), lambda i: (i, 0)),
        out_shape=jax.ShapeDtypeStruct((dec_batch * dec_seq, D_MODEL), F32),
        compiler_params=_params(("arbitrary",)),
        name="conv_decode",
    )(u, state, w, wshift, b)


def _extract_top(s, idx, count, none_rank):
    work = s
    rank = jnp.full(s.shape, none_rank, F32)
    vals = []
    for r in range(count):
        m = jnp.max(work, axis=0, keepdims=True)
        first = jnp.min(jnp.where(work == m, idx, np.float32(1e9)), axis=0, keepdims=True)
        hit = idx == first
        rank = jnp.where(hit, np.float32(r), rank)
        work = jnp.where(hit, -jnp.inf, work)
        vals.append(m)
    return rank, vals


_PEER_CAND = [(a, b) for a in range(PEER_TOPK) for b in range(PEER_TOPK) if (a + 1) * (b + 1) <= PEER_TOPK]
_PEER_CAND_ROWS = -(-len(_PEER_CAND) // 8) * 8


def _peer_select_body(q_ref, keys_ref, rank_ref, qe_ref, lq_ref, pe_ref):
    ts = q_ref.shape[0]
    kidx = lax.broadcasted_iota(jnp.int32, (N_KEYS, ts), 0).astype(F32)
    cidx = lax.broadcasted_iota(jnp.int32, (_PEER_CAND_ROWS, ts), 0).astype(F32)
    pad = jnp.full((_PEER_CAND_ROWS, ts), -jnp.inf, F32)

    def head(h, carry):
        c0 = pl.multiple_of(h * 2 * PEER_DKH, 2 * PEER_DKH)
        q0 = q_ref[:, pl.ds(c0, PEER_DKH)].astype(BF16)
        q1 = q_ref[:, pl.ds(c0 + PEER_DKH, PEER_DKH)].astype(BF16)
        s0 = _dot_nt(keys_ref[h, 0], q0)
        s1 = _dot_nt(keys_ref[h, 1], q1)
        rank0, v0 = _extract_top(s0, kidx, PEER_TOPK, PEER_TOPK)
        rank1, v1 = _extract_top(s1, kidx, PEER_TOPK, PEER_TOPK)
        cand = pad
        for ci, (a, b) in enumerate(_PEER_CAND):
            cand = jnp.where(cidx == ci, v0[a] + v1[b], cand)
        crank, _ = _extract_top(cand, cidx, PEER_TOPK, PEER_TOPK)
        picked = jnp.where(crank < PEER_TOPK, 1.0, 0.0)
        e = picked * jnp.exp(cand - (v0[0] + v1[0]))
        z = jnp.sum(e, axis=0, keepdims=True)
        lq = jnp.zeros((N_KEYS, ts), F32)
        for a in range(PEER_TOPK):
            cnt = jnp.zeros((1, ts), F32)
            for ci, (ca, _) in enumerate(_PEER_CAND):
                if ca == a:
                    cnt = cnt + picked[ci:ci + 1]
            lq = jnp.where(rank0 == a, cnt, lq)
        rank_ref[h] = rank1
        qe_ref[h] = jnp.exp(s1 - v1[0])
        lq_ref[h] = lq
        pe_ref[h] = jnp.exp(s0 - v0[0]) / z
        return carry

    lax.fori_loop(0, PEER_HEADS, head, 0)


def _peer_select(q, keys):
    n = q.shape[0]
    out = jax.ShapeDtypeStruct((PEER_HEADS, N_KEYS, n), F32)
    ospec = pl.BlockSpec((PEER_HEADS, N_KEYS, SEL_TILE), lambda i: (0, 0, i))
    return pl.pallas_call(
        _peer_select_body,
        grid=(n // SEL_TILE,),
        in_specs=[pl.BlockSpec((SEL_TILE, q.shape[1]), lambda i: (i, 0)), _full(keys.shape)],
        out_specs=[ospec] * 4,
        out_shape=[out] * 4,
        compiler_params=_params(("arbitrary",)),
        name="peer_select",
    )(q, keys)


def _peer_dense_body(ht_ref, u_ref, vt_ref, rank_ref, qe_ref, lq_ref, pe_ref, x_ref, o_ref,
                     acc_ref, a_ref, w_ref):
    step = pl.program_id(1)

    @pl.when(step == 0)
    def _():
        acc_ref[...] = jnp.zeros_like(acc_ref)

    a_ref[...] = _dot(u_ref[...], ht_ref[...])

    def cols(ci, carry):
        cs = pl.ds(pl.multiple_of(ci * LANES, LANES), LANES)
        for il in range(PEER_ROWS):
            rs = pl.ds(il * N_KEYS, N_KEYS)
            gate = jnp.zeros((N_KEYS, LANES), F32)
            for h in range(PEER_HEADS):
                take = rank_ref[h, :, cs] < lq_ref[h, pl.ds(il, 1), cs]
                gate = gate + jnp.where(take, qe_ref[h, :, cs], 0.0) * pe_ref[h, pl.ds(il, 1), cs]
            w_ref[rs, cs] = (gate * _gelu(a_ref[rs, cs])).astype(BF16)
        return carry

    lax.fori_loop(0, TOKEN_TILE // LANES, cols, 0)
    acc_ref[...] += _dot(vt_ref[...], w_ref[...])

    @pl.when(step == pl.num_programs(1) - 1)
    def _():
        o_ref[...] = x_ref[...] + acc_ref[...].T


def _peer_dense(ht, u, vt, rank, qe, lq, pe, x):
    n = x.shape[0]
    ex = PEER_ROWS * N_KEYS
    steps = N_KEYS // PEER_ROWS
    tok = pl.BlockSpec((PEER_HEADS, N_KEYS, TOKEN_TILE), lambda t, e: (0, 0, t))
    rowsel = pl.BlockSpec((PEER_HEADS, PEER_ROWS, TOKEN_TILE), lambda t, e: (0, e, t))
    return pl.pallas_call(
        _peer_dense_body,
        grid=(n // TOKEN_TILE, steps),
        in_specs=[pl.BlockSpec((D_MODEL, TOKEN_TILE), lambda t, e: (0, t)),
                  pl.BlockSpec((ex, D_MODEL), lambda t, e: (e, 0)),
                  pl.BlockSpec((D_MODEL, ex), lambda t, e: (0, e)),
                  tok, tok, rowsel, rowsel,
                  pl.BlockSpec((TOKEN_TILE, D_MODEL), lambda t, e: (t, 0))],
        out_specs=pl.BlockSpec((TOKEN_TILE, D_MODEL), lambda t, e: (t, 0)),
        out_shape=jax.ShapeDtypeStruct((n, D_MODEL), F32),
        scratch_shapes=[pltpu.VMEM((D_MODEL, TOKEN_TILE), F32), pltpu.VMEM((ex, TOKEN_TILE), F32),
                        pltpu.VMEM((ex, TOKEN_TILE), BF16)],
        compiler_params=_params(("arbitrary", "arbitrary")),
        name="peer_dense",
    )(ht, u, vt, rank, qe, lq, pe, x)


def _peer(x1, ht, q, keys, u, vt):
    rank, qe, lq, pe = _peer_select(q, keys)
    return _peer_dense(ht, u, vt, rank, qe, lq, pe, x1)


def _block_diag_mean(width):
    idx = np.arange(width) // HEAD_DIM
    return jnp.asarray((idx[:, None] == idx[None, :]).astype(np.float32) / HEAD_DIM)


def kernel(x_prompt, x_sample, cache_swa_k, cache_swa_v, state_hgrn, state_conv, norm_mix_g, norm_ffn_g, w_in0, attn_q_norm_g, attn_k_norm_g, attn_sinks, hgrn_lb, hgrn_o_norm_g, w_out0, conv_w_pw1, conv_b_pw1, conv_w_dw, conv_b_dw, conv_ln_g, conv_ln_b, conv_w_pw2, peer_w_q, peer_keys, peer_u, peer_v):
    batch, seq, _ = x_prompt.shape
    dec_batch, dec_seq, _ = x_sample.shape
    n_p = batch * seq
    n_s = dec_batch * dec_seq
    assert seq % CONV_ROWS == 0 and seq % HGRN_ROWS == 0 and n_p % TOKEN_TILE == 0 and n_s % TOKEN_TILE == 0
    assert dec_batch % DEC_GROUP == 0 and 8 % dec_seq == 0 and dec_seq <= CONV_W - 1

    x = jnp.concatenate([x_prompt.reshape(n_p, D_MODEL), x_sample.reshape(n_s, D_MODEL)], axis=0)
    row = lambda v: v.reshape(1, -1).astype(F32)

    z = _in0(x, row(norm_mix_g[0]), w_in0.astype(BF16))
    d = np.arange(LANES) % HEAD_DIM
    inv = ROPE_THETA ** (-jnp.arange(0, ROT_DIM, 2, dtype=F32) / ROT_DIM)
    inv_lane = jnp.where(jnp.asarray(d < ROT_DIM), inv[d % (ROT_DIM // 2)], 0.0).reshape(1, LANES)
    gq = row(jnp.tile(attn_q_norm_g, N_HEADS_A))
    gk = row(jnp.tile(attn_k_norm_g, N_KV_A))
    segq, segk = _block_diag_mean(QA), _block_diag_mean(KVA)
    sinks = attn_sinks.astype(F32)
    attn_p, k_win_p, v_win_p = _attn_prompt(z, batch, seq, gq, gk, inv_lane, segq, segk, sinks)
    kc = cache_swa_k.reshape(dec_batch * WINDOW, KVA)
    vc = cache_swa_v.reshape(dec_batch * WINDOW, KVA)
    attn_s, k_new, v_new = _attn_decode(z, n_p, dec_batch, dec_seq, kc, vc, gq, gk, inv_lane, segq, segk, sinks)

    go = row(hgrn_o_norm_g)
    tril = jnp.asarray(np.tril(np.ones((HGRN_CHUNK, HGRN_CHUNK), np.float32)))
    hg_p, s_p = _hgrn_prompt(z, batch, seq, hgrn_lb, go, tril)
    r = np.arange(LANES)
    tril_dec = jnp.asarray(((r[:, None] // dec_seq == r[None, :] // dec_seq) & (r[None, :] <= r[:, None]))
                           .astype(np.float32))
    hg_s, s_s = _hgrn_decode(z, n_p, dec_batch, dec_seq, state_hgrn, hgrn_lb, go, tril_dec)

    attn = jnp.concatenate([attn_p, attn_s], axis=0)
    hg = jnp.concatenate([hg_p, hg_s], axis=0)
    x1, ht, q = _mix_out(_out0_body, [attn, hg], x, [w_out0.astype(BF16)], row(norm_ffn_g[0]),
                         peer_w_q[0].astype(BF16), "mix0_out")
    x2 = _peer(x1, ht, q, peer_keys[0].astype(BF16), peer_u[0].astype(BF16), peer_v[0].astype(BF16).T)

    u = _glu(x2, row(norm_mix_g[1]), conv_w_pw1.astype(BF16), row(conv_b_pw1))
    wdw = jnp.concatenate([conv_w_dw, jnp.zeros((1, D_MODEL), F32)], axis=0)
    bdw = row(conv_b_dw)
    c_p = _conv_prompt(u, batch, seq, wdw, bdw)
    wshift = jnp.stack([jnp.concatenate([jnp.zeros((t, D_MODEL), F32), conv_w_dw[:CONV_W - 1 - t]], axis=0)
                        for t in range(dec_seq)])
    c_s = _conv_decode(u, n_p, dec_batch, dec_seq, state_conv, wdw, wshift, bdw)
    c = jnp.concatenate([c_p, c_s], axis=0)
    x3, ht, q = _mix_out(_post1_body, [c], x2, [row(conv_ln_g), row(conv_ln_b), conv_w_pw2.astype(BF16)],
                         row(norm_ffn_g[1]), peer_w_q[1].astype(BF16), "mix1_out")
    x4 = _peer(x3, ht, q, peer_keys[1].astype(BF16), peer_u[1].astype(BF16), peer_v[1].astype(BF16).T)

    kv = lambda t: t.reshape(t.shape[0], WINDOW, N_KV_A, HEAD_DIM)
    k_win_s = jnp.concatenate([cache_swa_k[:, dec_seq:], k_new.reshape(dec_batch, dec_seq, N_KV_A, HEAD_DIM)], axis=1)
    v_win_s = jnp.concatenate([cache_swa_v[:, dec_seq:], v_new.reshape(dec_batch, dec_seq, N_KV_A, HEAD_DIM)], axis=1)
    u_p = u[:n_p].reshape(batch, seq, D_MODEL)
    u_s = u[n_p:].reshape(dec_batch, dec_seq, D_MODEL)
    conv_buf_p = u_p[:, seq - (CONV_W - 1):]
    conv_buf_s = jnp.concatenate([state_conv[:, dec_seq:], u_s], axis=1)
    return (x4[:n_p].reshape(batch, seq, D_MODEL), x4[n_p:].reshape(dec_batch, dec_seq, D_MODEL),
            kv(k_win_p), kv(v_win_p), s_p, conv_buf_p, k_win_s, v_win_s, s_s, conv_buf_s)
```

```python
import functools

import numpy as np
import jax
import jax.numpy as jnp
from jax import lax
from jax.experimental import pallas as pl
from jax.experimental.pallas import tpu as pltpu

F32 = jnp.float32
BF16 = jnp.bfloat16
HI = lax.Precision.HIGHEST

D_MODEL = 1024
PAST_LEN = 8192
HEAD_DIM = 64
N_HEADS_A = 8
N_KV_A = 2
KV_REP = N_HEADS_A // N_KV_A
WINDOW = 128
ROT_DIM = HEAD_DIM // 4
ROPE_THETA = 500000.0
ATTN_SCALE = HEAD_DIM ** -0.5
NEG_INF = -1e30
N_HEADS_B = 4
DK_B = 128
DV_B = 128
CONV_W = 31
N_KEYS = 128
PEER_HEADS = 8
PEER_TOPK = 16
PEER_DKH = 128
NORM_EPS = 1e-6

QA = N_HEADS_A * HEAD_DIM
KVA = N_KV_A * HEAD_DIM
QB = N_HEADS_B * DK_B
VB = N_HEADS_B * DV_B
IN0_WIDTH = QA + 2 * KVA + 2 * QB + 2 * VB
COL_K = QA // 128
COL_V = (QA + KVA) // 128
COL_QB = (QA + 2 * KVA) // 128
COL_FB = COL_QB + QB // 128
COL_IB = COL_FB + QB // 128
COL_GB = COL_IB + VB // 128

LANES = 128
TOKEN_TILE = 512
SEL_TILE = 256
HGRN_CHUNK = 64
HGRN_SUB = 16
HGRN_ROWS = 256
CONV_ROWS = 512
CONV_HALO = 32
PEER_ROWS = 16
BF16_ROWS = 16
DEC_GROUP = 8
VMEM_LIMIT = 48 * 1024 * 1024


def _dot(a, b, prec=None):
    return jnp.dot(a, b, preferred_element_type=F32, precision=prec)


def _dot_nt(a, b, prec=None):
    return lax.dot_general(a, b, (((1,), (1,)), ((), ())), preferred_element_type=F32, precision=prec)


def _rms(x, g):
    return x * lax.rsqrt(jnp.mean(x * x, axis=-1, keepdims=True) + NORM_EPS) * g


def _silu(x):
    return x * jax.nn.sigmoid(x)


def _gelu(x):
    return 0.5 * x * (1.0 + lax.erf(x * np.float32(0.7071067811865476)))


def _params(sem):
    return pltpu.CompilerParams(dimension_semantics=sem, vmem_limit_bytes=VMEM_LIMIT)


def _full(shape):
    n = len(shape)
    return pl.BlockSpec(shape, lambda *_: (0,) * n)


def _in0_body(x_ref, g_ref, w_ref, z_ref):
    h = _rms(x_ref[...], g_ref[...]).astype(BF16)
    z_ref[...] = _dot(h, w_ref[...])


def _in0(x, g, w):
    n = x.shape[0]
    width = w.shape[1]
    return pl.pallas_call(
        _in0_body,
        grid=(n // TOKEN_TILE,),
        in_specs=[pl.BlockSpec((TOKEN_TILE, D_MODEL), lambda i: (i, 0)), _full((1, D_MODEL)),
                  _full((D_MODEL, width))],
        out_specs=pl.BlockSpec((TOKEN_TILE, width), lambda i: (i, 0)),
        out_shape=jax.ShapeDtypeStruct((n, width), F32),
        compiler_params=_params(("arbitrary",)),
        name="in0_proj",
    )(x, g, w)


def _glu_body(x_ref, g_ref, w_ref, b_ref, u_ref):
    h = _rms(x_ref[...], g_ref[...]).astype(BF16)
    a = _dot(h, w_ref[...]) + b_ref[...]
    u_ref[...] = a[:, :D_MODEL] * jax.nn.sigmoid(a[:, D_MODEL:])


def _glu(x, g, w, b):
    n = x.shape[0]
    return pl.pallas_call(
        _glu_body,
        grid=(n // TOKEN_TILE,),
        in_specs=[pl.BlockSpec((TOKEN_TILE, D_MODEL), lambda i: (i, 0)), _full((1, D_MODEL)),
                  _full((D_MODEL, 2 * D_MODEL)), _full((1, 2 * D_MODEL))],
        out_specs=pl.BlockSpec((TOKEN_TILE, D_MODEL), lambda i: (i, 0)),
        out_shape=jax.ShapeDtypeStruct((n, D_MODEL), F32),
        compiler_params=_params(("arbitrary",)),
        name="conv_glu",
    )(x, g, w, b)


def _ffn_query(x1, gf_ref, wq_ref, x1_ref, ht_ref, q_ref):
    x1_ref[...] = x1
    h2 = _rms(x1, gf_ref[...])
    ht_ref[...] = h2.T.astype(BF16)
    q_ref[...] = _dot(h2.astype(BF16), wq_ref[...])


def _out0_body(a_ref, hg_ref, x_ref, w_ref, gf_ref, wq_ref, x1_ref, ht_ref, q_ref):
    m = _dot(a_ref[...].astype(BF16), w_ref[:QA, :]) + _dot(hg_ref[...].astype(BF16), w_ref[QA:, :])
    _ffn_query(x_ref[...] + m, gf_ref, wq_ref, x1_ref, ht_ref, q_ref)


def _post1_body(c_ref, x_ref, lg_ref, lb_ref, w_ref, gf_ref, wq_ref, x1_ref, ht_ref, q_ref):
    c = c_ref[...]
    mu = jnp.mean(c, axis=-1, keepdims=True)
    d = c - mu
    var = jnp.mean(d * d, axis=-1, keepdims=True)
    ln = d * lax.rsqrt(var + NORM_EPS) * lg_ref[...] + lb_ref[...]
    y = _dot(_silu(ln).astype(BF16), w_ref[...])
    _ffn_query(x_ref[...] + y, gf_ref, wq_ref, x1_ref, ht_ref, q_ref)


def _mix_out(body, acts, x, consts, gf, wq, name):
    n = x.shape[0]
    qw = wq.shape[1]
    row = lambda w: pl.BlockSpec((TOKEN_TILE, w), lambda i: (i, 0))
    return pl.pallas_call(
        body,
        grid=(n // TOKEN_TILE,),
        in_specs=[row(a.shape[1]) for a in acts] + [row(D_MODEL)] + [_full(c.shape) for c in consts]
        + [_full(gf.shape), _full(wq.shape)],
        out_specs=[row(D_MODEL), pl.BlockSpec((D_MODEL, TOKEN_TILE), lambda i: (0, i)), row(qw)],
        out_shape=[jax.ShapeDtypeStruct((n, D_MODEL), F32), jax.ShapeDtypeStruct((D_MODEL, n), BF16),
                   jax.ShapeDtypeStruct((n, qw), F32)],
        compiler_params=_params(("arbitrary",)),
        name=name,
    )(*acts, x, *consts, gf, wq)


def _head_norm(x, g, seg):
    ms = _dot(x * x, seg, HI)
    return x * lax.rsqrt(ms + NORM_EPS) * g


def _rope(x, cos, sin, first_half):
    half = ROT_DIM // 2
    width = x.shape[1]
    up = pltpu.roll(x, width - half, axis=1)
    dn = pltpu.roll(x, half, axis=1)
    return x * cos + jnp.where(first_half, -up, dn) * sin


def _rope_tables(pos, inv_ref, reps):
    ang = pos * inv_ref[...]
    c, s = jnp.cos(ang), jnp.sin(ang)
    if reps > 1:
        c, s = (jnp.concatenate([t] * reps, axis=1) for t in (c, s))
    lane = lax.broadcasted_iota(jnp.int32, c.shape, 1)
    return c, s, (lane % HEAD_DIM) < (ROT_DIM // 2)


def _stack_heads(x, g):
    return jnp.concatenate(
        [x[:, (g * KV_REP + r) * HEAD_DIM:(g * KV_REP + r + 1) * HEAD_DIM] for r in range(KV_REP)], axis=0)


def _attn_prompt_body(q_ref, k_ref, v_ref, gq_ref, gk_ref, inv_ref, segq_ref, segk_ref, sink_ref,
                      o_ref, kw_ref, vw_ref, kprev_ref, vprev_ref):
    j = pl.program_id(1)
    w = WINDOW

    @pl.when(j == 0)
    def _():
        kprev_ref[...] = jnp.zeros_like(kprev_ref)
        vprev_ref[...] = jnp.zeros_like(vprev_ref)

    rows = lax.broadcasted_iota(jnp.int32, (w, LANES), 0)
    pos = (j * w + rows).astype(F32)
    ck, sk, fk = _rope_tables(pos, inv_ref, 1)
    cq, sq, fq = _rope_tables(pos, inv_ref, QA // LANES)
    q = _rope(_head_norm(q_ref[...], gq_ref[...], segq_ref[...]), cq, sq, fq)
    k = _rope(_head_norm(k_ref[...], gk_ref[...], segk_ref[...]), ck, sk, fk)
    v = v_ref[...]
    kp = kprev_ref[...]
    vp = vprev_ref[...]

    qi = lax.broadcasted_iota(jnp.int32, (KV_REP * w, w), 0) % w
    ki = lax.broadcasted_iota(jnp.int32, (KV_REP * w, w), 1)
    m_own = ki <= qi
    m_prev = (ki > qi) & (j > 0)
    rep = lax.broadcasted_iota(jnp.int32, (KV_REP * w, 1), 0) // w
    for g in range(N_KV_A):
        sl = slice(g * HEAD_DIM, (g + 1) * HEAD_DIM)
        qg = _stack_heads(q, g).astype(BF16)
        s_own = jnp.where(m_own, _dot_nt(qg, k[:, sl].astype(BF16)) * ATTN_SCALE, NEG_INF)
        s_prev = jnp.where(m_prev, _dot_nt(qg, kp[:, sl].astype(BF16)) * ATTN_SCALE, NEG_INF)
        sink = jnp.zeros((KV_REP * w, 1), F32)
        for r in range(KV_REP):
            sink = jnp.where(rep == r, sink_ref[g * KV_REP + r], sink)
        mx = jnp.maximum(jnp.maximum(jnp.max(s_own, axis=-1, keepdims=True),
                                     jnp.max(s_prev, axis=-1, keepdims=True)), sink)
        e_own = jnp.exp(s_own - mx)
        e_prev = jnp.exp(s_prev - mx)
        den = (jnp.sum(e_own, axis=-1, keepdims=True) + jnp.sum(e_prev, axis=-1, keepdims=True)
               + jnp.exp(sink - mx))
        o = (_dot(e_own.astype(BF16), v[:, sl].astype(BF16))
             + _dot(e_prev.astype(BF16), vp[:, sl].astype(BF16))) / den
        for r in range(KV_REP):
            hq = g * KV_REP + r
            o_ref[:, hq * HEAD_DIM:(hq + 1) * HEAD_DIM] = o[r * w:(r + 1) * w]

    kprev_ref[...] = k
    vprev_ref[...] = v

    @pl.when(j == pl.num_programs(1) - 1)
    def _():
        kw_ref[0] = k
        vw_ref[0] = v


def _attn_prompt(z, batch, seq, gq, gk, inv, segq, segk, sinks):
    nb = seq // WINDOW
    blk = lambda width, col: pl.BlockSpec((WINDOW, width), lambda b, j: (b * nb + j, col))
    return pl.pallas_call(
        _attn_prompt_body,
        grid=(batch, nb),
        in_specs=[blk(QA, 0), blk(KVA, COL_K), blk(KVA, COL_V), _full(gq.shape), _full(gk.shape),
                  _full(inv.shape), _full(segq.shape), _full(segk.shape),
                  pl.BlockSpec(memory_space=pltpu.SMEM)],
        out_specs=[pl.BlockSpec((WINDOW, QA), lambda b, j: (b * nb + j, 0)),
                   pl.BlockSpec((1, WINDOW, KVA), lambda b, j: (b, 0, 0)),
                   pl.BlockSpec((1, WINDOW, KVA), lambda b, j: (b, 0, 0))],
        out_shape=[jax.ShapeDtypeStruct((batch * seq, QA), F32),
                   jax.ShapeDtypeStruct((batch, WINDOW, KVA), F32),
                   jax.ShapeDtypeStruct((batch, WINDOW, KVA), F32)],
        scratch_shapes=[pltpu.VMEM((WINDOW, KVA), F32), pltpu.VMEM((WINDOW, KVA), F32)],
        compiler_params=_params(("arbitrary", "arbitrary")),
        name="swa_prompt",
    )(z, z, z, gq, gk, inv, segq, segk, sinks)


def _attn_decode_body(q_ref, k_ref, v_ref, kc_ref, vc_ref, gq_ref, gk_ref, inv_ref, segq_ref, segk_ref,
                      sink_ref, o_ref, kn_ref, vn_ref, *, dec_seq):
    rows_n = DEC_GROUP * dec_seq
    rows = lax.broadcasted_iota(jnp.int32, (rows_n, LANES), 0)
    pos = (PAST_LEN + rows % dec_seq).astype(F32)
    ck, sk, fk = _rope_tables(pos, inv_ref, 1)
    cq, sq, fq = _rope_tables(pos, inv_ref, QA // LANES)
    q = _rope(_head_norm(q_ref[...], gq_ref[...], segq_ref[...]), cq, sq, fq)
    k = _rope(_head_norm(k_ref[...], gk_ref[...], segk_ref[...]), ck, sk, fk)
    v = v_ref[...]
    kn_ref[...] = k
    vn_ref[...] = v

    nq = KV_REP * rows_n
    nc = DEC_GROUP * WINDOW

    def qrow(shape):
        r = lax.broadcasted_iota(jnp.int32, shape, 0) % rows_n
        return r // dec_seq, r % dec_seq

    bq, tq = qrow((nq, nc))
    col = lax.broadcasted_iota(jnp.int32, (nq, nc), 1)
    m_cache = (col // WINDOW == bq) & (col % WINDOW > tq)
    bq, tq = qrow((nq, rows_n))
    col = lax.broadcasted_iota(jnp.int32, (nq, rows_n), 1)
    m_new = (col // dec_seq == bq) & (col % dec_seq <= tq)
    rep = lax.broadcasted_iota(jnp.int32, (nq, 1), 0) // rows_n
    for g in range(N_KV_A):
        sl = slice(g * HEAD_DIM, (g + 1) * HEAD_DIM)
        qg = _stack_heads(q, g).astype(BF16)
        s_c = jnp.where(m_cache, _dot_nt(qg, kc_ref[:, sl].astype(BF16)) * ATTN_SCALE, NEG_INF)
        s_n = jnp.where(m_new, _dot_nt(qg, k[:, sl].astype(BF16)) * ATTN_SCALE, NEG_INF)
        sink = jnp.zeros((nq, 1), F32)
        for r in range(KV_REP):
            sink = jnp.where(rep == r, sink_ref[g * KV_REP + r], sink)
        mx = jnp.maximum(jnp.maximum(jnp.max(s_c, axis=-1, keepdims=True),
                                     jnp.max(s_n, axis=-1, keepdims=True)), sink)
        e_c = jnp.exp(s_c - mx)
        e_n = jnp.exp(s_n - mx)
        den = jnp.sum(e_c, axis=-1, keepdims=True) + jnp.sum(e_n, axis=-1, keepdims=True) + jnp.exp(sink - mx)
        o = (_dot(e_c.astype(BF16), vc_ref[:, sl].astype(BF16))
             + _dot(e_n.astype(BF16), v[:, sl].astype(BF16))) / den
        for r in range(KV_REP):
            hq = g * KV_REP + r
            o_ref[:, hq * HEAD_DIM:(hq + 1) * HEAD_DIM] = o[r * rows_n:(r + 1) * rows_n]


def _attn_decode(z, row0, dec_batch, dec_seq, kc, vc, gq, gk, inv, segq, segk, sinks):
    rows_n = DEC_GROUP * dec_seq
    r0 = row0 // rows_n
    blk = lambda width, col: pl.BlockSpec((rows_n, width), lambda i: (r0 + i, col))
    cache = pl.BlockSpec((DEC_GROUP * WINDOW, KVA), lambda i: (i, 0))
    n = dec_batch * dec_seq
    return pl.pallas_call(
        functools.partial(_attn_decode_body, dec_seq=dec_seq),
        grid=(dec_batch // DEC_GROUP,),
        in_specs=[blk(QA, 0), blk(KVA, COL_K), blk(KVA, COL_V), cache, cache, _full(gq.shape),
                  _full(gk.shape), _full(inv.shape), _full(segq.shape), _full(segk.shape),
                  pl.BlockSpec(memory_space=pltpu.SMEM)],
        out_specs=[pl.BlockSpec((rows_n, QA), lambda i: (i, 0)), pl.BlockSpec((rows_n, KVA), lambda i: (i, 0)),
                   pl.BlockSpec((rows_n, KVA), lambda i: (i, 0))],
        out_shape=[jax.ShapeDtypeStruct((n, QA), F32), jax.ShapeDtypeStruct((n, KVA), F32),
                   jax.ShapeDtypeStruct((n, KVA), F32)],
        compiler_params=_params(("arbitrary",)),
        name="swa_decode",
    )(z, z, z, kc, vc, gq, gk, inv, segq, segk, sinks)


def _hgrn_lower_bound(lb_ref):
    l = lb_ref[...]
    e = jnp.exp(l - jnp.max(l, axis=0, keepdims=True))
    return e[0:1] / jnp.sum(e, axis=0, keepdims=True)


def _hgrn_gates(qb, fb, lb):
    logf = jnp.log(lb + (1.0 - lb) * jax.nn.sigmoid(fb))
    kb = (1.0 - lb) * jax.nn.sigmoid(-fb)
    return _silu(qb), kb, logf


def _hgrn_out(o, g, gate):
    return _rms(o, g) * _silu(gate)


def _hgrn_chunk(qh, kb, ih, logf, st, tril):
    c = qh.shape[0]
    b = _dot(tril, logf, HI)
    o_parts = []
    row = lax.broadcasted_iota(jnp.int32, (c, 1), 0)
    trow = lax.broadcasted_iota(jnp.int32, (HGRN_SUB, 1), 0)
    for blk in range(c // HGRN_SUB):
        t0 = blk * HGRN_SUB
        bi = b[t0:t0 + HGRN_SUB]
        qi = qh[t0:t0 + HGRN_SUB]
        if blk == 0:
            oi = _dot_nt(qi * jnp.exp(bi), st, HI)
        else:
            base = b[t0 - 1:t0]
            qd = qi * jnp.exp(bi - base)
            kd = jnp.where(row < t0, kb * jnp.exp(jnp.minimum(base - b, 0.0)), 0.0)
            oi = _dot_nt(qd * jnp.exp(base), st, HI) + _dot(_dot_nt(qd, kd, HI), ih, HI)
        for s in range(HGRN_SUB):
            live = trow >= s
            e = jnp.exp(jnp.where(live, bi - bi[s:s + 1], 0.0))
            a = jnp.sum(qi * e * kb[t0 + s:t0 + s + 1], axis=-1, keepdims=True)
            oi = oi + jnp.where(live, a, 0.0) * ih[t0 + s:t0 + s + 1]
        o_parts.append(oi)
    last = b[c - 1:c]
    st_new = st * jnp.exp(last) + _dot(ih.T, kb * jnp.exp(last - b), HI)
    return jnp.concatenate(o_parts, axis=0), st_new


def _hgrn_prompt_body(qb_ref, fb_ref, ib_ref, gb_ref, lb_ref, go_ref, tril_ref, o_ref, s_ref, st_ref):
    tb = pl.program_id(2)

    @pl.when(tb == 0)
    def _():
        st_ref[...] = jnp.zeros_like(st_ref)

    lb = _hgrn_lower_bound(lb_ref)
    tril = tril_ref[...]
    st = st_ref[...]
    for c in range(HGRN_ROWS // HGRN_CHUNK):
        rs = slice(c * HGRN_CHUNK, (c + 1) * HGRN_CHUNK)
        qh, kb, logf = _hgrn_gates(qb_ref[rs, :], fb_ref[rs, :], lb)
        o, st = _hgrn_chunk(qh, kb, ib_ref[rs, :], logf, st, tril)
        o_ref[rs, :] = _hgrn_out(o, go_ref[...], gb_ref[rs, :])
    st_ref[...] = st

    @pl.when(tb == pl.num_programs(2) - 1)
    def _():
        s_ref[0, 0] = st.T


def _hgrn_prompt(z, batch, seq, lb, go, tril):
    nt = seq // HGRN_ROWS
    blk = lambda col: pl.BlockSpec((HGRN_ROWS, DK_B), lambda b, h, t: (b * nt + t, col + h))
    return pl.pallas_call(
        _hgrn_prompt_body,
        grid=(batch, N_HEADS_B, nt),
        in_specs=[blk(COL_QB), blk(COL_FB), blk(COL_IB), blk(COL_GB),
                  pl.BlockSpec((lb.shape[0], DK_B), lambda b, h, t: (0, h)), _full(go.shape), _full(tril.shape)],
        out_specs=[pl.BlockSpec((HGRN_ROWS, DV_B), lambda b, h, t: (b * nt + t, h)),
                   pl.BlockSpec((1, 1, DK_B, DV_B), lambda b, h, t: (b, h, 0, 0))],
        out_shape=[jax.ShapeDtypeStruct((batch * seq, VB), F32),
                   jax.ShapeDtypeStruct((batch, N_HEADS_B, DK_B, DV_B), F32)],
        scratch_shapes=[pltpu.VMEM((DV_B, DK_B), F32)],
        compiler_params=_params(("arbitrary", "arbitrary", "arbitrary")),
        name="hgrn_prompt",
    )(z, z, z, z, lb, go, tril)


def _pad_rows(x, rows):
    return jnp.concatenate([x, jnp.zeros((rows - x.shape[0], x.shape[1]), x.dtype)], axis=0)


def _hgrn_decode_body(qb_ref, fb_ref, ib_ref, gb_ref, s0_ref, lb_ref, go_ref, tril_ref, o_ref, s_ref, *, dec_seq):
    rows_n = qb_ref.shape[0]
    groups = rows_n // dec_seq
    lb = _hgrn_lower_bound(lb_ref)
    qh, kb, logf = _hgrn_gates(qb_ref[...], fb_ref[...], lb)
    ih = ib_ref[...]
    b = _dot(tril_ref[...], _pad_rows(logf, LANES), HI)[:rows_n]
    b_t = _pad_rows(b, LANES).T
    ih_pad = _pad_rows(ih, LANES)
    row = lax.broadcasted_iota(jnp.int32, (rows_n, 1), 0)
    o = jnp.zeros((rows_n, DV_B), F32)
    for e in range(groups):
        mine = (row // dec_seq) == e
        s0 = s0_ref[e, 0]
        o = o + _dot(jnp.where(mine, qh * jnp.exp(b), 0.0), s0, HI)
        for s in range(dec_seq):
            r = e * dec_seq + s
            live = mine & (row >= r)
            ex = jnp.exp(jnp.where(live, b - b[r:r + 1], 0.0))
            a = jnp.sum(qh * ex * kb[r:r + 1], axis=-1, keepdims=True)
            o = o + jnp.where(live, a, 0.0) * ih[r:r + 1]
        r_last = (e + 1) * dec_seq - 1
        last = b[r_last:r_last + 1]
        kd = jnp.where(mine, kb * jnp.exp(jnp.minimum(last - b, 0.0)), 0.0)
        s_ref[e, 0] = s0 * jnp.exp(b_t[:, r_last:r_last + 1]) + _dot(_pad_rows(kd, LANES).T, ih_pad, HI)
    o_ref[...] = _hgrn_out(o, go_ref[...], gb_ref[...])


def _hgrn_decode(z, row0, dec_batch, dec_seq, s0, lb, go, tril):
    groups = 8 // dec_seq
    rows_n = groups * dec_seq
    r0 = row0 // rows_n
    blk = lambda col: pl.BlockSpec((rows_n, DK_B), lambda i, h: (r0 + i, col + h))
    st = pl.BlockSpec((groups, 1, DK_B, DV_B), lambda i, h: (i, h, 0, 0))
    return pl.pallas_call(
        functools.partial(_hgrn_decode_body, dec_seq=dec_seq),
        grid=(dec_batch // groups, N_HEADS_B),
        in_specs=[blk(COL_QB), blk(COL_FB), blk(COL_IB), blk(COL_GB), st,
                  pl.BlockSpec((lb.shape[0], DK_B), lambda i, h: (0, h)), _full(go.shape), _full(tril.shape)],
        out_specs=[pl.BlockSpec((rows_n, DV_B), lambda i, h: (i, h)), st],
        out_shape=[jax.ShapeDtypeStruct((dec_batch * dec_seq, VB), F32),
                   jax.ShapeDtypeStruct(s0.shape, F32)],
        compiler_params=_params(("arbitrary", "arbitrary")),
        name="hgrn_decode",
    )(z, z, z, z, s0, lb, go, tril)


def _conv_prompt_body(cur_ref, prev_ref, w_ref, b_ref, c_ref, ext_ref):
    t = pl.program_id(1)
    ext_ref[:CONV_HALO, :] = jnp.where(t > 0, prev_ref[...], 0.0)
    ext_ref[CONV_HALO:, :] = cur_ref[...]
    lead = CONV_HALO - (CONV_W - 1)
    chunk = 128

    def cols(ci, carry):
        cs = pl.ds(pl.multiple_of(ci * LANES, LANES), LANES)
        for r in range(CONV_ROWS // chunk):
            acc = jnp.zeros((chunk, LANES), F32) + b_ref[:, cs]
            for w in range(CONV_W):
                acc = acc + ext_ref[pl.ds(r * chunk + lead + w, chunk), cs] * w_ref[pl.ds(w, 1), cs]
            c_ref[pl.ds(r * chunk, chunk), cs] = acc
        return carry

    lax.fori_loop(0, D_MODEL // LANES, cols, 0)


def _conv_prompt(u, batch, seq, w, b):
    nt = seq // CONV_ROWS
    per = CONV_ROWS // CONV_HALO
    return pl.pallas_call(
        _conv_prompt_body,
        grid=(batch, nt),
        in_specs=[pl.BlockSpec((CONV_ROWS, D_MODEL), lambda bi, t: (bi * nt + t, 0)),
                  pl.BlockSpec((CONV_HALO, D_MODEL), lambda bi, t: (jnp.maximum((bi * nt + t) * per - 1, 0), 0)),
                  _full(w.shape), _full(b.shape)],
        out_specs=pl.BlockSpec((CONV_ROWS, D_MODEL), lambda bi, t: (bi * nt + t, 0)),
        out_shape=jax.ShapeDtypeStruct((batch * seq, D_MODEL), F32),
        scratch_shapes=[pltpu.VMEM((CONV_HALO + CONV_ROWS, D_MODEL), F32)],
        compiler_params=_params(("arbitrary", "arbitrary")),
        name="conv_prompt",
    )(u, u, w, b)


def _conv_decode_body(u_ref, st_ref, w_ref, wshift_ref, b_ref, c_ref, *, dec_seq):
    for e in range(DEC_GROUP):
        past = st_ref[e]
        for t in range(dec_seq):
            acc = jnp.sum(past * wshift_ref[t], axis=0, keepdims=True) + b_ref[...]
            for t2 in range(t + 1):
                wi = CONV_W - 1 - t + t2
                acc = acc + u_ref[e * dec_seq + t2:e * dec_seq + t2 + 1, :] * w_ref[wi:wi + 1, :]
            c_ref[e * dec_seq + t:e * dec_seq + t + 1, :] = acc


def _conv_decode(u, row0, dec_batch, dec_seq, state, w, wshift, b):
    rows_n = DEC_GROUP * dec_seq
    r0 = row0 // rows_n
    return pl.pallas_call(
        functools.partial(_conv_decode_body, dec_seq=dec_seq),
        grid=(dec_batch // DEC_GROUP,),
        in_specs=[pl.BlockSpec((rows_n, D_MODEL), lambda i: (r0 + i, 0)),
                  pl.BlockSpec((DEC_GROUP, CONV_W - 1, D_MODEL), lambda i: (i, 0, 0)),
                  _full(w.shape), _full(wshift.shape), _full(b.shape)],
        out_specs=pl.BlockSpec((rows_n, D_MODEL), lambda i: (i, 0)),
        out_shape=jax.ShapeDtypeStruct((dec_batch * dec_seq, D_MODEL), F32),
        compiler_params=_params(("arbitrary",)),
        name="conv_decode",
    )(u, state, w, wshift, b)


def _extract_top(s, idx, count, none_rank):
    work = s
    rank = jnp.full(s.shape, none_rank, F32)
    vals = []
    for r in range(count):
        m = jnp.max(work, axis=0, keepdims=True)
        first = jnp.min(jnp.where(work == m, idx, np.float32(1e9)), axis=0, keepdims=True)
        hit = idx == first
        rank = jnp.where(hit, np.float32(r), rank)
        work = jnp.where(hit, -jnp.inf, work)
        vals.append(m)
    return rank, vals


_PEER_CAND = [(a, b) for a in range(PEER_TOPK) for b in range(PEER_TOPK) if (a + 1) * (b + 1) <= PEER_TOPK]
_PEER_CAND_ROWS = -(-len(_PEER_CAND) // 8) * 8


def _peer_select_body(q_ref, keys_ref, rank_ref, qe_ref, lq_ref, pe_ref):
    ts = q_ref.shape[0]
    kidx = lax.broadcasted_iota(jnp.int32, (N_KEYS, ts), 0).astype(F32)
    cidx = lax.broadcasted_iota(jnp.int32, (_PEER_CAND_ROWS, ts), 0).astype(F32)
    pad = jnp.full((_PEER_CAND_ROWS, ts), -jnp.inf, F32)

    def head(h, carry):
        c0 = pl.multiple_of(h * 2 * PEER_DKH, 2 * PEER_DKH)
        q0 = q_ref[:, pl.ds(c0, PEER_DKH)].astype(BF16)
        q1 = q_ref[:, pl.ds(c0 + PEER_DKH, PEER_DKH)].astype(BF16)
        s0 = _dot_nt(keys_ref[h, 0], q0)
        s1 = _dot_nt(keys_ref[h, 1], q1)
        rank0, v0 = _extract_top(s0, kidx, PEER_TOPK, PEER_TOPK)
        rank1, v1 = _extract_top(s1, kidx, PEER_TOPK, PEER_TOPK)
        cand = pad
        for ci, (a, b) in enumerate(_PEER_CAND):
            cand = jnp.where(cidx == ci, v0[a] + v1[b], cand)
        crank, _ = _extract_top(cand, cidx, PEER_TOPK, PEER_TOPK)
        picked = jnp.where(crank < PEER_TOPK, 1.0, 0.0)
        e = picked * jnp.exp(cand - (v0[0] + v1[0]))
        z = jnp.sum(e, axis=0, keepdims=True)
        lq = jnp.zeros((N_KEYS, ts), F32)
        for a in range(PEER_TOPK):
            cnt = jnp.zeros((1, ts), F32)
            for ci, (ca, _) in enumerate(_PEER_CAND):
                if ca == a:
                    cnt = cnt + picked[ci:ci + 1]
            lq = jnp.where(rank0 == a, cnt, lq)
        rank_b = rank1.astype(BF16)
        qe_b = jnp.exp(s1 - v1[0]).astype(BF16)
        for k in range(N_KEYS // BF16_ROWS):
            rank_ref[h, k] = pltpu.bitcast(rank_b[k * BF16_ROWS:(k + 1) * BF16_ROWS], jnp.uint32)
            qe_ref[h, k] = pltpu.bitcast(qe_b[k * BF16_ROWS:(k + 1) * BF16_ROWS], jnp.uint32)
        lq_ref[h] = lq
        pe_ref[h] = jnp.exp(s0 - v0[0]) / z
        return carry

    lax.fori_loop(0, PEER_HEADS, head, 0)


def _peer_select(q, keys):
    n = q.shape[0]
    out = jax.ShapeDtypeStruct((PEER_HEADS, N_KEYS, n), F32)
    ospec = pl.BlockSpec((PEER_HEADS, N_KEYS, SEL_TILE), lambda i: (0, 0, i))
    packed = (PEER_HEADS, N_KEYS // BF16_ROWS, BF16_ROWS // 2)
    out_b = jax.ShapeDtypeStruct(packed + (n,), jnp.uint32)
    ospec_b = pl.BlockSpec(packed + (SEL_TILE,), lambda i: (0, 0, 0, i))
    return pl.pallas_call(
        _peer_select_body,
        grid=(n // SEL_TILE,),
        in_specs=[pl.BlockSpec((SEL_TILE, q.shape[1]), lambda i: (i, 0)), _full(keys.shape)],
        out_specs=[ospec_b, ospec_b, ospec, ospec],
        out_shape=[out_b, out_b, out, out],
        compiler_params=_params(("arbitrary",)),
        name="peer_select",
    )(q, keys)


def _peer_dense_body(ht_ref, u_ref, vt_ref, rank_ref, qe_ref, lq_ref, pe_ref, x_ref, o_ref,
                     acc_ref, a_ref, w_ref):
    step = pl.program_id(1)

    @pl.when(step == 0)
    def _():
        acc_ref[...] = jnp.zeros_like(acc_ref)

    quarter = PEER_ROWS * N_KEYS // 4
    for qi in range(4):
        rows = slice(qi * quarter, (qi + 1) * quarter)
        a_ref[rows, :] = _dot(u_ref[rows, :], ht_ref[...])
    sub = (N_KEYS // BF16_ROWS, BF16_ROWS, LANES)
    for il in range(PEER_ROWS):
        for ci in range(TOKEN_TILE // LANES):
            cs = slice(ci * LANES, (ci + 1) * LANES)
            gate = jnp.zeros(sub, BF16)
            for h in range(PEER_HEADS):
                lq = jnp.broadcast_to(lq_ref[h, il:il + 1, cs], sub[1:]).astype(BF16)
                pe = jnp.broadcast_to(pe_ref[h, il:il + 1, cs], sub[1:]).astype(BF16)
                take = pltpu.bitcast(rank_ref[h, :, :, cs], BF16) < lq[None]
                qe = pltpu.bitcast(qe_ref[h, :, :, cs], BF16)
                gate = gate + jnp.where(take, qe, jnp.zeros(sub, BF16)) * pe[None]
            for k in range(sub[0]):
                rs = slice(il * N_KEYS + k * BF16_ROWS, il * N_KEYS + (k + 1) * BF16_ROWS)
                w_ref[rs, cs] = gate[k] * _gelu(a_ref[rs, cs]).astype(BF16)
    acc_ref[...] += _dot(vt_ref[...], w_ref[...])

    @pl.when(step == pl.num_programs(1) - 1)
    def _():
        o_ref[...] = x_ref[...] + acc_ref[...].T


def _peer_dense(ht, u, vt, rank, qe, lq, pe, x):
    n = x.shape[0]
    ex = PEER_ROWS * N_KEYS
    steps = N_KEYS // PEER_ROWS
    tok = pl.BlockSpec((PEER_HEADS, N_KEYS // BF16_ROWS, BF16_ROWS // 2, TOKEN_TILE), lambda t, e: (0, 0, 0, t))
    rowsel = pl.BlockSpec((PEER_HEADS, PEER_ROWS, TOKEN_TILE), lambda t, e: (0, e, t))
    return pl.pallas_call(
        _peer_dense_body,
        grid=(n // TOKEN_TILE, steps),
        in_specs=[pl.BlockSpec((D_MODEL, TOKEN_TILE), lambda t, e: (0, t)),
                  pl.BlockSpec((ex, D_MODEL), lambda t, e: (e, 0)),
                  pl.BlockSpec((D_MODEL, ex), lambda t, e: (0, e)),
                  tok, tok, rowsel, rowsel,
                  pl.BlockSpec((TOKEN_TILE, D_MODEL), lambda t, e: (t, 0))],
        out_specs=pl.BlockSpec((TOKEN_TILE, D_MODEL), lambda t, e: (t, 0)),
        out_shape=jax.ShapeDtypeStruct((n, D_MODEL), F32),
        scratch_shapes=[pltpu.VMEM((D_MODEL, TOKEN_TILE), F32), pltpu.VMEM((ex, TOKEN_TILE), F32),
                        pltpu.VMEM((ex, TOKEN_TILE), BF16)],
        compiler_params=_params(("arbitrary", "arbitrary")),
        name="peer_dense",
    )(ht, u, vt, rank, qe, lq, pe, x)


def _peer(x1, ht, q, keys, u, vt):
    rank, qe, lq, pe = _peer_select(q, keys)
    return _peer_dense(ht, u, vt, rank, qe, lq, pe, x1)


def _block_diag_mean(width):
    idx = np.arange(width) // HEAD_DIM
    return jnp.asarray((idx[:, None] == idx[None, :]).astype(np.float32) / HEAD_DIM)


def kernel(x_prompt, x_sample, cache_swa_k, cache_swa_v, state_hgrn, state_conv, norm_mix_g, norm_ffn_g, w_in0, attn_q_norm_g, attn_k_norm_g, attn_sinks, hgrn_lb, hgrn_o_norm_g, w_out0, conv_w_pw1, conv_b_pw1, conv_w_dw, conv_b_dw, conv_ln_g, conv_ln_b, conv_w_pw2, peer_w_q, peer_keys, peer_u, peer_v):
    batch, seq, _ = x_prompt.shape
    dec_batch, dec_seq, _ = x_sample.shape
    n_p = batch * seq
    n_s = dec_batch * dec_seq
    assert seq % CONV_ROWS == 0 and seq % HGRN_ROWS == 0 and n_p % TOKEN_TILE == 0 and n_s % TOKEN_TILE == 0
    assert dec_batch % DEC_GROUP == 0 and 8 % dec_seq == 0 and dec_seq <= CONV_W - 1

    x = jnp.concatenate([x_prompt.reshape(n_p, D_MODEL), x_sample.reshape(n_s, D_MODEL)], axis=0)
    row = lambda v: v.reshape(1, -1).astype(F32)

    z = _in0(x, row(norm_mix_g[0]), w_in0.astype(BF16))
    d = np.arange(LANES) % HEAD_DIM
    inv = ROPE_THETA ** (-jnp.arange(0, ROT_DIM, 2, dtype=F32) / ROT_DIM)
    inv_lane = jnp.where(jnp.asarray(d < ROT_DIM), inv[d % (ROT_DIM // 2)], 0.0).reshape(1, LANES)
    gq = row(jnp.tile(attn_q_norm_g, N_HEADS_A))
    gk = row(jnp.tile(attn_k_norm_g, N_KV_A))
    segq, segk = _block_diag_mean(QA), _block_diag_mean(KVA)
    sinks = attn_sinks.astype(F32)
    attn_p, k_win_p, v_win_p = _attn_prompt(z, batch, seq, gq, gk, inv_lane, segq, segk, sinks)
    kc = cache_swa_k.reshape(dec_batch * WINDOW, KVA)
    vc = cache_swa_v.reshape(dec_batch * WINDOW, KVA)
    attn_s, k_new, v_new = _attn_decode(z, n_p, dec_batch, dec_seq, kc, vc, gq, gk, inv_lane, segq, segk, sinks)

    go = row(hgrn_o_norm_g)
    tril = jnp.asarray(np.tril(np.ones((HGRN_CHUNK, HGRN_CHUNK), np.float32)))
    hg_p, s_p = _hgrn_prompt(z, batch, seq, hgrn_lb, go, tril)
    r = np.arange(LANES)
    tril_dec = jnp.asarray(((r[:, None] // dec_seq == r[None, :] // dec_seq) & (r[None, :] <= r[:, None]))
                           .astype(np.float32))
    hg_s, s_s = _hgrn_decode(z, n_p, dec_batch, dec_seq, state_hgrn, hgrn_lb, go, tril_dec)

    attn = jnp.concatenate([attn_p, attn_s], axis=0)
    hg = jnp.concatenate([hg_p, hg_s], axis=0)
    x1, ht, q = _mix_out(_out0_body, [attn, hg], x, [w_out0.astype(BF16)], row(norm_ffn_g[0]),
                         peer_w_q[0].astype(BF16), "mix0_out")
    x2 = _peer(x1, ht, q, peer_keys[0].astype(BF16), peer_u[0].astype(BF16), peer_v[0].astype(BF16).T)

    u = _glu(x2, row(norm_mix_g[1]), conv_w_pw1.astype(BF16), row(conv_b_pw1))
    wdw = jnp.concatenate([conv_w_dw, jnp.zeros((1, D_MODEL), F32)], axis=0)
    bdw = row(conv_b_dw)
    c_p = _conv_prompt(u, batch, seq, wdw, bdw)
    wshift = jnp.stack([jnp.concatenate([jnp.zeros((t, D_MODEL), F32), conv_w_dw[:CONV_W - 1 - t]], axis=0)
                        for t in range(dec_seq)])
    c_s = _conv_decode(u, n_p, dec_batch, dec_seq, state_conv, wdw, wshift, bdw)
    c = jnp.concatenate([c_p, c_s], axis=0)
    x3, ht, q = _mix_out(_post1_body, [c], x2, [row(conv_ln_g), row(conv_ln_b), conv_w_pw2.astype(BF16)],
                         row(norm_ffn_g[1]), peer_w_q[1].astype(BF16), "mix1_out")
    x4 = _peer(x3, ht, q, peer_keys[1].astype(BF16), peer_u[1].astype(BF16), peer_v[1].astype(BF16).T)

    kv = lambda t: t.reshape(t.shape[0], WINDOW, N_KV_A, HEAD_DIM)
    k_win_s = jnp.concatenate([cache_swa_k[:, dec_seq:], k_new.reshape(dec_batch, dec_seq, N_KV_A, HEAD_DIM)], axis=1)
    v_win_s = jnp.concatenate([cache_swa_v[:, dec_seq:], v_new.reshape(dec_batch, dec_seq, N_KV_A, HEAD_DIM)], axis=1)
    u_p = u[:n_p].reshape(batch, seq, D_MODEL)
    u_s = u[n_p:].reshape(dec_batch, dec_seq, D_MODEL)
    conv_buf_p = u_p[:, seq - (CONV_W - 1):]
    conv_buf_s = jnp.concatenate([state_conv[:, dec_seq:], u_s], axis=1)
    return (x4[:n_p].reshape(batch, seq, D_MODEL), x4[n_p:].reshape(dec_batch, dec_seq, D_MODEL),
            kv(k_win_p), kv(v_win_p), s_p, conv_buf_p, k_win_s, v_win_s, s_s, conv_buf_s)
```

```python
import functools

import numpy as np
import jax
import jax.numpy as jnp
from jax import lax
from jax.experimental import pallas as pl
from jax.experimental.pallas import tpu as pltpu

F32 = jnp.float32
BF16 = jnp.bfloat16
HI = lax.Precision.HIGHEST

D_MODEL = 1024
PAST_LEN = 8192
HEAD_DIM = 64
N_HEADS_A = 8
N_KV_A = 2
KV_REP = N_HEADS_A // N_KV_A
WINDOW = 128
ROT_DIM = HEAD_DIM // 4
ROPE_THETA = 500000.0
ATTN_SCALE = HEAD_DIM ** -0.5
NEG_INF = -1e30
N_HEADS_B = 4
DK_B = 128
DV_B = 128
CONV_W = 31
N_KEYS = 128
PEER_HEADS = 8
PEER_TOPK = 16
PEER_DKH = 128
NORM_EPS = 1e-6

QA = N_HEADS_A * HEAD_DIM
KVA = N_KV_A * HEAD_DIM
QB = N_HEADS_B * DK_B
VB = N_HEADS_B * DV_B
IN0_WIDTH = QA + 2 * KVA + 2 * QB + 2 * VB
COL_K = QA // 128
COL_V = (QA + KVA) // 128
COL_QB = (QA + 2 * KVA) // 128
COL_FB = COL_QB + QB // 128
COL_IB = COL_FB + QB // 128
COL_GB = COL_IB + VB // 128

LANES = 128
SUBLANES = 8
TOKEN_TILE = 512
SEL_TILE = 256
HGRN_CHUNK = 64
HGRN_SUB = 16
HGRN_ROWS = 256
CONV_ROWS = 512
CONV_HALO = 32
CONV_CHUNK = 128
PEER_ROWS = 16
BF16_ROWS = 16
DEC_GROUP = 8
VMEM_LIMIT = 48 * 1024 * 1024


def _dot(a, b, prec=None):
    return jnp.dot(a, b, preferred_element_type=F32, precision=prec)


def _dot_nt(a, b, prec=None):
    return lax.dot_general(a, b, (((1,), (1,)), ((), ())), preferred_element_type=F32, precision=prec)


def _rms(x, g):
    return x * lax.rsqrt(jnp.mean(x * x, axis=-1, keepdims=True) + NORM_EPS) * g


def _silu(x):
    return x * jax.nn.sigmoid(x)


def _gelu(x):
    return 0.5 * x * (1.0 + lax.erf(x * np.float32(0.7071067811865476)))


def _params(sem, flags=None):
    return pltpu.CompilerParams(dimension_semantics=sem, vmem_limit_bytes=VMEM_LIMIT, flags=flags)


def _full(shape):
    n = len(shape)
    return pl.BlockSpec(shape, lambda *_: (0,) * n)


def _in0_body(x_ref, g_ref, w_ref, z_ref):
    h = _rms(x_ref[...], g_ref[...]).astype(BF16)
    z_ref[...] = _dot(h, w_ref[...])


def _in0(x, g, w):
    n = x.shape[0]
    width = w.shape[1]
    return pl.pallas_call(
        _in0_body,
        grid=(n // TOKEN_TILE,),
        in_specs=[pl.BlockSpec((TOKEN_TILE, D_MODEL), lambda i: (i, 0)), _full((1, D_MODEL)),
                  _full((D_MODEL, width))],
        out_specs=pl.BlockSpec((TOKEN_TILE, width), lambda i: (i, 0)),
        out_shape=jax.ShapeDtypeStruct((n, width), F32),
        compiler_params=_params(("arbitrary",)),
        name="in0_proj",
    )(x, g, w)


def _glu_body(x_ref, g_ref, w_ref, b_ref, u_ref):
    h = _rms(x_ref[...], g_ref[...]).astype(BF16)
    a = _dot(h, w_ref[...]) + b_ref[...]
    u_ref[...] = a[:, :D_MODEL] * jax.nn.sigmoid(a[:, D_MODEL:])


def _glu(x, g, w, b):
    n = x.shape[0]
    return pl.pallas_call(
        _glu_body,
        grid=(n // TOKEN_TILE,),
        in_specs=[pl.BlockSpec((TOKEN_TILE, D_MODEL), lambda i: (i, 0)), _full((1, D_MODEL)),
                  _full((D_MODEL, 2 * D_MODEL)), _full((1, 2 * D_MODEL))],
        out_specs=pl.BlockSpec((TOKEN_TILE, D_MODEL), lambda i: (i, 0)),
        out_shape=jax.ShapeDtypeStruct((n, D_MODEL), F32),
        compiler_params=_params(("arbitrary",)),
        name="conv_glu",
    )(x, g, w, b)


def _ffn_query(x1, gf_ref, wq_ref, x1_ref, ht_ref, q_ref):
    x1_ref[...] = x1
    h2 = _rms(x1, gf_ref[...])
    ht_ref[...] = h2.T.astype(BF16)
    q_ref[...] = _dot(h2.astype(BF16), wq_ref[...])


def _out0_body(a_ref, hg_ref, x_ref, w_ref, gf_ref, wq_ref, x1_ref, ht_ref, q_ref):
    m = _dot(a_ref[...].astype(BF16), w_ref[:QA, :]) + _dot(hg_ref[...].astype(BF16), w_ref[QA:, :])
    _ffn_query(x_ref[...] + m, gf_ref, wq_ref, x1_ref, ht_ref, q_ref)


def _post1_body(c_ref, x_ref, lg_ref, lb_ref, w_ref, gf_ref, wq_ref, x1_ref, ht_ref, q_ref):
    c = c_ref[...]
    mu = jnp.mean(c, axis=-1, keepdims=True)
    d = c - mu
    var = jnp.mean(d * d, axis=-1, keepdims=True)
    ln = d * lax.rsqrt(var + NORM_EPS) * lg_ref[...] + lb_ref[...]
    y = _dot(_silu(ln).astype(BF16), w_ref[...])
    _ffn_query(x_ref[...] + y, gf_ref, wq_ref, x1_ref, ht_ref, q_ref)


def _mix_out(body, acts, x, consts, gf, wq, name):
    n = x.shape[0]
    qw = wq.shape[1]
    row = lambda w: pl.BlockSpec((TOKEN_TILE, w), lambda i: (i, 0))
    return pl.pallas_call(
        body,
        grid=(n // TOKEN_TILE,),
        in_specs=[row(a.shape[1]) for a in acts] + [row(D_MODEL)] + [_full(c.shape) for c in consts]
        + [_full(gf.shape), _full(wq.shape)],
        out_specs=[row(D_MODEL), pl.BlockSpec((D_MODEL, TOKEN_TILE), lambda i: (0, i)), row(qw)],
        out_shape=[jax.ShapeDtypeStruct((n, D_MODEL), F32), jax.ShapeDtypeStruct((D_MODEL, n), BF16),
                   jax.ShapeDtypeStruct((n, qw), F32)],
        compiler_params=_params(("arbitrary",)),
        name=name,
    )(*acts, x, *consts, gf, wq)


def _head_norm(x, g, seg):
    ms = _dot(x * x, seg, HI)
    return x * lax.rsqrt(ms + NORM_EPS) * g


def _rope(x, cos, sin, first_half):
    half = ROT_DIM // 2
    width = x.shape[1]
    up = pltpu.roll(x, width - half, axis=1)
    dn = pltpu.roll(x, half, axis=1)
    return x * cos + jnp.where(first_half, -up, dn) * sin


def _rope_tables(pos, inv_ref, reps):
    ang = pos * inv_ref[...]
    c, s = jnp.cos(ang), jnp.sin(ang)
    if reps > 1:
        c, s = (jnp.concatenate([t] * reps, axis=1) for t in (c, s))
    lane = lax.broadcasted_iota(jnp.int32, c.shape, 1)
    return c, s, (lane % HEAD_DIM) < (ROT_DIM // 2)


def _stack_heads(x, g):
    return jnp.concatenate(
        [x[:, (g * KV_REP + r) * HEAD_DIM:(g * KV_REP + r + 1) * HEAD_DIM] for r in range(KV_REP)], axis=0)


def _attn_prompt_body(q_ref, k_ref, v_ref, gq_ref, gk_ref, inv_ref, segq_ref, segk_ref, sink_ref,
                      o_ref, kw_ref, vw_ref, kprev_ref, vprev_ref):
    j = pl.program_id(1)
    w = WINDOW

    @pl.when(j == 0)
    def _():
        kprev_ref[...] = jnp.zeros_like(kprev_ref)
        vprev_ref[...] = jnp.zeros_like(vprev_ref)

    rows = lax.broadcasted_iota(jnp.int32, (w, LANES), 0)
    pos = (j * w + rows).astype(F32)
    ck, sk, fk = _rope_tables(pos, inv_ref, 1)
    cq, sq, fq = _rope_tables(pos, inv_ref, QA // LANES)
    q = _rope(_head_norm(q_ref[...], gq_ref[...], segq_ref[...]), cq, sq, fq)
    k = _rope(_head_norm(k_ref[...], gk_ref[...], segk_ref[...]), ck, sk, fk)
    v = v_ref[...]
    kp = kprev_ref[...]
    vp = vprev_ref[...]

    qi = lax.broadcasted_iota(jnp.int32, (KV_REP * w, w), 0) % w
    ki = lax.broadcasted_iota(jnp.int32, (KV_REP * w, w), 1)
    m_own = ki <= qi
    m_prev = (ki > qi) & (j > 0)
    rep = lax.broadcasted_iota(jnp.int32, (KV_REP * w, 1), 0) // w
    for g in range(N_KV_A):
        sl = slice(g * HEAD_DIM, (g + 1) * HEAD_DIM)
        qg = _stack_heads(q, g).astype(BF16)
        s_own = jnp.where(m_own, _dot_nt(qg, k[:, sl].astype(BF16)) * ATTN_SCALE, NEG_INF)
        s_prev = jnp.where(m_prev, _dot_nt(qg, kp[:, sl].astype(BF16)) * ATTN_SCALE, NEG_INF)
        sink = jnp.zeros((KV_REP * w, 1), F32)
        for r in range(KV_REP):
            sink = jnp.where(rep == r, sink_ref[g * KV_REP + r], sink)
        mx = jnp.maximum(jnp.maximum(jnp.max(s_own, axis=-1, keepdims=True),
                                     jnp.max(s_prev, axis=-1, keepdims=True)), sink)
        e_own = jnp.exp(s_own - mx)
        e_prev = jnp.exp(s_prev - mx)
        den = (jnp.sum(e_own, axis=-1, keepdims=True) + jnp.sum(e_prev, axis=-1, keepdims=True)
               + jnp.exp(sink - mx))
        o = (_dot(e_own.astype(BF16), v[:, sl].astype(BF16))
             + _dot(e_prev.astype(BF16), vp[:, sl].astype(BF16))) / den
        for r in range(KV_REP):
            hq = g * KV_REP + r
            o_ref[:, hq * HEAD_DIM:(hq + 1) * HEAD_DIM] = o[r * w:(r + 1) * w]

    kprev_ref[...] = k
    vprev_ref[...] = v

    @pl.when(j == pl.num_programs(1) - 1)
    def _():
        kw_ref[0] = k
        vw_ref[0] = v


def _attn_prompt(z, batch, seq, gq, gk, inv, segq, segk, sinks):
    nb = seq // WINDOW
    blk = lambda width, col: pl.BlockSpec((WINDOW, width), lambda b, j: (b * nb + j, col))
    return pl.pallas_call(
        _attn_prompt_body,
        grid=(batch, nb),
        in_specs=[blk(QA, 0), blk(KVA, COL_K), blk(KVA, COL_V), _full(gq.shape), _full(gk.shape),
                  _full(inv.shape), _full(segq.shape), _full(segk.shape),
                  pl.BlockSpec(memory_space=pltpu.SMEM)],
        out_specs=[pl.BlockSpec((WINDOW, QA), lambda b, j: (b * nb + j, 0)),
                   pl.BlockSpec((1, WINDOW, KVA), lambda b, j: (b, 0, 0)),
                   pl.BlockSpec((1, WINDOW, KVA), lambda b, j: (b, 0, 0))],
        out_shape=[jax.ShapeDtypeStruct((batch * seq, QA), F32),
                   jax.ShapeDtypeStruct((batch, WINDOW, KVA), F32),
                   jax.ShapeDtypeStruct((batch, WINDOW, KVA), F32)],
        scratch_shapes=[pltpu.VMEM((WINDOW, KVA), F32), pltpu.VMEM((WINDOW, KVA), F32)],
        compiler_params=_params(("arbitrary", "arbitrary")),
        name="swa_prompt",
    )(z, z, z, gq, gk, inv, segq, segk, sinks)


def _attn_decode_body(q_ref, k_ref, v_ref, kc_ref, vc_ref, gq_ref, gk_ref, inv_ref, segq_ref, segk_ref,
                      sink_ref, o_ref, kn_ref, vn_ref, *, dec_seq):
    rows_n = DEC_GROUP * dec_seq
    rows = lax.broadcasted_iota(jnp.int32, (rows_n, LANES), 0)
    pos = (PAST_LEN + rows % dec_seq).astype(F32)
    ck, sk, fk = _rope_tables(pos, inv_ref, 1)
    cq, sq, fq = _rope_tables(pos, inv_ref, QA // LANES)
    q = _rope(_head_norm(q_ref[...], gq_ref[...], segq_ref[...]), cq, sq, fq)
    k = _rope(_head_norm(k_ref[...], gk_ref[...], segk_ref[...]), ck, sk, fk)
    v = v_ref[...]
    kn_ref[...] = k
    vn_ref[...] = v

    nq = KV_REP * rows_n
    nc = DEC_GROUP * WINDOW

    def qrow(shape):
        r = lax.broadcasted_iota(jnp.int32, shape, 0) % rows_n
        return r // dec_seq, r % dec_seq

    bq, tq = qrow((nq, nc))
    col = lax.broadcasted_iota(jnp.int32, (nq, nc), 1)
    m_cache = (col // WINDOW == bq) & (col % WINDOW > tq)
    bq, tq = qrow((nq, rows_n))
    col = lax.broadcasted_iota(jnp.int32, (nq, rows_n), 1)
    m_new = (col // dec_seq == bq) & (col % dec_seq <= tq)
    rep = lax.broadcasted_iota(jnp.int32, (nq, 1), 0) // rows_n
    for g in range(N_KV_A):
        sl = slice(g * HEAD_DIM, (g + 1) * HEAD_DIM)
        qg = _stack_heads(q, g).astype(BF16)
        s_c = jnp.where(m_cache, _dot_nt(qg, kc_ref[:, sl].astype(BF16)) * ATTN_SCALE, NEG_INF)
        s_n = jnp.where(m_new, _dot_nt(qg, k[:, sl].astype(BF16)) * ATTN_SCALE, NEG_INF)
        sink = jnp.zeros((nq, 1), F32)
        for r in range(KV_REP):
            sink = jnp.where(rep == r, sink_ref[g * KV_REP + r], sink)
        mx = jnp.maximum(jnp.maximum(jnp.max(s_c, axis=-1, keepdims=True),
                                     jnp.max(s_n, axis=-1, keepdims=True)), sink)
        e_c = jnp.exp(s_c - mx)
        e_n = jnp.exp(s_n - mx)
        den = jnp.sum(e_c, axis=-1, keepdims=True) + jnp.sum(e_n, axis=-1, keepdims=True) + jnp.exp(sink - mx)
        o = (_dot(e_c.astype(BF16), vc_ref[:, sl].astype(BF16))
             + _dot(e_n.astype(BF16), v[:, sl].astype(BF16))) / den
        for r in range(KV_REP):
            hq = g * KV_REP + r
            o_ref[:, hq * HEAD_DIM:(hq + 1) * HEAD_DIM] = o[r * rows_n:(r + 1) * rows_n]


def _attn_decode(z, row0, dec_batch, dec_seq, kc, vc, gq, gk, inv, segq, segk, sinks):
    rows_n = DEC_GROUP * dec_seq
    r0 = row0 // rows_n
    blk = lambda width, col: pl.BlockSpec((rows_n, width), lambda i: (r0 + i, col))
    cache = pl.BlockSpec((DEC_GROUP * WINDOW, KVA), lambda i: (i, 0))
    n = dec_batch * dec_seq
    return pl.pallas_call(
        functools.partial(_attn_decode_body, dec_seq=dec_seq),
        grid=(dec_batch // DEC_GROUP,),
        in_specs=[blk(QA, 0), blk(KVA, COL_K), blk(KVA, COL_V), cache, cache, _full(gq.shape),
                  _full(gk.shape), _full(inv.shape), _full(segq.shape), _full(segk.shape),
                  pl.BlockSpec(memory_space=pltpu.SMEM)],
        out_specs=[pl.BlockSpec((rows_n, QA), lambda i: (i, 0)), pl.BlockSpec((rows_n, KVA), lambda i: (i, 0)),
                   pl.BlockSpec((rows_n, KVA), lambda i: (i, 0))],
        out_shape=[jax.ShapeDtypeStruct((n, QA), F32), jax.ShapeDtypeStruct((n, KVA), F32),
                   jax.ShapeDtypeStruct((n, KVA), F32)],
        compiler_params=_params(("arbitrary",)),
        name="swa_decode",
    )(z, z, z, kc, vc, gq, gk, inv, segq, segk, sinks)


def _hgrn_lower_bound(lb_ref):
    l = lb_ref[...]
    e = jnp.exp(l - jnp.max(l, axis=0, keepdims=True))
    return e[0:1] / jnp.sum(e, axis=0, keepdims=True)


def _hgrn_gates(qb, fb, lb):
    logf = jnp.log(lb + (1.0 - lb) * jax.nn.sigmoid(fb))
    kb = (1.0 - lb) * jax.nn.sigmoid(-fb)
    return _silu(qb), kb, logf


def _hgrn_out(o, g, gate):
    return _rms(o, g) * _silu(gate)


def _hgrn_chunk(qh, kb, ih, logf, st, tril):
    c = qh.shape[0]
    hi = logf.astype(BF16)
    rest = logf - hi.astype(F32)
    mid = rest.astype(BF16)
    b = _dot(tril, hi) + _dot(tril, mid) + _dot(tril, (rest - mid.astype(F32)).astype(BF16))
    ones = jnp.ones((DK_B, LANES), BF16)
    st_b = st.astype(BF16)
    ih_b = ih.astype(BF16)
    o_parts = []
    row = lax.broadcasted_iota(jnp.int32, (c, 1), 0)
    trow = lax.broadcasted_iota(jnp.int32, (HGRN_SUB, 1), 0)
    for blk in range(c // HGRN_SUB):
        t0 = blk * HGRN_SUB
        bi = b[t0:t0 + HGRN_SUB]
        qi = qh[t0:t0 + HGRN_SUB]
        if blk == 0:
            oi = _dot_nt((qi * jnp.exp(bi)).astype(BF16), st_b)
        else:
            base = b[t0 - 1:t0]
            qd = qi * jnp.exp(bi - base)
            kd = jnp.where(row < t0, kb * jnp.exp(jnp.minimum(base - b, 0.0)), 0.0)
            a_off = _dot_nt(qd.astype(BF16), kd.astype(BF16))
            oi = _dot_nt((qd * jnp.exp(base)).astype(BF16), st_b) + _dot(a_off.astype(BF16), ih_b)
        prods = []
        for s in range(HGRN_SUB):
            e = jnp.exp(jnp.where(trow >= s, bi - bi[s:s + 1], 0.0))
            prods.append(qi * e * kb[t0 + s:t0 + s + 1])
        a_diag = _dot(jnp.concatenate(prods, axis=0).astype(BF16), ones)
        for s in range(HGRN_SUB):
            a = a_diag[s * HGRN_SUB:(s + 1) * HGRN_SUB]
            oi = oi + jnp.where(trow >= s, a, 0.0) * ih[t0 + s:t0 + s + 1]
        o_parts.append(oi)
    last = b[c - 1:c]
    st_new = st * jnp.exp(last) + _dot(ih.T.astype(BF16), (kb * jnp.exp(last - b)).astype(BF16))
    return jnp.concatenate(o_parts, axis=0), st_new


def _hgrn_prompt_body(qb_ref, fb_ref, ib_ref, gb_ref, lb_ref, go_ref, tril_ref, o_ref, s_ref, st_ref):
    tb = pl.program_id(2)

    @pl.when(tb == 0)
    def _():
        st_ref[...] = jnp.zeros_like(st_ref)

    lb = _hgrn_lower_bound(lb_ref)
    tril = tril_ref[...]
    st = st_ref[...]
    for c in range(HGRN_ROWS // HGRN_CHUNK):
        rs = slice(c * HGRN_CHUNK, (c + 1) * HGRN_CHUNK)
        qh, kb, logf = _hgrn_gates(qb_ref[rs, :], fb_ref[rs, :], lb)
        o, st = _hgrn_chunk(qh, kb, ib_ref[rs, :], logf, st, tril)
        o_ref[rs, :] = _hgrn_out(o, go_ref[...], gb_ref[rs, :])
    st_ref[...] = st

    @pl.when(tb == pl.num_programs(2) - 1)
    def _():
        s_ref[0, 0] = st.T


def _hgrn_prompt(z, batch, seq, lb, go, tril):
    nt = seq // HGRN_ROWS
    blk = lambda col: pl.BlockSpec((HGRN_ROWS, DK_B), lambda b, h, t: (b * nt + t, col + h))
    return pl.pallas_call(
        _hgrn_prompt_body,
        grid=(batch, N_HEADS_B, nt),
        in_specs=[blk(COL_QB), blk(COL_FB), blk(COL_IB), blk(COL_GB),
                  pl.BlockSpec((lb.shape[0], DK_B), lambda b, h, t: (0, h)), _full(go.shape), _full(tril.shape)],
        out_specs=[pl.BlockSpec((HGRN_ROWS, DV_B), lambda b, h, t: (b * nt + t, h)),
                   pl.BlockSpec((1, 1, DK_B, DV_B), lambda b, h, t: (b, h, 0, 0))],
        out_shape=[jax.ShapeDtypeStruct((batch * seq, VB), F32),
                   jax.ShapeDtypeStruct((batch, N_HEADS_B, DK_B, DV_B), F32)],
        scratch_shapes=[pltpu.VMEM((DV_B, DK_B), F32)],
        compiler_params=_params(("arbitrary", "arbitrary", "arbitrary")),
        name="hgrn_prompt",
    )(z, z, z, z, lb, go, tril)


def _pad_rows(x, rows):
    return jnp.concatenate([x, jnp.zeros((rows - x.shape[0], x.shape[1]), x.dtype)], axis=0)


def _hgrn_decode_body(z_ref, s0_ref, lb_ref, go_ref, tril_ref, o_ref, s_ref, *, dec_seq):
    rows_n = z_ref.shape[0]
    groups = rows_n // dec_seq
    row = lax.broadcasted_iota(jnp.int32, (rows_n, 1), 0)
    for h in range(N_HEADS_B):
        col = lambda c0: slice((c0 + h) * LANES, (c0 + h + 1) * LANES)
        hs = slice(h * DK_B, (h + 1) * DK_B)
        lb = _hgrn_lower_bound(lb_ref.at[:, hs])
        qh, kb, logf = _hgrn_gates(z_ref[:, col(COL_QB)], z_ref[:, col(COL_FB)], lb)
        ih = z_ref[:, col(COL_IB)]
        b = _dot(tril_ref[...], _pad_rows(logf, LANES), HI)[:rows_n]
        b_t = _pad_rows(b, LANES).T
        ih_pad = _pad_rows(ih, LANES)
        o = jnp.zeros((rows_n, DV_B), F32)
        for e in range(groups):
            mine = (row // dec_seq) == e
            s0 = s0_ref[e, h]
            o = o + _dot(jnp.where(mine, qh * jnp.exp(b), 0.0), s0, HI)
            for s in range(dec_seq):
                r = e * dec_seq + s
                live = mine & (row >= r)
                ex = jnp.exp(jnp.where(live, b - b[r:r + 1], 0.0))
                a = jnp.sum(qh * ex * kb[r:r + 1], axis=-1, keepdims=True)
                o = o + jnp.where(live, a, 0.0) * ih[r:r + 1]
            r_last = (e + 1) * dec_seq - 1
            last = b[r_last:r_last + 1]
            kd = jnp.where(mine, kb * jnp.exp(jnp.minimum(last - b, 0.0)), 0.0)
            s_ref[e, h] = s0 * jnp.exp(b_t[:, r_last:r_last + 1]) + _dot(_pad_rows(kd, LANES).T, ih_pad, HI)
        o_ref[:, hs] = _hgrn_out(o, go_ref[...], z_ref[:, col(COL_GB)])


def _hgrn_decode(z, row0, dec_batch, dec_seq, s0, lb, go, tril):
    groups = 8 // dec_seq
    rows_n = groups * dec_seq
    r0 = row0 // rows_n
    st = pl.BlockSpec((groups, N_HEADS_B, DK_B, DV_B), lambda i: (i, 0, 0, 0))
    return pl.pallas_call(
        functools.partial(_hgrn_decode_body, dec_seq=dec_seq),
        grid=(dec_batch // groups,),
        in_specs=[pl.BlockSpec((rows_n, z.shape[1]), lambda i: (r0 + i, 0)), st,
                  _full(lb.shape), _full(go.shape), _full(tril.shape)],
        out_specs=[pl.BlockSpec((rows_n, VB), lambda i: (i, 0)), st],
        out_shape=[jax.ShapeDtypeStruct((dec_batch * dec_seq, VB), F32),
                   jax.ShapeDtypeStruct(s0.shape, F32)],
        compiler_params=_params(("arbitrary",)),
        name="hgrn_decode",
    )(z, s0, lb, go, tril)


def _conv_prompt_body(cur_ref, prev_ref, w_ref, b_ref, c_ref, ext_ref, sh_ref):
    t = pl.program_id(1)
    ext_ref[:CONV_HALO, :] = jnp.where(t > 0, prev_ref[...], 0.0)
    ext_ref[CONV_HALO:, :] = cur_ref[...]
    lead = CONV_HALO - (CONV_W - 1)
    chunk = CONV_CHUNK

    def cols(ci, carry):
        cs = pl.ds(pl.multiple_of(ci * LANES, LANES), LANES)
        for r in range(CONV_ROWS // chunk):
            acc = jnp.zeros((chunk, LANES), F32) + b_ref[:, cs]
            for res in range(SUBLANES):
                taps = range(res, CONV_W, SUBLANES)
                span = chunk + taps[-1] - res
                sh_ref[:span, :] = ext_ref[pl.ds(r * chunk + lead + res, span), cs]
                for w in taps:
                    acc = acc + sh_ref[w - res:w - res + chunk, :] * w_ref[pl.ds(w, 1), cs]
            c_ref[pl.ds(r * chunk, chunk), cs] = acc
        return carry

    lax.fori_loop(0, D_MODEL // LANES, cols, 0)


def _conv_prompt(u, batch, seq, w, b):
    nt = seq // CONV_ROWS
    per = CONV_ROWS // CONV_HALO
    return pl.pallas_call(
        _conv_prompt_body,
        grid=(batch, nt),
        in_specs=[pl.BlockSpec((CONV_ROWS, D_MODEL), lambda bi, t: (bi * nt + t, 0)),
                  pl.BlockSpec((CONV_HALO, D_MODEL), lambda bi, t: (jnp.maximum((bi * nt + t) * per - 1, 0), 0)),
                  _full(w.shape), _full(b.shape)],
        out_specs=pl.BlockSpec((CONV_ROWS, D_MODEL), lambda bi, t: (bi * nt + t, 0)),
        out_shape=jax.ShapeDtypeStruct((batch * seq, D_MODEL), F32),
        scratch_shapes=[pltpu.VMEM((CONV_HALO + CONV_ROWS, D_MODEL), F32),
                        pltpu.VMEM((CONV_CHUNK + CONV_HALO, LANES), F32)],
        compiler_params=_params(("arbitrary", "arbitrary")),
        name="conv_prompt",
    )(u, u, w, b)


def _conv_decode_body(u_ref, st_ref, w_ref, wshift_ref, b_ref, c_ref, *, dec_seq):
    for e in range(DEC_GROUP):
        past = st_ref[e]
        for t in range(dec_seq):
            acc = jnp.sum(past * wshift_ref[t], axis=0, keepdims=True) + b_ref[...]
            for t2 in range(t + 1):
                wi = CONV_W - 1 - t + t2
                acc = acc + u_ref[e * dec_seq + t2:e * dec_seq + t2 + 1, :] * w_ref[wi:wi + 1, :]
            c_ref[e * dec_seq + t:e * dec_seq + t + 1, :] = acc


def _conv_decode(u, row0, dec_batch, dec_seq, state, w, wshift, b):
    rows_n = DEC_GROUP * dec_seq
    r0 = row0 // rows_n
    return pl.pallas_call(
        functools.partial(_conv_decode_body, dec_seq=dec_seq),
        grid=(dec_batch // DEC_GROUP,),
        in_specs=[pl.BlockSpec((rows_n, D_MODEL), lambda i: (r0 + i, 0)),
                  pl.BlockSpec((DEC_GROUP, CONV_W - 1, D_MODEL), lambda i: (i, 0, 0)),
                  _full(w.shape), _full(wshift.shape), _full(b.shape)],
        out_specs=pl.BlockSpec((rows_n, D_MODEL), lambda i: (i, 0)),
        out_shape=jax.ShapeDtypeStruct((dec_batch * dec_seq, D_MODEL), F32),
        compiler_params=_params(("arbitrary",)),
        name="conv_decode",
    )(u, state, w, wshift, b)


def _extract_top(s, idx, count, none_rank):
    work = s
    rank = jnp.full(s.shape, none_rank, F32)
    vals = []
    for r in range(count):
        m = jnp.max(work, axis=0, keepdims=True)
        first = jnp.min(jnp.where(work == m, idx, np.float32(1e9)), axis=0, keepdims=True)
        hit = idx == first
        rank = jnp.where(hit, np.float32(r), rank)
        work = jnp.where(hit, -jnp.inf, work)
        vals.append(m)
    return rank, vals


_PEER_CAND = [(a, b) for a in range(PEER_TOPK) for b in range(PEER_TOPK) if (a + 1) * (b + 1) <= PEER_TOPK]
_PEER_CAND_ROWS = -(-len(_PEER_CAND) // 8) * 8


def _extract_by_value(s, count):
    work = s
    rank = jnp.full(s.shape, np.float32(count), F32)
    vals = []
    for r in range(count):
        m = jnp.max(work, axis=0, keepdims=True)
        hit = work == m
        rank = jnp.where(hit, np.float32(r), rank)
        work = jnp.where(hit, -jnp.inf, work)
        vals.append(m)
    ranked = jnp.sum(jnp.where(rank < count, 1.0, 0.0), axis=0, keepdims=True)
    return rank, vals, ranked


def _peer_select_body(q_ref, keys_ref, rank_ref, qe_ref, lq_ref, pe_ref):
    ts = q_ref.shape[0]
    kidx = lax.broadcasted_iota(jnp.int32, (N_KEYS, ts), 0).astype(F32)
    cidx = lax.broadcasted_iota(jnp.int32, (_PEER_CAND_ROWS, ts), 0).astype(F32)
    tidx = lax.broadcasted_iota(jnp.int32, (PEER_TOPK, ts), 0).astype(F32)
    bidx = lax.broadcasted_iota(jnp.int32, (SUBLANES, ts), 0).astype(F32)
    pad =jnp.full((_PEER_CAND_ROWS, ts), -jnp.inf, F32)

    def tables(h, s0, s1, exact):
        if exact:
            rank0, v0 = _extract_top(s0, kidx, PEER_TOPK, PEER_TOPK)
            rank1, v1 = _extract_top(s1, kidx, PEER_TOPK, PEER_TOPK)
        else:
            rank0, v0, n0 = _extract_by_value(s0, PEER_TOPK)
            rank1, v1, n1 = _extract_by_value(s1, PEER_TOPK)
        if exact:
            pairs = list(_PEER_CAND)
            cand = pad
            for ci, (a, b) in enumerate(pairs):
                cand = jnp.where(cidx == ci, v0[a] + v1[b], cand)
            crank, _ = _extract_top(cand, cidx, PEER_TOPK, PEER_TOPK)
            off = None
        else:
            col0 = jnp.zeros((PEER_TOPK, ts), F32)
            col1 = jnp.zeros((PEER_TOPK, ts), F32)
            low1 = jnp.zeros((SUBLANES, ts), F32)
            for r in range(PEER_TOPK):
                col0 = jnp.where(tidx == r, v0[r], col0)
                col1 = jnp.where(tidx == r, v1[r], col1)
                if r < SUBLANES:
                    low1 = jnp.where(bidx == r, v1[r], low1)
            ninf = np.float32(-np.inf)
            groups = [v0[0] + col1, jnp.where(tidx == 0, ninf, col0 + v1[0])]
            pairs = [(0, b) for b in range(PEER_TOPK)] + [(a, 0) if a else None for a in range(PEER_TOPK)]
            rest = [p for p in _PEER_CAND if p[0] and p[1]]
            for a in sorted({p[0] for p in rest}):
                mine = [p for p in rest if p[0] == a]
                if len(mine) > 2:
                    groups.append(jnp.where((bidx >= 1) & (bidx <= len(mine)), v0[a] + low1, ninf))
                    pairs += [(a, b) if 1 <= b <= len(mine) else None for b in range(SUBLANES)]
            loose = [p for p in rest if p not in pairs]
            assert len(loose) <= SUBLANES
            tail = jnp.full((SUBLANES, ts), ninf, F32)
            for r, (a, b) in enumerate(loose):
                tail = jnp.where(bidx == r, v0[a] + v1[b], tail)
            groups.append(tail)
            pairs += loose + [None] * (SUBLANES - len(loose))
            assert sorted(p for p in pairs if p) == sorted(_PEER_CAND)
            cand = jnp.concatenate(groups, axis=0)
            crank, _, nc = _extract_by_value(cand, PEER_TOPK)
            off = jnp.max(jnp.abs(n0 - PEER_TOPK) + jnp.abs(n1 - PEER_TOPK) + jnp.abs(nc - PEER_TOPK))
        picked = jnp.where(crank < PEER_TOPK, 1.0, 0.0)
        e = picked * jnp.exp(cand - (v0[0] + v1[0]))
        z = jnp.sum(e, axis=0, keepdims=True)
        lq = jnp.zeros((N_KEYS, ts), F32)
        for a in range(PEER_TOPK):
            cnt = jnp.zeros((1, ts), F32)
            for ci, p in enumerate(pairs):
                if p is not None and p[0] == a:
                    cnt = cnt + picked[ci:ci + 1]
            lq = jnp.where(rank0 == a, cnt, lq)
        rank_b = rank1.astype(BF16)
        qe_b = jnp.exp(s1 - v1[0]).astype(BF16)
        for k in range(N_KEYS // BF16_ROWS):
            rank_ref[h, k] = pltpu.bitcast(rank_b[k * BF16_ROWS:(k + 1) * BF16_ROWS], jnp.uint32)
            qe_ref[h, k] = pltpu.bitcast(qe_b[k * BF16_ROWS:(k + 1) * BF16_ROWS], jnp.uint32)
        lq_ref[h] = lq
        pe_ref[h] = jnp.exp(s0 - v0[0]) / z
        return off

    def head(h, carry):
        c0 = pl.multiple_of(h * 2 * PEER_DKH, 2 * PEER_DKH)
        q0 = q_ref[:, pl.ds(c0, PEER_DKH)].astype(BF16)
        q1 = q_ref[:, pl.ds(c0 + PEER_DKH, PEER_DKH)].astype(BF16)
        s0 = _dot_nt(keys_ref[h, 0], q0)
        s1 = _dot_nt(keys_ref[h, 1], q1)
        off = tables(h, s0, s1, exact=False)

        @pl.when(off > 0.5)
        def _():
            tables(h, s0, s1, exact=True)

        return carry

    lax.fori_loop(0, PEER_HEADS, head, 0)


def _peer_select(q, keys):
    n = q.shape[0]
    out = jax.ShapeDtypeStruct((PEER_HEADS, N_KEYS, n), F32)
    ospec = pl.BlockSpec((PEER_HEADS, N_KEYS, SEL_TILE), lambda i: (0, 0, i))
    packed = (PEER_HEADS, N_KEYS // BF16_ROWS, BF16_ROWS // 2)
    out_b = jax.ShapeDtypeStruct(packed + (n,), jnp.uint32)
    ospec_b = pl.BlockSpec(packed + (SEL_TILE,), lambda i: (0, 0, 0, i))
    return pl.pallas_call(
        _peer_select_body,
        grid=(n // SEL_TILE,),
        in_specs=[pl.BlockSpec((SEL_TILE, q.shape[1]), lambda i: (i, 0)), _full(keys.shape)],
        out_specs=[ospec_b, ospec_b, ospec, ospec],
        out_shape=[out_b, out_b, out, out],
        compiler_params=_params(("arbitrary",)),
        name="peer_select",
    )(q, keys)


def _peer_dense_body(ht_ref, u_ref, vt_ref, rank_ref, qe_ref, lq_ref, pe_ref, x_ref, o_ref,
                     acc_ref, a_ref, w_ref):
    step = pl.program_id(1)

    @pl.when(step == 0)
    def _():
        acc_ref[...] = jnp.zeros_like(acc_ref)

    quarter = PEER_ROWS * N_KEYS // 4
    for qi in range(4):
        rows = slice(qi * quarter, (qi + 1) * quarter)
        a_ref[rows, :] = _dot(u_ref[rows, :], ht_ref[...])
    sub = (N_KEYS // BF16_ROWS, BF16_ROWS, LANES)
    for il in range(PEER_ROWS):
        for ci in range(TOKEN_TILE // LANES):
            cs = slice(ci * LANES, (ci + 1) * LANES)
            gate = jnp.zeros(sub, BF16)
            for h in range(PEER_HEADS):
                lq = jnp.broadcast_to(lq_ref[h, il:il + 1, cs], sub[1:]).astype(BF16)
                pe = jnp.broadcast_to(pe_ref[h, il:il + 1, cs], sub[1:]).astype(BF16)
                take = pltpu.bitcast(rank_ref[h, :, :, cs], BF16) < lq[None]
                qe = pltpu.bitcast(qe_ref[h, :, :, cs], BF16)
                gate = gate + jnp.where(take, qe, jnp.zeros(sub, BF16)) * pe[None]
            for k in range(sub[0]):
                rs = slice(il * N_KEYS + k * BF16_ROWS, il * N_KEYS + (k + 1) * BF16_ROWS)
                w_ref[rs, cs] = gate[k] * _gelu(a_ref[rs, cs]).astype(BF16)
    acc_ref[...] += _dot(vt_ref[...], w_ref[...])

    @pl.when(step == pl.num_programs(1) - 1)
    def _():
        o_ref[...] = x_ref[...] + acc_ref[...].T


def _peer_dense(ht, u, vt, rank, qe, lq, pe, x):
    n = x.shape[0]
    ex = PEER_ROWS * N_KEYS
    steps = N_KEYS // PEER_ROWS
    tok = pl.BlockSpec((PEER_HEADS, N_KEYS // BF16_ROWS, BF16_ROWS // 2, TOKEN_TILE), lambda t, e: (0, 0, 0, t))
    rowsel = pl.BlockSpec((PEER_HEADS, PEER_ROWS, TOKEN_TILE), lambda t, e: (0, e, t))
    return pl.pallas_call(
        _peer_dense_body,
        grid=(n // TOKEN_TILE, steps),
        in_specs=[pl.BlockSpec((D_MODEL, TOKEN_TILE), lambda t, e: (0, t)),
                  pl.BlockSpec((ex, D_MODEL), lambda t, e: (e, 0)),
                  pl.BlockSpec((D_MODEL, ex), lambda t, e: (0, e)),
                  tok, tok, rowsel, rowsel,
                  pl.BlockSpec((TOKEN_TILE, D_MODEL), lambda t, e: (t, 0))],
        out_specs=pl.BlockSpec((TOKEN_TILE, D_MODEL), lambda t, e: (t, 0)),
        out_shape=jax.ShapeDtypeStruct((n, D_MODEL), F32),
        scratch_shapes=[pltpu.VMEM((D_MODEL, TOKEN_TILE), F32), pltpu.VMEM((ex, TOKEN_TILE), F32),
                        pltpu.VMEM((ex, TOKEN_TILE), BF16)],
        compiler_params=_params(("arbitrary", "arbitrary")),
        name="peer_dense",
    )(ht, u, vt, rank, qe, lq, pe, x)


def _peer(x1, ht, q, keys, u, vt):
    rank, qe, lq, pe = _peer_select(q, keys)
    return _peer_dense(ht, u, vt, rank, qe, lq, pe, x1)


def _block_diag_mean(width):
    idx = np.arange(width) // HEAD_DIM
    return jnp.asarray((idx[:, None] == idx[None, :]).astype(np.float32) / HEAD_DIM)


def kernel(x_prompt, x_sample, cache_swa_k, cache_swa_v, state_hgrn, state_conv, norm_mix_g, norm_ffn_g, w_in0, attn_q_norm_g, attn_k_norm_g, attn_sinks, hgrn_lb, hgrn_o_norm_g, w_out0, conv_w_pw1, conv_b_pw1, conv_w_dw, conv_b_dw, conv_ln_g, conv_ln_b, conv_w_pw2, peer_w_q, peer_keys, peer_u, peer_v):
    batch, seq, _ = x_prompt.shape
    dec_batch, dec_seq, _ = x_sample.shape
    n_p = batch * seq
    n_s = dec_batch * dec_seq
    assert seq % CONV_ROWS == 0 and seq % HGRN_ROWS == 0 and n_p % TOKEN_TILE == 0 and n_s % TOKEN_TILE == 0
    assert dec_batch % DEC_GROUP == 0 and 8 % dec_seq == 0 and dec_seq <= CONV_W - 1

    x = jnp.concatenate([x_prompt.reshape(n_p, D_MODEL), x_sample.reshape(n_s, D_MODEL)], axis=0)
    row = lambda v: v.reshape(1, -1).astype(F32)

    z = _in0(x, row(norm_mix_g[0]), w_in0.astype(BF16))
    d = np.arange(LANES) % HEAD_DIM
    inv = ROPE_THETA ** (-jnp.arange(0, ROT_DIM, 2, dtype=F32) / ROT_DIM)
    inv_lane = jnp.where(jnp.asarray(d < ROT_DIM), inv[d % (ROT_DIM // 2)], 0.0).reshape(1, LANES)
    gq = row(jnp.tile(attn_q_norm_g, N_HEADS_A))
    gk = row(jnp.tile(attn_k_norm_g, N_KV_A))
    segq, segk = _block_diag_mean(QA), _block_diag_mean(KVA)
    sinks = attn_sinks.astype(F32)
    attn_p, k_win_p, v_win_p = _attn_prompt(z, batch, seq, gq, gk, inv_lane, segq, segk, sinks)
    kc = cache_swa_k.reshape(dec_batch * WINDOW, KVA)
    vc = cache_swa_v.reshape(dec_batch * WINDOW, KVA)
    attn_s, k_new, v_new = _attn_decode(z, n_p, dec_batch, dec_seq, kc, vc, gq, gk, inv_lane, segq, segk, sinks)

    go = row(hgrn_o_norm_g)
    tril = jnp.asarray(np.tril(np.ones((HGRN_CHUNK, HGRN_CHUNK), np.float32)), BF16)
    hg_p, s_p = _hgrn_prompt(z, batch, seq, hgrn_lb, go, tril)
    r = np.arange(LANES)
    tril_dec = jnp.asarray(((r[:, None] // dec_seq == r[None, :] // dec_seq) & (r[None, :] <= r[:, None]))
                           .astype(np.float32))
    hg_s, s_s = _hgrn_decode(z, n_p, dec_batch, dec_seq, state_hgrn, hgrn_lb, go, tril_dec)

    attn = jnp.concatenate([attn_p, attn_s], axis=0)
    hg = jnp.concatenate([hg_p, hg_s], axis=0)
    x1, ht, q = _mix_out(_out0_body, [attn, hg], x, [w_out0.astype(BF16)], row(norm_ffn_g[0]),
                         peer_w_q[0].astype(BF16), "mix0_out")
    x2 = _peer(x1, ht, q, peer_keys[0].astype(BF16), peer_u[0].astype(BF16), peer_v[0].astype(BF16).T)

    u = _glu(x2, row(norm_mix_g[1]), conv_w_pw1.astype(BF16), row(conv_b_pw1))
    wdw = jnp.concatenate([conv_w_dw, jnp.zeros((1, D_MODEL), F32)], axis=0)
    bdw = row(conv_b_dw)
    c_p = _conv_prompt(u, batch, seq, wdw, bdw)
    wshift = jnp.stack([jnp.concatenate([jnp.zeros((t, D_MODEL), F32), conv_w_dw[:CONV_W - 1 - t]], axis=0)
                        for t in range(dec_seq)])
    c_s = _conv_decode(u, n_p, dec_batch, dec_seq, state_conv, wdw, wshift, bdw)
    c = jnp.concatenate([c_p, c_s], axis=0)
    x3, ht, q = _mix_out(_post1_body, [c], x2, [row(conv_ln_g), row(conv_ln_b), conv_w_pw2.astype(BF16)],
                         row(norm_ffn_g[1]), peer_w_q[1].astype(BF16), "mix1_out")
    x4 = _peer(x3, ht, q, peer_keys[1].astype(BF16), peer_u[1].astype(BF16), peer_v[1].astype(BF16).T)

    kv = lambda t: t.reshape(t.shape[0], WINDOW, N_KV_A, HEAD_DIM)
    k_win_s = jnp.concatenate([cache_swa_k[:, dec_seq:], k_new.reshape(dec_batch, dec_seq, N_KV_A, HEAD_DIM)], axis=1)
    v_win_s = jnp.concatenate([cache_swa_v[:, dec_seq:], v_new.reshape(dec_batch, dec_seq, N_KV_A, HEAD_DIM)], axis=1)
    u_p = u[:n_p].reshape(batch, seq, D_MODEL)
    u_s = u[n_p:].reshape(dec_batch, dec_seq, D_MODEL)
    conv_buf_p = u_p[:, seq - (CONV_W - 1):]
    conv_buf_s = jnp.concatenate([state_conv[:, dec_seq:], u_s], axis=1)
    return (x4[:n_p].reshape(batch, seq, D_MODEL), x4[n_p:].reshape(dec_batch, dec_seq, D_MODEL),
            kv(k_win_p), kv(v_win_p), s_p, conv_buf_p, k_win_s, v_win_s, s_s, conv_buf_s)
```

```python
import functools

import numpy as np
import jax
import jax.numpy as jnp
from jax import lax
from jax.experimental import pallas as pl
from jax.experimental.pallas import tpu as pltpu

F32 = jnp.float32
BF16 = jnp.bfloat16
HI = lax.Precision.HIGHEST

D_MODEL = 1024
PAST_LEN = 8192
HEAD_DIM = 64
N_HEADS_A = 8
N_KV_A = 2
KV_REP = N_HEADS_A // N_KV_A
WINDOW = 128
ROT_DIM = HEAD_DIM // 4
ROPE_THETA = 500000.0
ATTN_SCALE = HEAD_DIM ** -0.5
NEG_INF = -1e30
N_HEADS_B = 4
DK_B = 128
DV_B = 128
CONV_W = 31
N_KEYS = 128
PEER_HEADS = 8
PEER_TOPK = 16
PEER_DKH = 128
NORM_EPS = 1e-6

QA = N_HEADS_A * HEAD_DIM
KVA = N_KV_A * HEAD_DIM
QB = N_HEADS_B * DK_B
VB = N_HEADS_B * DV_B
IN0_WIDTH = QA + 2 * KVA + 2 * QB + 2 * VB
COL_K = QA // 128
COL_V = (QA + KVA) // 128
COL_QB = (QA + 2 * KVA) // 128
COL_FB = COL_QB + QB // 128
COL_IB = COL_FB + QB // 128
COL_GB = COL_IB + VB // 128

LANES = 128
SUBLANES = 8
TOKEN_TILE = 512
SEL_TILE = 256
HGRN_CHUNK = 64
HGRN_SUB = 16
HGRN_ROWS = 256
HGRN_HEADS_PER_STEP = 2
CONV_ROWS = 512
CONV_HALO = 32
CONV_CHUNK = 128
PEER_ROWS = 16
BF16_ROWS = 16
DEC_GROUP = 8
VMEM_LIMIT = 48 * 1024 * 1024


def _dot(a, b, prec=None):
    return jnp.dot(a, b, preferred_element_type=F32, precision=prec)


def _dot_nt(a, b, prec=None):
    return lax.dot_general(a, b, (((1,), (1,)), ((), ())), preferred_element_type=F32, precision=prec)


def _rms(x, g):
    return x * lax.rsqrt(jnp.mean(x * x, axis=-1, keepdims=True) + NORM_EPS) * g


def _silu(x):
    return x * jax.nn.sigmoid(x)


def _gelu(x):
    return 0.5 * x * (1.0 + lax.erf(x * np.float32(0.7071067811865476)))


def _params(sem, flags=None):
    return pltpu.CompilerParams(dimension_semantics=sem, vmem_limit_bytes=VMEM_LIMIT, flags=flags)


def _full(shape):
    n = len(shape)
    return pl.BlockSpec(shape, lambda *_: (0,) * n)


def _row_inputs(arrs):
    specs, ops, layout = [], [], []
    for a in arrs:
        if isinstance(a, tuple):
            tp = a[0].shape[0] // TOKEN_TILE
            width = a[0].shape[1]
            specs += [pl.BlockSpec((TOKEN_TILE, width), lambda i, tp=tp: (jnp.minimum(i, tp - 1), 0)),
                      pl.BlockSpec((TOKEN_TILE, width), lambda i, tp=tp: (jnp.maximum(i - tp, 0), 0))]
            ops += list(a)
            layout.append(tp)
        else:
            specs.append(pl.BlockSpec((TOKEN_TILE, a.shape[1]), lambda i: (i, 0)))
            ops.append(a)
            layout.append(None)
    return specs, ops, tuple(layout)


def _row_values(layout, refs):
    i = pl.program_id(0)
    vals, k = [], 0
    for tp in layout:
        if tp is None:
            vals.append(refs[k][...])
            k += 1
        else:
            vals.append(jnp.where(i < tp, refs[k][...], refs[k + 1][...]))
            k += 2
    return vals, refs[k:]


def _rows_of(a):
    return a[0].shape[0] + a[1].shape[0] if isinstance(a, tuple) else a.shape[0]


def _in0_body(*refs, layout):
    (x,), (g_ref, w_ref, z_ref) = _row_values(layout, refs)
    h = _rms(x, g_ref[...]).astype(BF16)
    z_ref[...] = _dot(h, w_ref[...])


def _in0(x, g, w):
    n = _rows_of(x)
    width = w.shape[1]
    specs, ops, layout = _row_inputs([x])
    return pl.pallas_call(
        functools.partial(_in0_body, layout=layout),
        grid=(n // TOKEN_TILE,),
        in_specs=specs + [_full((1, D_MODEL)), _full((D_MODEL, width))],
        out_specs=pl.BlockSpec((TOKEN_TILE, width), lambda i: (i, 0)),
        out_shape=jax.ShapeDtypeStruct((n, width), F32),
        compiler_params=_params(("arbitrary",)),
        name="in0_proj",
    )(*ops, g, w)


def _glu_body(x_ref, g_ref, w_ref, b_ref, u_ref):
    h = _rms(x_ref[...], g_ref[...]).astype(BF16)
    a = _dot(h, w_ref[...]) + b_ref[...]
    u_ref[...] = a[:, :D_MODEL] * jax.nn.sigmoid(a[:, D_MODEL:])


def _glu(x, g, w, b):
    n = x.shape[0]
    return pl.pallas_call(
        _glu_body,
        grid=(n // TOKEN_TILE,),
        in_specs=[pl.BlockSpec((TOKEN_TILE, D_MODEL), lambda i: (i, 0)), _full((1, D_MODEL)),
                  _full((D_MODEL, 2 * D_MODEL)), _full((1, 2 * D_MODEL))],
        out_specs=pl.BlockSpec((TOKEN_TILE, D_MODEL), lambda i: (i, 0)),
        out_shape=jax.ShapeDtypeStruct((n, D_MODEL), F32),
        compiler_params=_params(("arbitrary",)),
        name="conv_glu",
    )(x, g, w, b)


def _ffn_query(x1, gf_ref, wq_ref, x1_ref, ht_ref, q_ref):
    x1_ref[...] = x1
    h2 = _rms(x1, gf_ref[...])
    ht_ref[...] = pltpu.bitcast(h2.T.astype(BF16), jnp.uint32)
    q_ref[...] = _dot(h2.astype(BF16), wq_ref[...])


def _out0_body(*refs, layout):
    (attn, hg, x), (w_ref, gf_ref, wq_ref, x1_ref, ht_ref, q_ref) = _row_values(layout, refs)
    m = _dot(attn.astype(BF16), w_ref[:QA, :]) + _dot(hg.astype(BF16), w_ref[QA:, :])
    _ffn_query(x + m, gf_ref, wq_ref, x1_ref, ht_ref, q_ref)


def _post1_body(*refs, layout):
    (c, x), (lg_ref, lb_ref, w_ref, gf_ref, wq_ref, x1_ref, ht_ref, q_ref) = _row_values(layout, refs)
    mu = jnp.mean(c, axis=-1, keepdims=True)
    d = c - mu
    var = jnp.mean(d * d, axis=-1, keepdims=True)
    ln = d * lax.rsqrt(var + NORM_EPS) * lg_ref[...] + lb_ref[...]
    y = _dot(_silu(ln).astype(BF16), w_ref[...])
    _ffn_query(x + y, gf_ref, wq_ref, x1_ref, ht_ref, q_ref)


def _mix_out(body, acts, x, consts, gf, wq, name):
    n = _rows_of(x)
    qw = wq.shape[1]
    row = lambda w: pl.BlockSpec((TOKEN_TILE, w), lambda i: (i, 0))
    specs, ops, layout = _row_inputs(list(acts) + [x])
    return pl.pallas_call(
        functools.partial(body, layout=layout),
        grid=(n // TOKEN_TILE,),
        in_specs=specs + [_full(c.shape) for c in consts] + [_full(gf.shape), _full(wq.shape)],
        out_specs=[row(D_MODEL), pl.BlockSpec((D_MODEL // 2, TOKEN_TILE), lambda i: (0, i)), row(qw)],
        out_shape=[jax.ShapeDtypeStruct((n, D_MODEL), F32), jax.ShapeDtypeStruct((D_MODEL // 2, n), jnp.uint32),
                   jax.ShapeDtypeStruct((n, qw), F32)],
        compiler_params=_params(("arbitrary",)),
        name=name,
    )(*ops, *consts, gf, wq)


def _head_norm(x, g, seg):
    ms = _dot(x * x, seg, HI)
    return x * lax.rsqrt(ms + NORM_EPS) * g


def _rope(x, cos, sin, first_half):
    half = ROT_DIM // 2
    width = x.shape[1]
    up = pltpu.roll(x, width - half, axis=1)
    dn = pltpu.roll(x, half, axis=1)
    return x * cos + jnp.where(first_half, -up, dn) * sin


def _rope_tables(pos, inv_ref, reps):
    ang = pos * inv_ref[...]
    c, s = jnp.cos(ang), jnp.sin(ang)
    if reps > 1:
        c, s = (jnp.concatenate([t] * reps, axis=1) for t in (c, s))
    lane = lax.broadcasted_iota(jnp.int32, c.shape, 1)
    return c, s, (lane % HEAD_DIM) < (ROT_DIM // 2)


def _stack_heads(x, g):
    return jnp.concatenate(
        [x[:, (g * KV_REP + r) * HEAD_DIM:(g * KV_REP + r + 1) * HEAD_DIM] for r in range(KV_REP)], axis=0)


def _attn_prompt_body(q_ref, k_ref, v_ref, gq_ref, gk_ref, inv_ref, segq_ref, segk_ref, sink_ref,
                      o_ref, kw_ref, vw_ref, kprev_ref, vprev_ref):
    j = pl.program_id(1)
    w = WINDOW

    @pl.when(j == 0)
    def _():
        kprev_ref[...] = jnp.zeros_like(kprev_ref)
        vprev_ref[...] = jnp.zeros_like(vprev_ref)

    rows = lax.broadcasted_iota(jnp.int32, (w, LANES), 0)
    pos = (j * w + rows).astype(F32)
    ck, sk, fk = _rope_tables(pos, inv_ref, 1)
    cq, sq, fq = _rope_tables(pos, inv_ref, QA // LANES)
    q = _rope(_head_norm(q_ref[...], gq_ref[...], segq_ref[...]), cq, sq, fq)
    k = _rope(_head_norm(k_ref[...], gk_ref[...], segk_ref[...]), ck, sk, fk)
    v = v_ref[...]
    kp = kprev_ref[...]
    vp = vprev_ref[...]

    qi = lax.broadcasted_iota(jnp.int32, (KV_REP * w, w), 0) % w
    ki = lax.broadcasted_iota(jnp.int32, (KV_REP * w, w), 1)
    m_own = ki <= qi
    m_prev = (ki > qi) & (j > 0)
    rep = lax.broadcasted_iota(jnp.int32, (KV_REP * w, 1), 0) // w
    for g in range(N_KV_A):
        sl = slice(g * HEAD_DIM, (g + 1) * HEAD_DIM)
        qg = _stack_heads(q, g).astype(BF16)
        s_own = jnp.where(m_own, _dot_nt(qg, k[:, sl].astype(BF16)) * ATTN_SCALE, NEG_INF)
        s_prev = jnp.where(m_prev, _dot_nt(qg, kp[:, sl].astype(BF16)) * ATTN_SCALE, NEG_INF)
        sink = jnp.zeros((KV_REP * w, 1), F32)
        for r in range(KV_REP):
            sink = jnp.where(rep == r, sink_ref[g * KV_REP + r], sink)
        mx = jnp.maximum(jnp.maximum(jnp.max(s_own, axis=-1, keepdims=True),
                                     jnp.max(s_prev, axis=-1, keepdims=True)), sink)
        e_own = jnp.exp(s_own - mx)
        e_prev = jnp.exp(s_prev - mx)
        den = (jnp.sum(e_own, axis=-1, keepdims=True) + jnp.sum(e_prev, axis=-1, keepdims=True)
               + jnp.exp(sink - mx))
        o = (_dot(e_own.astype(BF16), v[:, sl].astype(BF16))
             + _dot(e_prev.astype(BF16), vp[:, sl].astype(BF16))) / den
        for r in range(KV_REP):
            hq = g * KV_REP + r
            o_ref[:, hq * HEAD_DIM:(hq + 1) * HEAD_DIM] = o[r * w:(r + 1) * w]

    kprev_ref[...] = k
    vprev_ref[...] = v

    @pl.when(j == pl.num_programs(1) - 1)
    def _():
        kw_ref[0] = k
        vw_ref[0] = v


def _attn_prompt(z, batch, seq, gq, gk, inv, segq, segk, sinks):
    nb = seq // WINDOW
    blk = lambda width, col: pl.BlockSpec((WINDOW, width), lambda b, j: (b * nb + j, col))
    return pl.pallas_call(
        _attn_prompt_body,
        grid=(batch, nb),
        in_specs=[blk(QA, 0), blk(KVA, COL_K), blk(KVA, COL_V), _full(gq.shape), _full(gk.shape),
                  _full(inv.shape), _full(segq.shape), _full(segk.shape),
                  pl.BlockSpec(memory_space=pltpu.SMEM)],
        out_specs=[pl.BlockSpec((WINDOW, QA), lambda b, j: (b * nb + j, 0)),
                   pl.BlockSpec((1, WINDOW, KVA), lambda b, j: (b, 0, 0)),
                   pl.BlockSpec((1, WINDOW, KVA), lambda b, j: (b, 0, 0))],
        out_shape=[jax.ShapeDtypeStruct((batch * seq, QA), F32),
                   jax.ShapeDtypeStruct((batch, WINDOW, KVA), F32),
                   jax.ShapeDtypeStruct((batch, WINDOW, KVA), F32)],
        scratch_shapes=[pltpu.VMEM((WINDOW, KVA), F32), pltpu.VMEM((WINDOW, KVA), F32)],
        compiler_params=_params(("arbitrary", "arbitrary")),
        name="swa_prompt",
    )(z, z, z, gq, gk, inv, segq, segk, sinks)


def _attn_decode_body(q_ref, k_ref, v_ref, kc_ref, vc_ref, gq_ref, gk_ref, inv_ref, segq_ref, segk_ref,
                      sink_ref, o_ref, kn_ref, vn_ref, *, dec_seq):
    rows_n = DEC_GROUP * dec_seq
    rows = lax.broadcasted_iota(jnp.int32, (rows_n, LANES), 0)
    pos = (PAST_LEN + rows % dec_seq).astype(F32)
    ck, sk, fk = _rope_tables(pos, inv_ref, 1)
    cq, sq, fq = _rope_tables(pos, inv_ref, QA // LANES)
    q = _rope(_head_norm(q_ref[...], gq_ref[...], segq_ref[...]), cq, sq, fq)
    k = _rope(_head_norm(k_ref[...], gk_ref[...], segk_ref[...]), ck, sk, fk)
    v = v_ref[...]
    kn_ref[...] = k
    vn_ref[...] = v

    nq = KV_REP * rows_n
    nc = DEC_GROUP * WINDOW

    def qrow(shape):
        r = lax.broadcasted_iota(jnp.int32, shape, 0) % rows_n
        return r // dec_seq, r % dec_seq

    bq, tq = qrow((nq, nc))
    col = lax.broadcasted_iota(jnp.int32, (nq, nc), 1)
    m_cache = (col // WINDOW == bq) & (col % WINDOW > tq)
    bq, tq = qrow((nq, rows_n))
    col = lax.broadcasted_iota(jnp.int32, (nq, rows_n), 1)
    m_new = (col // dec_seq == bq) & (col % dec_seq <= tq)
    rep = lax.broadcasted_iota(jnp.int32, (nq, 1), 0) // rows_n
    for g in range(N_KV_A):
        sl = slice(g * HEAD_DIM, (g + 1) * HEAD_DIM)
        qg = _stack_heads(q, g).astype(BF16)
        s_c = jnp.where(m_cache, _dot_nt(qg, kc_ref[:, sl].astype(BF16)) * ATTN_SCALE, NEG_INF)
        s_n = jnp.where(m_new, _dot_nt(qg, k[:, sl].astype(BF16)) * ATTN_SCALE, NEG_INF)
        sink = jnp.zeros((nq, 1), F32)
        for r in range(KV_REP):
            sink = jnp.where(rep == r, sink_ref[g * KV_REP + r], sink)
        mx = jnp.maximum(jnp.maximum(jnp.max(s_c, axis=-1, keepdims=True),
                                     jnp.max(s_n, axis=-1, keepdims=True)), sink)
        e_c = jnp.exp(s_c - mx)
        e_n = jnp.exp(s_n - mx)
        den = jnp.sum(e_c, axis=-1, keepdims=True) + jnp.sum(e_n, axis=-1, keepdims=True) + jnp.exp(sink - mx)
        o = (_dot(e_c.astype(BF16), vc_ref[:, sl].astype(BF16))
             + _dot(e_n.astype(BF16), v[:, sl].astype(BF16))) / den
        for r in range(KV_REP):
            hq = g * KV_REP + r
            o_ref[:, hq * HEAD_DIM:(hq + 1) * HEAD_DIM] = o[r * rows_n:(r + 1) * rows_n]


def _attn_decode(z, row0, dec_batch, dec_seq, kc, vc, gq, gk, inv, segq, segk, sinks):
    rows_n = DEC_GROUP * dec_seq
    r0 = row0 // rows_n
    blk = lambda width, col: pl.BlockSpec((rows_n, width), lambda i: (r0 + i, col))
    cache = pl.BlockSpec((DEC_GROUP * WINDOW, KVA), lambda i: (i, 0))
    n = dec_batch * dec_seq
    return pl.pallas_call(
        functools.partial(_attn_decode_body, dec_seq=dec_seq),
        grid=(dec_batch // DEC_GROUP,),
        in_specs=[blk(QA, 0), blk(KVA, COL_K), blk(KVA, COL_V), cache, cache, _full(gq.shape),
                  _full(gk.shape), _full(inv.shape), _full(segq.shape), _full(segk.shape),
                  pl.BlockSpec(memory_space=pltpu.SMEM)],
        out_specs=[pl.BlockSpec((rows_n, QA), lambda i: (i, 0)), pl.BlockSpec((rows_n, KVA), lambda i: (i, 0)),
                   pl.BlockSpec((rows_n, KVA), lambda i: (i, 0))],
        out_shape=[jax.ShapeDtypeStruct((n, QA), F32), jax.ShapeDtypeStruct((n, KVA), F32),
                   jax.ShapeDtypeStruct((n, KVA), F32)],
        compiler_params=_params(("arbitrary",)),
        name="swa_decode",
    )(z, z, z, kc, vc, gq, gk, inv, segq, segk, sinks)


def _hgrn_lower_bound(lb_ref):
    l = lb_ref[...]
    e = jnp.exp(l - jnp.max(l, axis=0, keepdims=True))
    return e[0:1] / jnp.sum(e, axis=0, keepdims=True)


def _hgrn_gates(qb, fb, lb):
    logf = jnp.log(lb + (1.0 - lb) * jax.nn.sigmoid(fb))
    kb = (1.0 - lb) * jax.nn.sigmoid(-fb)
    return _silu(qb), kb, logf


def _hgrn_out(o, g, gate):
    return _rms(o, g) * _silu(gate)


def _hgrn_chunk(qh, kb, ih, logf, st, tril):
    c = qh.shape[0]
    hi = logf.astype(BF16)
    rest = logf - hi.astype(F32)
    mid = rest.astype(BF16)
    b = _dot(tril, hi) + _dot(tril, mid) + _dot(tril, (rest - mid.astype(F32)).astype(BF16))
    ones = jnp.ones((DK_B, LANES), BF16)
    st_b = st.astype(BF16)
    ih_b = ih.astype(BF16)
    o_parts = []
    row = lax.broadcasted_iota(jnp.int32, (c, 1), 0)
    trow = lax.broadcasted_iota(jnp.int32, (HGRN_SUB, 1), 0)
    for blk in range(c // HGRN_SUB):
        t0 = blk * HGRN_SUB
        bi = b[t0:t0 + HGRN_SUB]
        qi = qh[t0:t0 + HGRN_SUB]
        if blk == 0:
            oi = _dot_nt((qi * jnp.exp(bi)).astype(BF16), st_b)
        else:
            base = b[t0 - 1:t0]
            qd = qi * jnp.exp(bi - base)
            kd = jnp.where(row < t0, kb * jnp.exp(jnp.minimum(base - b, 0.0)), 0.0)
            a_off = _dot_nt(qd.astype(BF16), kd.astype(BF16))
            oi = _dot_nt((qd * jnp.exp(base)).astype(BF16), st_b) + _dot(a_off.astype(BF16), ih_b)
        prods = []
        for s in range(HGRN_SUB):
            e = jnp.exp(jnp.where(trow >= s, bi - bi[s:s + 1], 0.0))
            prods.append(qi * e * kb[t0 + s:t0 + s + 1])
        a_diag = _dot(jnp.concatenate(prods, axis=0).astype(BF16), ones)
        for s in range(HGRN_SUB):
            a = a_diag[s * HGRN_SUB:(s + 1) * HGRN_SUB]
            oi = oi + jnp.where(trow >= s, a, 0.0) * ih[t0 + s:t0 + s + 1]
        o_parts.append(oi)
    last = b[c - 1:c]
    st_new = st * jnp.exp(last) + _dot(ih.T.astype(BF16), (kb * jnp.exp(last - b)).astype(BF16))
    return jnp.concatenate(o_parts, axis=0), st_new


def _hgrn_prompt_body(qb_ref, fb_ref, ib_ref, gb_ref, lb_ref, go_ref, tril_ref, o_ref, s_ref, st_ref):
    tb = pl.program_id(2)

    @pl.when(tb == 0)
    def _():
        st_ref[...] = jnp.zeros_like(st_ref)

    tril = tril_ref[...]
    heads = [slice(h * DK_B, (h + 1) * DK_B) for h in range(HGRN_HEADS_PER_STEP)]
    lbs = [_hgrn_lower_bound(lb_ref.at[:, hs]) for hs in heads]
    sts = [st_ref[h] for h in range(HGRN_HEADS_PER_STEP)]
    for c in range(HGRN_ROWS // HGRN_CHUNK):
        rs = slice(c * HGRN_CHUNK, (c + 1) * HGRN_CHUNK)
        for h, hs in enumerate(heads):
            qh, kb, logf = _hgrn_gates(qb_ref[rs, hs], fb_ref[rs, hs], lbs[h])
            o, sts[h] = _hgrn_chunk(qh, kb, ib_ref[rs, hs], logf, sts[h], tril)
            o_ref[rs, hs] = _hgrn_out(o, go_ref[...], gb_ref[rs, hs])
    for h in range(HGRN_HEADS_PER_STEP):
        st_ref[h] = sts[h]

    @pl.when(tb == pl.num_programs(2) - 1)
    def _():
        for h in range(HGRN_HEADS_PER_STEP):
            s_ref[0, h] = sts[h].T


def _hgrn_prompt(z, batch, seq, lb, go, tril):
    nt = seq // HGRN_ROWS
    hp = HGRN_HEADS_PER_STEP
    assert N_HEADS_B % hp == 0 and all(c % hp == 0 for c in (COL_QB, COL_FB, COL_IB, COL_GB))
    blk = lambda col: pl.BlockSpec((HGRN_ROWS, hp * DK_B), lambda b, h, t: (b * nt + t, col // hp + h))
    return pl.pallas_call(
        _hgrn_prompt_body,
        grid=(batch, N_HEADS_B // hp, nt),
        in_specs=[blk(COL_QB), blk(COL_FB), blk(COL_IB), blk(COL_GB),
                  pl.BlockSpec((lb.shape[0], hp * DK_B), lambda b, h, t: (0, h)), _full(go.shape), _full(tril.shape)],
        out_specs=[pl.BlockSpec((HGRN_ROWS, hp * DV_B), lambda b, h, t: (b * nt + t, h)),
                   pl.BlockSpec((1, hp, DK_B, DV_B), lambda b, h, t: (b, h, 0, 0))],
        out_shape=[jax.ShapeDtypeStruct((batch * seq, VB), F32),
                   jax.ShapeDtypeStruct((batch, N_HEADS_B, DK_B, DV_B), F32)],
        scratch_shapes=[pltpu.VMEM((hp, DV_B, DK_B), F32)],
        compiler_params=_params(("arbitrary", "arbitrary", "arbitrary")),
        name="hgrn_prompt",
    )(z, z, z, z, lb, go, tril)


def _pad_rows(x, rows):
    return jnp.concatenate([x, jnp.zeros((rows - x.shape[0], x.shape[1]), x.dtype)], axis=0)


def _hgrn_decode_body(z_ref, s0_ref, lb_ref, go_ref, tril_ref, o_ref, s_ref, *, dec_seq):
    rows_n = z_ref.shape[0]
    groups = rows_n // dec_seq
    row = lax.broadcasted_iota(jnp.int32, (rows_n, 1), 0)
    for h in range(N_HEADS_B):
        col = lambda c0: slice((c0 + h) * LANES, (c0 + h + 1) * LANES)
        hs = slice(h * DK_B, (h + 1) * DK_B)
        lb = _hgrn_lower_bound(lb_ref.at[:, hs])
        qh, kb, logf = _hgrn_gates(z_ref[:, col(COL_QB)], z_ref[:, col(COL_FB)], lb)
        ih = z_ref[:, col(COL_IB)]
        b = _dot(tril_ref[...], _pad_rows(logf, LANES), HI)[:rows_n]
        b_t = _pad_rows(b, LANES).T
        ih_pad = _pad_rows(ih, LANES)
        o = jnp.zeros((rows_n, DV_B), F32)
        for e in range(groups):
            mine = (row // dec_seq) == e
            s0 = s0_ref[e, h]
            o = o + _dot(jnp.where(mine, qh * jnp.exp(b), 0.0), s0, HI)
            for s in range(dec_seq):
                r = e * dec_seq + s
                live = mine & (row >= r)
                ex = jnp.exp(jnp.where(live, b - b[r:r + 1], 0.0))
                a = jnp.sum(qh * ex * kb[r:r + 1], axis=-1, keepdims=True)
                o = o + jnp.where(live, a, 0.0) * ih[r:r + 1]
            r_last = (e + 1) * dec_seq - 1
            last = b[r_last:r_last + 1]
            kd = jnp.where(mine, kb * jnp.exp(jnp.minimum(last - b, 0.0)), 0.0)
            s_ref[e, h] = s0 * jnp.exp(b_t[:, r_last:r_last + 1]) + _dot(_pad_rows(kd, LANES).T, ih_pad, HI)
        o_ref[:, hs] = _hgrn_out(o, go_ref[...], z_ref[:, col(COL_GB)])


def _hgrn_decode(z, row0, dec_batch, dec_seq, s0, lb, go, tril):
    groups = 8 // dec_seq
    rows_n = groups * dec_seq
    r0 = row0 // rows_n
    st = pl.BlockSpec((groups, N_HEADS_B, DK_B, DV_B), lambda i: (i, 0, 0, 0))
    return pl.pallas_call(
        functools.partial(_hgrn_decode_body, dec_seq=dec_seq),
        grid=(dec_batch // groups,),
        in_specs=[pl.BlockSpec((rows_n, z.shape[1]), lambda i: (r0 + i, 0)), st,
                  _full(lb.shape), _full(go.shape), _full(tril.shape)],
        out_specs=[pl.BlockSpec((rows_n, VB), lambda i: (i, 0)), st],
        out_shape=[jax.ShapeDtypeStruct((dec_batch * dec_seq, VB), F32),
                   jax.ShapeDtypeStruct(s0.shape, F32)],
        compiler_params=_params(("arbitrary",)),
        name="hgrn_decode",
    )(z, s0, lb, go, tril)


def _conv_prompt_body(cur_ref, prev_ref, w_ref, b_ref, c_ref, ext_ref, sh_ref):
    t = pl.program_id(1)
    ext_ref[:CONV_HALO, :] = jnp.where(t > 0, prev_ref[...], 0.0)
    ext_ref[CONV_HALO:, :] = cur_ref[...]
    lead = CONV_HALO - (CONV_W - 1)
    chunk = CONV_CHUNK

    def cols(ci, carry):
        cs = pl.ds(pl.multiple_of(ci * LANES, LANES), LANES)
        for r in range(CONV_ROWS // chunk):
            acc = jnp.zeros((chunk, LANES), F32) + b_ref[:, cs]
            for res in range(SUBLANES):
                taps = range(res, CONV_W, SUBLANES)
                span = chunk + taps[-1] - res
                sh_ref[:span, :] = ext_ref[pl.ds(r * chunk + lead + res, span), cs]
                for w in taps:
                    acc = acc + sh_ref[w - res:w - res + chunk, :] * w_ref[pl.ds(w, 1), cs]
            c_ref[pl.ds(r * chunk, chunk), cs] = acc
        return carry

    lax.fori_loop(0, D_MODEL // LANES, cols, 0)


def _conv_prompt(u, batch, seq, w, b):
    nt = seq // CONV_ROWS
    per = CONV_ROWS // CONV_HALO
    return pl.pallas_call(
        _conv_prompt_body,
        grid=(batch, nt),
        in_specs=[pl.BlockSpec((CONV_ROWS, D_MODEL), lambda bi, t: (bi * nt + t, 0)),
                  pl.BlockSpec((CONV_HALO, D_MODEL), lambda bi, t: (jnp.maximum((bi * nt + t) * per - 1, 0), 0)),
                  _full(w.shape), _full(b.shape)],
        out_specs=pl.BlockSpec((CONV_ROWS, D_MODEL), lambda bi, t: (bi * nt + t, 0)),
        out_shape=jax.ShapeDtypeStruct((batch * seq, D_MODEL), F32),
        scratch_shapes=[pltpu.VMEM((CONV_HALO + CONV_ROWS, D_MODEL), F32),
                        pltpu.VMEM((CONV_CHUNK + CONV_HALO, LANES), F32)],
        compiler_params=_params(("arbitrary", "arbitrary")),
        name="conv_prompt",
    )(u, u, w, b)


def _conv_decode_body(u_ref, st_ref, w_ref, wshift_ref, b_ref, c_ref, *, dec_seq):
    for e in range(DEC_GROUP):
        past = st_ref[e]
        for t in range(dec_seq):
            acc = jnp.sum(past * wshift_ref[t], axis=0, keepdims=True) + b_ref[...]
            for t2 in range(t + 1):
                wi = CONV_W - 1 - t + t2
                acc = acc + u_ref[e * dec_seq + t2:e * dec_seq + t2 + 1, :] * w_ref[wi:wi + 1, :]
            c_ref[e * dec_seq + t:e * dec_seq + t + 1, :] = acc


def _conv_decode(u, row0, dec_batch, dec_seq, state, w, wshift, b):
    rows_n = DEC_GROUP * dec_seq
    r0 = row0 // rows_n
    return pl.pallas_call(
        functools.partial(_conv_decode_body, dec_seq=dec_seq),
        grid=(dec_batch // DEC_GROUP,),
        in_specs=[pl.BlockSpec((rows_n, D_MODEL), lambda i: (r0 + i, 0)),
                  pl.BlockSpec((DEC_GROUP, CONV_W - 1, D_MODEL), lambda i: (i, 0, 0)),
                  _full(w.shape), _full(wshift.shape), _full(b.shape)],
        out_specs=pl.BlockSpec((rows_n, D_MODEL), lambda i: (i, 0)),
        out_shape=jax.ShapeDtypeStruct((dec_batch * dec_seq, D_MODEL), F32),
        compiler_params=_params(("arbitrary",)),
        name="conv_decode",
    )(u, state, w, wshift, b)


def _extract_top(s, idx, count, none_rank):
    work = s
    rank = jnp.full(s.shape, none_rank, F32)
    vals = []
    for r in range(count):
        m = jnp.max(work, axis=0, keepdims=True)
        first = jnp.min(jnp.where(work == m, idx, np.float32(1e9)), axis=0, keepdims=True)
        hit = idx == first
        rank = jnp.where(hit, np.float32(r), rank)
        work = jnp.where(hit, -jnp.inf, work)
        vals.append(m)
    return rank, vals


_PEER_CAND = [(a, b) for a in range(PEER_TOPK) for b in range(PEER_TOPK) if (a + 1) * (b + 1) <= PEER_TOPK]
_PEER_CAND_ROWS = -(-len(_PEER_CAND) // 8) * 8


def _extract_by_value(s, count):
    work = s
    rank = jnp.full(s.shape, np.float32(count), F32)
    vals = []
    for r in range(count):
        m = jnp.max(work, axis=0, keepdims=True)
        hit = work == m
        rank = jnp.where(hit, np.float32(r), rank)
        work = jnp.where(hit, -jnp.inf, work)
        vals.append(m)
    ranked = jnp.sum(jnp.where(rank < count, 1.0, 0.0), axis=0, keepdims=True)
    return rank, vals, ranked


def _peer_select_body(q_ref, keys_ref, rank_ref, qe_ref, lq_ref, pe_ref):
    ts = q_ref.shape[0]
    kidx = lax.broadcasted_iota(jnp.int32, (N_KEYS, ts), 0).astype(F32)
    cidx = lax.broadcasted_iota(jnp.int32, (_PEER_CAND_ROWS, ts), 0).astype(F32)
    tidx = lax.broadcasted_iota(jnp.int32, (PEER_TOPK, ts), 0).astype(F32)
    bidx = lax.broadcasted_iota(jnp.int32, (SUBLANES, ts), 0).astype(F32)
    pad =jnp.full((_PEER_CAND_ROWS, ts), -jnp.inf, F32)

    def tables(h, s0, s1, exact):
        if exact:
            rank0, v0 = _extract_top(s0, kidx, PEER_TOPK, PEER_TOPK)
            rank1, v1 = _extract_top(s1, kidx, PEER_TOPK, PEER_TOPK)
        else:
            rank0, v0, n0 = _extract_by_value(s0, PEER_TOPK)
            rank1, v1, n1 = _extract_by_value(s1, PEER_TOPK)
        if exact:
            pairs = list(_PEER_CAND)
            cand = pad
            for ci, (a, b) in enumerate(pairs):
                cand = jnp.where(cidx == ci, v0[a] + v1[b], cand)
            crank, _ = _extract_top(cand, cidx, PEER_TOPK, PEER_TOPK)
            off = None
        else:
            col0 = jnp.zeros((PEER_TOPK, ts), F32)
            col1 = jnp.zeros((PEER_TOPK, ts), F32)
            low1 = jnp.zeros((SUBLANES, ts), F32)
            for r in range(PEER_TOPK):
                col0 = jnp.where(tidx == r, v0[r], col0)
                col1 = jnp.where(tidx == r, v1[r], col1)
                if r < SUBLANES:
                    low1 = jnp.where(bidx == r, v1[r], low1)
            ninf = np.float32(-np.inf)
            groups = [v0[0] + col1, jnp.where(tidx == 0, ninf, col0 + v1[0])]
            pairs = [(0, b) for b in range(PEER_TOPK)] + [(a, 0) if a else None for a in range(PEER_TOPK)]
            rest = [p for p in _PEER_CAND if p[0] and p[1]]
            for a in sorted({p[0] for p in rest}):
                mine = [p for p in rest if p[0] == a]
                if len(mine) > 2:
                    groups.append(jnp.where((bidx >= 1) & (bidx <= len(mine)), v0[a] + low1, ninf))
                    pairs += [(a, b) if 1 <= b <= len(mine) else None for b in range(SUBLANES)]
            loose = [p for p in rest if p not in pairs]
            assert len(loose) <= SUBLANES
            tail = jnp.full((SUBLANES, ts), ninf, F32)
            for r, (a, b) in enumerate(loose):
                tail = jnp.where(bidx == r, v0[a] + v1[b], tail)
            groups.append(tail)
            pairs += loose + [None] * (SUBLANES - len(loose))
            assert sorted(p for p in pairs if p) == sorted(_PEER_CAND)
            cand = jnp.concatenate(groups, axis=0)
            crank, _, nc = _extract_by_value(cand, PEER_TOPK)
            off = jnp.max(jnp.abs(n0 - PEER_TOPK) + jnp.abs(n1 - PEER_TOPK) + jnp.abs(nc - PEER_TOPK))
        picked = jnp.where(crank < PEER_TOPK, 1.0, 0.0)
        e = picked * jnp.exp(cand - (v0[0] + v1[0]))
        z = jnp.sum(e, axis=0, keepdims=True)
        lq = jnp.zeros((N_KEYS, ts), F32)
        for a in range(PEER_TOPK):
            cnt = jnp.zeros((1, ts), F32)
            for ci, p in enumerate(pairs):
                if p is not None and p[0] == a:
                    cnt = cnt + picked[ci:ci + 1]
            lq = jnp.where(rank0 == a, cnt, lq)
        rank_b = rank1.astype(BF16)
        qe_b = jnp.exp(s1 - v1[0]).astype(BF16)
        for k in range(N_KEYS // BF16_ROWS):
            rank_ref[h, k] = pltpu.bitcast(rank_b[k * BF16_ROWS:(k + 1) * BF16_ROWS], jnp.uint32)
            qe_ref[h, k] = pltpu.bitcast(qe_b[k * BF16_ROWS:(k + 1) * BF16_ROWS], jnp.uint32)
        lq_ref[h] = lq
        pe_ref[h] = jnp.exp(s0 - v0[0]) / z
        return off

    def head(h, carry):
        c0 = pl.multiple_of(h * 2 * PEER_DKH, 2 * PEER_DKH)
        q0 = q_ref[:, pl.ds(c0, PEER_DKH)].astype(BF16)
        q1 = q_ref[:, pl.ds(c0 + PEER_DKH, PEER_DKH)].astype(BF16)
        s0 = _dot_nt(keys_ref[h, 0], q0)
        s1 = _dot_nt(keys_ref[h, 1], q1)
        off = tables(h, s0, s1, exact=False)

        @pl.when(off > 0.5)
        def _():
            tables(h, s0, s1, exact=True)

        return carry

    lax.fori_loop(0, PEER_HEADS, head, 0)


def _peer_select(q, keys):
    n = q.shape[0]
    out = jax.ShapeDtypeStruct((PEER_HEADS, N_KEYS, n), F32)
    ospec = pl.BlockSpec((PEER_HEADS, N_KEYS, SEL_TILE), lambda i: (0, 0, i))
    packed = (PEER_HEADS, N_KEYS // BF16_ROWS, BF16_ROWS // 2)
    out_b = jax.ShapeDtypeStruct(packed + (n,), jnp.uint32)
    ospec_b = pl.BlockSpec(packed + (SEL_TILE,), lambda i: (0, 0, 0, i))
    return pl.pallas_call(
        _peer_select_body,
        grid=(n // SEL_TILE,),
        in_specs=[pl.BlockSpec((SEL_TILE, q.shape[1]), lambda i: (i, 0)), _full(keys.shape)],
        out_specs=[ospec_b, ospec_b, ospec, ospec],
        out_shape=[out_b, out_b, out, out],
        compiler_params=_params(("arbitrary",)),
        name="peer_select",
    )(q, keys)


PEER_PARTS = 4
PART_ROWS = PEER_ROWS // PEER_PARTS


def _peer_dense_body(ht_ref, u_ref, vt_ref, rank_ref, qe_ref, lq_ref, pe_ref, x_ref, *rest, prompt_tiles):
    *o_ref, acc_ref, a_ref, g_ref, w_ref = rest
    o_ref = o_ref[0] if prompt_tiles is None else o_ref
    _peer_dense_steps(ht_ref, u_ref, vt_ref, rank_ref, qe_ref, lq_ref, pe_ref, x_ref, o_ref,
                      acc_ref, a_ref, g_ref, w_ref, prompt_tiles)


def _peer_dense_steps(ht_ref, u_ref, vt_ref, rank_ref, qe_ref, lq_ref, pe_ref, x_ref, o_ref,
                      acc_ref, a_ref, g_ref, w_ref, prompt_tiles):
    step = pl.program_id(1)

    @pl.when(step == 0)
    def _():
        acc_ref[...] = jnp.zeros_like(acc_ref)

    part = PART_ROWS * N_KEYS
    sub = (N_KEYS // BF16_ROWS, BF16_ROWS, LANES)
    tiles = [(j, ci, k) for j in range(PART_ROWS) for ci in range(TOKEN_TILE // LANES) for k in range(sub[0])]

    def score_and_gates(p):
        slot = p % 2
        for j in range(PART_ROWS):
            for ci in range(TOKEN_TILE // LANES):
                cs = slice(ci * LANES, (ci + 1) * LANES)
                gate = jnp.zeros(sub, BF16)
                for h in range(PEER_HEADS):
                    lq = jnp.broadcast_to(lq_ref[h, p, j:j + 1, cs], sub[1:]).astype(BF16)
                    pe = jnp.broadcast_to(pe_ref[h, p, j:j + 1, cs], sub[1:]).astype(BF16)
                    take = pltpu.bitcast(rank_ref[h, :, :, cs], BF16) < lq[None]
                    qe = pltpu.bitcast(qe_ref[h, :, :, cs], BF16)
                    gate = gate + jnp.where(take, qe, jnp.zeros(sub, BF16)) * pe[None]
                for k in range(sub[0]):
                    r0 = (j * N_KEYS + k * BF16_ROWS) // 2
                    g_ref[slot, r0:r0 + BF16_ROWS // 2, cs] = pltpu.bitcast(gate[k], jnp.uint32)
        rows = pl.ds(pl.multiple_of(p * (part // 2), part // 2), part // 2)
        a_ref[slot] = _dot(pltpu.bitcast(u_ref[rows, :], BF16), pltpu.bitcast(ht_ref[...], BF16))

    def drain(p):
        slot = p % 2
        for j, ci, k in tiles:
            cs = slice(ci * LANES, (ci + 1) * LANES)
            r0 = j * N_KEYS + k * BF16_ROWS
            gate = pltpu.bitcast(g_ref[slot, r0 // 2:(r0 + BF16_ROWS) // 2, cs], BF16)
            w = gate * _gelu(a_ref[slot, r0:r0 + BF16_ROWS, cs]).astype(BF16)
            w_ref[p, r0 // 2:(r0 + BF16_ROWS) // 2, cs] = pltpu.bitcast(w, jnp.uint32)

    def body(p, carry):
        drain(p - 1)
        score_and_gates(p)
        return carry

    score_and_gates(0)
    lax.fori_loop(1, PEER_PARTS, body, 0)
    drain(PEER_PARTS - 1)
    w_all = w_ref[...].reshape(PEER_PARTS * part // 2, TOKEN_TILE)
    acc_ref[...] += _dot(pltpu.bitcast(vt_ref[...], BF16), pltpu.bitcast(w_all, BF16))

    last = step == pl.num_programs(1) - 1
    if prompt_tiles is None:
        @pl.when(last)
        def _():
            o_ref[...] = x_ref[...] + acc_ref[...].T
    else:
        tile = pl.program_id(0)
        o_prompt, o_sample = o_ref

        @pl.when(last & (tile < prompt_tiles))
        def _():
            o_prompt[...] = x_ref[...] + acc_ref[...].T

        @pl.when(last & (tile >= prompt_tiles))
        def _():
            o_sample[...] = x_ref[...] + acc_ref[...].T


def _peer_dense(ht, u, vt, rank, qe, lq, pe, x, n_prompt=None):
    n = x.shape[0]
    ex = PEER_ROWS * N_KEYS
    steps = N_KEYS // PEER_ROWS
    part = PART_ROWS * N_KEYS
    by_part = lambda t: t.reshape(PEER_HEADS, N_KEYS // PART_ROWS, PART_ROWS, n)
    tok = pl.BlockSpec((PEER_HEADS, N_KEYS // BF16_ROWS, BF16_ROWS // 2, TOKEN_TILE), lambda t, e: (0, 0, 0, t))
    rowsel = pl.BlockSpec((PEER_HEADS, PEER_PARTS, PART_ROWS, TOKEN_TILE), lambda t, e: (0, e, 0, t))
    if n_prompt is None:
        tp = None
        out_specs = pl.BlockSpec((TOKEN_TILE, D_MODEL), lambda t, e: (t, 0))
        out_shape = jax.ShapeDtypeStruct((n, D_MODEL), F32)
    else:
        tp = n_prompt // TOKEN_TILE
        out_specs = [pl.BlockSpec((TOKEN_TILE, D_MODEL), lambda t, e: (jnp.minimum(t, tp - 1), 0)),
                     pl.BlockSpec((TOKEN_TILE, D_MODEL), lambda t, e: (jnp.maximum(t - tp, 0), 0))]
        out_shape = [jax.ShapeDtypeStruct((n_prompt, D_MODEL), F32),
                     jax.ShapeDtypeStruct((n - n_prompt, D_MODEL), F32)]
    return pl.pallas_call(
        functools.partial(_peer_dense_body, prompt_tiles=tp),
        grid=(n // TOKEN_TILE, steps),
        in_specs=[pl.BlockSpec((D_MODEL // 2, TOKEN_TILE), lambda t, e: (0, t)),
                  pl.BlockSpec((ex // 2, D_MODEL), lambda t, e: (e, 0)),
                  pl.BlockSpec((D_MODEL // 2, ex), lambda t, e: (0, e)),
                  tok, tok, rowsel, rowsel,
                  pl.BlockSpec((TOKEN_TILE, D_MODEL), lambda t, e: (t, 0))],
        out_specs=out_specs,
        out_shape=out_shape,
        scratch_shapes=[pltpu.VMEM((D_MODEL, TOKEN_TILE), F32), pltpu.VMEM((2, part, TOKEN_TILE), F32),
                        pltpu.VMEM((2, part // 2, TOKEN_TILE), jnp.uint32),
                        pltpu.VMEM((PEER_PARTS, part // 2, TOKEN_TILE), jnp.uint32)],
        compiler_params=_params(("arbitrary", "arbitrary")),
        name="peer_dense",
    )(ht, u, vt, rank, qe, by_part(lq), by_part(pe), x)


def _peer(x1, ht, q, keys, u, vt, n_prompt=None):
    rank, qe, lq, pe = _peer_select(q, keys)
    return _peer_dense(ht, u, vt, rank, qe, lq, pe, x1, n_prompt)


def _pack_row_pairs(x):
    r2, c = x.shape
    return lax.bitcast_convert_type(x.astype(BF16).reshape(r2 // 2, 2, c).transpose(0, 2, 1), jnp.uint32)


def _pack_col_pairs_t(x):
    r, c2 = x.shape
    return lax.bitcast_convert_type(x.astype(BF16).reshape(r, c2 // 2, 2), jnp.uint32).T


def _block_diag_mean(width):
    idx = np.arange(width) // HEAD_DIM
    return jnp.asarray((idx[:, None] == idx[None, :]).astype(np.float32) / HEAD_DIM)


def kernel(x_prompt, x_sample, cache_swa_k, cache_swa_v, state_hgrn, state_conv, norm_mix_g, norm_ffn_g, w_in0, attn_q_norm_g, attn_k_norm_g, attn_sinks, hgrn_lb, hgrn_o_norm_g, w_out0, conv_w_pw1, conv_b_pw1, conv_w_dw, conv_b_dw, conv_ln_g, conv_ln_b, conv_w_pw2, peer_w_q, peer_keys, peer_u, peer_v):
    batch, seq, _ = x_prompt.shape
    dec_batch, dec_seq, _ = x_sample.shape
    n_p = batch * seq
    n_s = dec_batch * dec_seq
    assert seq % CONV_ROWS == 0 and seq % HGRN_ROWS == 0 and n_p % TOKEN_TILE == 0 and n_s % TOKEN_TILE == 0
    assert dec_batch % DEC_GROUP == 0 and 8 % dec_seq == 0 and dec_seq <= CONV_W - 1

    x = (x_prompt.reshape(n_p, D_MODEL), x_sample.reshape(n_s, D_MODEL))
    row = lambda v: v.reshape(1, -1).astype(F32)

    z = _in0(x, row(norm_mix_g[0]), w_in0.astype(BF16))
    d = np.arange(LANES) % HEAD_DIM
    inv = ROPE_THETA ** (-jnp.arange(0, ROT_DIM, 2, dtype=F32) / ROT_DIM)
    inv_lane = jnp.where(jnp.asarray(d < ROT_DIM), inv[d % (ROT_DIM // 2)], 0.0).reshape(1, LANES)
    gq = row(jnp.tile(attn_q_norm_g, N_HEADS_A))
    gk = row(jnp.tile(attn_k_norm_g, N_KV_A))
    segq, segk = _block_diag_mean(QA), _block_diag_mean(KVA)
    sinks = attn_sinks.astype(F32)
    attn_p, k_win_p, v_win_p = _attn_prompt(z, batch, seq, gq, gk, inv_lane, segq, segk, sinks)
    kc = cache_swa_k.reshape(dec_batch * WINDOW, KVA)
    vc = cache_swa_v.reshape(dec_batch * WINDOW, KVA)
    attn_s, k_new, v_new = _attn_decode(z, n_p, dec_batch, dec_seq, kc, vc, gq, gk, inv_lane, segq, segk, sinks)

    go = row(hgrn_o_norm_g)
    tril = jnp.asarray(np.tril(np.ones((HGRN_CHUNK, HGRN_CHUNK), np.float32)), BF16)
    hg_p, s_p = _hgrn_prompt(z, batch, seq, hgrn_lb, go, tril)
    r = np.arange(LANES)
    tril_dec = jnp.asarray(((r[:, None] // dec_seq == r[None, :] // dec_seq) & (r[None, :] <= r[:, None]))
                           .astype(np.float32))
    hg_s, s_s = _hgrn_decode(z, n_p, dec_batch, dec_seq, state_hgrn, hgrn_lb, go, tril_dec)

    x1, ht, q = _mix_out(_out0_body, [(attn_p, attn_s), (hg_p, hg_s)], x, [w_out0.astype(BF16)], row(norm_ffn_g[0]),
                         peer_w_q[0].astype(BF16), "mix0_out")
    x2 = _peer(x1, ht, q, peer_keys[0].astype(BF16), _pack_row_pairs(peer_u[0]), _pack_col_pairs_t(peer_v[0]))

    u = _glu(x2, row(norm_mix_g[1]), conv_w_pw1.astype(BF16), row(conv_b_pw1))
    wdw = jnp.concatenate([conv_w_dw, jnp.zeros((1, D_MODEL), F32)], axis=0)
    bdw = row(conv_b_dw)
    c_p = _conv_prompt(u, batch, seq, wdw, bdw)
    wshift = jnp.stack([jnp.concatenate([jnp.zeros((t, D_MODEL), F32), conv_w_dw[:CONV_W - 1 - t]], axis=0)
                        for t in range(dec_seq)])
    c_s = _conv_decode(u, n_p, dec_batch, dec_seq, state_conv, wdw, wshift, bdw)
    x3, ht, q = _mix_out(_post1_body, [(c_p, c_s)], x2, [row(conv_ln_g), row(conv_ln_b), conv_w_pw2.astype(BF16)],
                         row(norm_ffn_g[1]), peer_w_q[1].astype(BF16), "mix1_out")
    y_p, y_s = _peer(x3, ht, q, peer_keys[1].astype(BF16), _pack_row_pairs(peer_u[1]),
                     _pack_col_pairs_t(peer_v[1]), n_p)

    kv = lambda t: t.reshape(t.shape[0], WINDOW, N_KV_A, HEAD_DIM)
    k_win_s = jnp.concatenate([cache_swa_k[:, dec_seq:], k_new.reshape(dec_batch, dec_seq, N_KV_A, HEAD_DIM)], axis=1)
    v_win_s = jnp.concatenate([cache_swa_v[:, dec_seq:], v_new.reshape(dec_batch, dec_seq, N_KV_A, HEAD_DIM)], axis=1)
    u_p = u[:n_p].reshape(batch, seq, D_MODEL)
    u_s = u[n_p:].reshape(dec_batch, dec_seq, D_MODEL)
    conv_buf_p = u_p[:, seq - (CONV_W - 1):]
    conv_buf_s = jnp.concatenate([state_conv[:, dec_seq:], u_s], axis=1)
    return (y_p.reshape(batch, seq, D_MODEL), y_s.reshape(dec_batch, dec_seq, D_MODEL),
            kv(k_win_p), kv(v_win_p), s_p, conv_buf_p, k_win_s, v_win_s, s_s, conv_buf_s)
```

```python
import functools

import numpy as np
import jax
import jax.numpy as jnp
from jax import lax
from jax.experimental import pallas as pl
from jax.experimental.pallas import tpu as pltpu

F32 = jnp.float32
BF16 = jnp.bfloat16
HI = lax.Precision.HIGHEST

D_MODEL = 1024
PAST_LEN = 8192
HEAD_DIM = 64
N_HEADS_A = 8
N_KV_A = 2
KV_REP = N_HEADS_A // N_KV_A
WINDOW = 128
ROT_DIM = HEAD_DIM // 4
ROPE_THETA = 500000.0
ATTN_SCALE = HEAD_DIM ** -0.5
NEG_INF = -1e30
N_HEADS_B = 4
DK_B = 128
DV_B = 128
CONV_W = 31
N_KEYS = 128
PEER_HEADS = 8
PEER_TOPK = 16
PEER_DKH = 128
NORM_EPS = 1e-6

QA = N_HEADS_A * HEAD_DIM
KVA = N_KV_A * HEAD_DIM
QB = N_HEADS_B * DK_B
VB = N_HEADS_B * DV_B
IN0_WIDTH = QA + 2 * KVA + 2 * QB + 2 * VB
COL_K = QA // 128
COL_V = (QA + KVA) // 128
COL_QB = (QA + 2 * KVA) // 128
COL_FB = COL_QB + QB // 128
COL_IB = COL_FB + QB // 128
COL_GB = COL_IB + VB // 128

LANES = 128
SUBLANES = 8
TOKEN_TILE = 512
SEL_TILE = 256
HGRN_CHUNK = 64
HGRN_SUB = 16
HGRN_ROWS = 256
HGRN_HEADS_PER_STEP = 2
CONV_ROWS = 512
CONV_HALO = 32
CONV_CHUNK = 128
PEER_ROWS = 16
BF16_ROWS = 16
DEC_GROUP = 8
VMEM_LIMIT = 48 * 1024 * 1024


def _dot(a, b, prec=None):
    return jnp.dot(a, b, preferred_element_type=F32, precision=prec)


def _dot_nt(a, b, prec=None):
    return lax.dot_general(a, b, (((1,), (1,)), ((), ())), preferred_element_type=F32, precision=prec)


def _rms(x, g):
    return x * lax.rsqrt(jnp.mean(x * x, axis=-1, keepdims=True) + NORM_EPS) * g


def _silu(x):
    return x * jax.nn.sigmoid(x)


def _gelu(x):
    return 0.5 * x * (1.0 + lax.erf(x * np.float32(0.7071067811865476)))


def _params(sem, flags=None):
    return pltpu.CompilerParams(dimension_semantics=sem, vmem_limit_bytes=VMEM_LIMIT, flags=flags)


def _full(shape):
    n = len(shape)
    return pl.BlockSpec(shape, lambda *_: (0,) * n)


def _row_inputs(arrs):
    specs, ops, layout = [], [], []
    for a in arrs:
        if isinstance(a, tuple):
            tp = a[0].shape[0] // TOKEN_TILE
            width = a[0].shape[1]
            specs += [pl.BlockSpec((TOKEN_TILE, width), lambda i, tp=tp: (jnp.minimum(i, tp - 1), 0)),
                      pl.BlockSpec((TOKEN_TILE, width), lambda i, tp=tp: (jnp.maximum(i - tp, 0), 0))]
            ops += list(a)
            layout.append(tp)
        else:
            specs.append(pl.BlockSpec((TOKEN_TILE, a.shape[1]), lambda i: (i, 0)))
            ops.append(a)
            layout.append(None)
    return specs, ops, tuple(layout)


def _row_values(layout, refs):
    i = pl.program_id(0)
    vals, k = [], 0
    for tp in layout:
        if tp is None:
            vals.append(refs[k][...])
            k += 1
        else:
            vals.append(jnp.where(i < tp, refs[k][...], refs[k + 1][...]))
            k += 2
    return vals, refs[k:]


def _rows_of(a):
    return a[0].shape[0] + a[1].shape[0] if isinstance(a, tuple) else a.shape[0]


def _in0_body(*refs, layout):
    (x,), (g_ref, w_ref, z_ref) = _row_values(layout, refs)
    h = _rms(x, g_ref[...]).astype(BF16)
    z_ref[...] = _dot(h, w_ref[...])


def _in0(x, g, w):
    n = _rows_of(x)
    width = w.shape[1]
    specs, ops, layout = _row_inputs([x])
    return pl.pallas_call(
        functools.partial(_in0_body, layout=layout),
        grid=(n // TOKEN_TILE,),
        in_specs=specs + [_full((1, D_MODEL)), _full((D_MODEL, width))],
        out_specs=pl.BlockSpec((TOKEN_TILE, width), lambda i: (i, 0)),
        out_shape=jax.ShapeDtypeStruct((n, width), F32),
        compiler_params=_params(("arbitrary",)),
        name="in0_proj",
    )(*ops, g, w)


def _glu_body(x_ref, g_ref, w_ref, b_ref, u_ref):
    h = _rms(x_ref[...], g_ref[...]).astype(BF16)
    a = _dot(h, w_ref[...]) + b_ref[...]
    u_ref[...] = a[:, :D_MODEL] * jax.nn.sigmoid(a[:, D_MODEL:])


def _glu(x, g, w, b):
    n = x.shape[0]
    return pl.pallas_call(
        _glu_body,
        grid=(n // TOKEN_TILE,),
        in_specs=[pl.BlockSpec((TOKEN_TILE, D_MODEL), lambda i: (i, 0)), _full((1, D_MODEL)),
                  _full((D_MODEL, 2 * D_MODEL)), _full((1, 2 * D_MODEL))],
        out_specs=pl.BlockSpec((TOKEN_TILE, D_MODEL), lambda i: (i, 0)),
        out_shape=jax.ShapeDtypeStruct((n, D_MODEL), F32),
        compiler_params=_params(("arbitrary",)),
        name="conv_glu",
    )(x, g, w, b)


def _ffn_query(x1, gf_ref, wq_ref, x1_ref, ht_ref, q_ref):
    x1_ref[...] = x1
    h2 = _rms(x1, gf_ref[...])
    ht_ref[...] = h2.T.astype(BF16)
    q_ref[...] = _dot(h2.astype(BF16), wq_ref[...])


def _out0_body(*refs, layout):
    (attn, hg, x), (w_ref, gf_ref, wq_ref, x1_ref, ht_ref, q_ref) = _row_values(layout, refs)
    m = _dot(attn.astype(BF16), w_ref[:QA, :]) + _dot(hg.astype(BF16), w_ref[QA:, :])
    _ffn_query(x + m, gf_ref, wq_ref, x1_ref, ht_ref, q_ref)


def _post1_body(*refs, layout):
    (c, x), (lg_ref, lb_ref, w_ref, gf_ref, wq_ref, x1_ref, ht_ref, q_ref) = _row_values(layout, refs)
    mu = jnp.mean(c, axis=-1, keepdims=True)
    d = c - mu
    var = jnp.mean(d * d, axis=-1, keepdims=True)
    ln = d * lax.rsqrt(var + NORM_EPS) * lg_ref[...] + lb_ref[...]
    y = _dot(_silu(ln).astype(BF16), w_ref[...])
    _ffn_query(x + y, gf_ref, wq_ref, x1_ref, ht_ref, q_ref)


def _mix_out(body, acts, x, consts, gf, wq, name):
    n = _rows_of(x)
    qw = wq.shape[1]
    row = lambda w: pl.BlockSpec((TOKEN_TILE, w), lambda i: (i, 0))
    specs, ops, layout = _row_inputs(list(acts) + [x])
    return pl.pallas_call(
        functools.partial(body, layout=layout),
        grid=(n // TOKEN_TILE,),
        in_specs=specs + [_full(c.shape) for c in consts] + [_full(gf.shape), _full(wq.shape)],
        out_specs=[row(D_MODEL), pl.BlockSpec((D_MODEL, TOKEN_TILE), lambda i: (0, i)), row(qw)],
        out_shape=[jax.ShapeDtypeStruct((n, D_MODEL), F32), jax.ShapeDtypeStruct((D_MODEL, n), BF16),
                   jax.ShapeDtypeStruct((n, qw), F32)],
        compiler_params=_params(("arbitrary",)),
        name=name,
    )(*ops, *consts, gf, wq)


def _head_norm(x, g, seg):
    ms = _dot(x * x, seg, HI)
    return x * lax.rsqrt(ms + NORM_EPS) * g


def _rope(x, cos, sin, first_half):
    half = ROT_DIM // 2
    width = x.shape[1]
    up = pltpu.roll(x, width - half, axis=1)
    dn = pltpu.roll(x, half, axis=1)
    return x * cos + jnp.where(first_half, -up, dn) * sin


def _rope_tables(pos, inv_ref, reps):
    ang = pos * inv_ref[...]
    c, s = jnp.cos(ang), jnp.sin(ang)
    if reps > 1:
        c, s = (jnp.concatenate([t] * reps, axis=1) for t in (c, s))
    lane = lax.broadcasted_iota(jnp.int32, c.shape, 1)
    return c, s, (lane % HEAD_DIM) < (ROT_DIM // 2)


def _stack_heads(x, g):
    return jnp.concatenate(
        [x[:, (g * KV_REP + r) * HEAD_DIM:(g * KV_REP + r + 1) * HEAD_DIM] for r in range(KV_REP)], axis=0)


def _attn_prompt_body(q_ref, k_ref, v_ref, gq_ref, gk_ref, inv_ref, segq_ref, segk_ref, sink_ref,
                      o_ref, kw_ref, vw_ref, kprev_ref, vprev_ref):
    j = pl.program_id(1)
    w = WINDOW

    @pl.when(j == 0)
    def _():
        kprev_ref[...] = jnp.zeros_like(kprev_ref)
        vprev_ref[...] = jnp.zeros_like(vprev_ref)

    rows = lax.broadcasted_iota(jnp.int32, (w, LANES), 0)
    pos = (j * w + rows).astype(F32)
    ck, sk, fk = _rope_tables(pos, inv_ref, 1)
    cq, sq, fq = _rope_tables(pos, inv_ref, QA // LANES)
    q = _rope(_head_norm(q_ref[...], gq_ref[...], segq_ref[...]), cq, sq, fq)
    k = _rope(_head_norm(k_ref[...], gk_ref[...], segk_ref[...]), ck, sk, fk)
    v = v_ref[...]
    kp = kprev_ref[...]
    vp = vprev_ref[...]

    qi = lax.broadcasted_iota(jnp.int32, (KV_REP * w, w), 0) % w
    ki = lax.broadcasted_iota(jnp.int32, (KV_REP * w, w), 1)
    m_own = ki <= qi
    m_prev = (ki > qi) & (j > 0)
    rep = lax.broadcasted_iota(jnp.int32, (KV_REP * w, 1), 0) // w
    for g in range(N_KV_A):
        sl = slice(g * HEAD_DIM, (g + 1) * HEAD_DIM)
        qg = _stack_heads(q, g).astype(BF16)
        s_own = jnp.where(m_own, _dot_nt(qg, k[:, sl].astype(BF16)) * ATTN_SCALE, NEG_INF)
        s_prev = jnp.where(m_prev, _dot_nt(qg, kp[:, sl].astype(BF16)) * ATTN_SCALE, NEG_INF)
        sink = jnp.zeros((KV_REP * w, 1), F32)
        for r in range(KV_REP):
            sink = jnp.where(rep == r, sink_ref[g * KV_REP + r], sink)
        mx = jnp.maximum(jnp.maximum(jnp.max(s_own, axis=-1, keepdims=True),
                                     jnp.max(s_prev, axis=-1, keepdims=True)), sink)
        e_own = jnp.exp(s_own - mx)
        e_prev = jnp.exp(s_prev - mx)
        den = (jnp.sum(e_own, axis=-1, keepdims=True) + jnp.sum(e_prev, axis=-1, keepdims=True)
               + jnp.exp(sink - mx))
        o = (_dot(e_own.astype(BF16), v[:, sl].astype(BF16))
             + _dot(e_prev.astype(BF16), vp[:, sl].astype(BF16))) / den
        for r in range(KV_REP):
            hq = g * KV_REP + r
            o_ref[:, hq * HEAD_DIM:(hq + 1) * HEAD_DIM] = o[r * w:(r + 1) * w]

    kprev_ref[...] = k
    vprev_ref[...] = v

    @pl.when(j == pl.num_programs(1) - 1)
    def _():
        kw_ref[0] = k
        vw_ref[0] = v


def _attn_prompt(z, batch, seq, gq, gk, inv, segq, segk, sinks):
    nb = seq // WINDOW
    blk = lambda width, col: pl.BlockSpec((WINDOW, width), lambda b, j: (b * nb + j, col))
    return pl.pallas_call(
        _attn_prompt_body,
        grid=(batch, nb),
        in_specs=[blk(QA, 0), blk(KVA, COL_K), blk(KVA, COL_V), _full(gq.shape), _full(gk.shape),
                  _full(inv.shape), _full(segq.shape), _full(segk.shape),
                  pl.BlockSpec(memory_space=pltpu.SMEM)],
        out_specs=[pl.BlockSpec((WINDOW, QA), lambda b, j: (b * nb + j, 0)),
                   pl.BlockSpec((1, WINDOW, KVA), lambda b, j: (b, 0, 0)),
                   pl.BlockSpec((1, WINDOW, KVA), lambda b, j: (b, 0, 0))],
        out_shape=[jax.ShapeDtypeStruct((batch * seq, QA), F32),
                   jax.ShapeDtypeStruct((batch, WINDOW, KVA), F32),
                   jax.ShapeDtypeStruct((batch, WINDOW, KVA), F32)],
        scratch_shapes=[pltpu.VMEM((WINDOW, KVA), F32), pltpu.VMEM((WINDOW, KVA), F32)],
        compiler_params=_params(("arbitrary", "arbitrary")),
        name="swa_prompt",
    )(z, z, z, gq, gk, inv, segq, segk, sinks)


def _attn_decode_body(q_ref, k_ref, v_ref, kc_ref, vc_ref, gq_ref, gk_ref, inv_ref, segq_ref, segk_ref,
                      sink_ref, o_ref, kn_ref, vn_ref, *, dec_seq):
    rows_n = DEC_GROUP * dec_seq
    rows = lax.broadcasted_iota(jnp.int32, (rows_n, LANES), 0)
    pos = (PAST_LEN + rows % dec_seq).astype(F32)
    ck, sk, fk = _rope_tables(pos, inv_ref, 1)
    cq, sq, fq = _rope_tables(pos, inv_ref, QA // LANES)
    q = _rope(_head_norm(q_ref[...], gq_ref[...], segq_ref[...]), cq, sq, fq)
    k = _rope(_head_norm(k_ref[...], gk_ref[...], segk_ref[...]), ck, sk, fk)
    v = v_ref[...]
    kn_ref[...] = k
    vn_ref[...] = v

    nq = KV_REP * rows_n
    nc = DEC_GROUP * WINDOW

    def qrow(shape):
        r = lax.broadcasted_iota(jnp.int32, shape, 0) % rows_n
        return r // dec_seq, r % dec_seq

    bq, tq = qrow((nq, nc))
    col = lax.broadcasted_iota(jnp.int32, (nq, nc), 1)
    m_cache = (col // WINDOW == bq) & (col % WINDOW > tq)
    bq, tq = qrow((nq, rows_n))
    col = lax.broadcasted_iota(jnp.int32, (nq, rows_n), 1)
    m_new = (col // dec_seq == bq) & (col % dec_seq <= tq)
    rep = lax.broadcasted_iota(jnp.int32, (nq, 1), 0) // rows_n
    for g in range(N_KV_A):
        sl = slice(g * HEAD_DIM, (g + 1) * HEAD_DIM)
        qg = _stack_heads(q, g).astype(BF16)
        s_c = jnp.where(m_cache, _dot_nt(qg, kc_ref[:, sl].astype(BF16)) * ATTN_SCALE, NEG_INF)
        s_n = jnp.where(m_new, _dot_nt(qg, k[:, sl].astype(BF16)) * ATTN_SCALE, NEG_INF)
        sink = jnp.zeros((nq, 1), F32)
        for r in range(KV_REP):
            sink = jnp.where(rep == r, sink_ref[g * KV_REP + r], sink)
        mx = jnp.maximum(jnp.maximum(jnp.max(s_c, axis=-1, keepdims=True),
                                     jnp.max(s_n, axis=-1, keepdims=True)), sink)
        e_c = jnp.exp(s_c - mx)
        e_n = jnp.exp(s_n - mx)
        den = jnp.sum(e_c, axis=-1, keepdims=True) + jnp.sum(e_n, axis=-1, keepdims=True) + jnp.exp(sink - mx)
        o = (_dot(e_c.astype(BF16), vc_ref[:, sl].astype(BF16))
             + _dot(e_n.astype(BF16), v[:, sl].astype(BF16))) / den
        for r in range(KV_REP):
            hq = g * KV_REP + r
            o_ref[:, hq * HEAD_DIM:(hq + 1) * HEAD_DIM] = o[r * rows_n:(r + 1) * rows_n]


def _attn_decode(z, row0, dec_batch, dec_seq, kc, vc, gq, gk, inv, segq, segk, sinks):
    rows_n = DEC_GROUP * dec_seq
    r0 = row0 // rows_n
    blk = lambda width, col: pl.BlockSpec((rows_n, width), lambda i: (r0 + i, col))
    cache = pl.BlockSpec((DEC_GROUP * WINDOW, KVA), lambda i: (i, 0))
    n = dec_batch * dec_seq
    return pl.pallas_call(
        functools.partial(_attn_decode_body, dec_seq=dec_seq),
        grid=(dec_batch // DEC_GROUP,),
        in_specs=[blk(QA, 0), blk(KVA, COL_K), blk(KVA, COL_V), cache, cache, _full(gq.shape),
                  _full(gk.shape), _full(inv.shape), _full(segq.shape), _full(segk.shape),
                  pl.BlockSpec(memory_space=pltpu.SMEM)],
        out_specs=[pl.BlockSpec((rows_n, QA), lambda i: (i, 0)), pl.BlockSpec((rows_n, KVA), lambda i: (i, 0)),
                   pl.BlockSpec((rows_n, KVA), lambda i: (i, 0))],
        out_shape=[jax.ShapeDtypeStruct((n, QA), F32), jax.ShapeDtypeStruct((n, KVA), F32),
                   jax.ShapeDtypeStruct((n, KVA), F32)],
        compiler_params=_params(("arbitrary",)),
        name="swa_decode",
    )(z, z, z, kc, vc, gq, gk, inv, segq, segk, sinks)


def _hgrn_lower_bound(lb_ref):
    l = lb_ref[...]
    e = jnp.exp(l - jnp.max(l, axis=0, keepdims=True))
    return e[0:1] / jnp.sum(e, axis=0, keepdims=True)


def _hgrn_gates(qb, fb, lb):
    logf = jnp.log(lb + (1.0 - lb) * jax.nn.sigmoid(fb))
    kb = (1.0 - lb) * jax.nn.sigmoid(-fb)
    return _silu(qb), kb, logf


def _hgrn_out(o, g, gate):
    return _rms(o, g) * _silu(gate)


def _hgrn_chunk(qh, kb, ih, logf, st, tril):
    c = qh.shape[0]
    hi = logf.astype(BF16)
    rest = logf - hi.astype(F32)
    mid = rest.astype(BF16)
    b = _dot(tril, hi) + _dot(tril, mid) + _dot(tril, (rest - mid.astype(F32)).astype(BF16))
    ones = jnp.ones((DK_B, LANES), BF16)
    st_b = st.astype(BF16)
    ih_b = ih.astype(BF16)
    o_parts = []
    row = lax.broadcasted_iota(jnp.int32, (c, 1), 0)
    trow = lax.broadcasted_iota(jnp.int32, (HGRN_SUB, 1), 0)
    for blk in range(c // HGRN_SUB):
        t0 = blk * HGRN_SUB
        bi = b[t0:t0 + HGRN_SUB]
        qi = qh[t0:t0 + HGRN_SUB]
        if blk == 0:
            oi = _dot_nt((qi * jnp.exp(bi)).astype(BF16), st_b)
        else:
            base = b[t0 - 1:t0]
            qd = qi * jnp.exp(bi - base)
            kd = jnp.where(row < t0, kb * jnp.exp(jnp.minimum(base - b, 0.0)), 0.0)
            a_off = _dot_nt(qd.astype(BF16), kd.astype(BF16))
            oi = _dot_nt((qd * jnp.exp(base)).astype(BF16), st_b) + _dot(a_off.astype(BF16), ih_b)
        prods = []
        for s in range(HGRN_SUB):
            e = jnp.exp(jnp.where(trow >= s, bi - bi[s:s + 1], 0.0))
            prods.append(qi * e * kb[t0 + s:t0 + s + 1])
        a_diag = _dot(jnp.concatenate(prods, axis=0).astype(BF16), ones)
        for s in range(HGRN_SUB):
            a = a_diag[s * HGRN_SUB:(s + 1) * HGRN_SUB]
            oi = oi + jnp.where(trow >= s, a, 0.0) * ih[t0 + s:t0 + s + 1]
        o_parts.append(oi)
    last = b[c - 1:c]
    st_new = st * jnp.exp(last) + _dot(ih.T.astype(BF16), (kb * jnp.exp(last - b)).astype(BF16))
    return jnp.concatenate(o_parts, axis=0), st_new


def _hgrn_prompt_body(qb_ref, fb_ref, ib_ref, gb_ref, lb_ref, go_ref, tril_ref, o_ref, s_ref, st_ref):
    tb = pl.program_id(2)

    @pl.when(tb == 0)
    def _():
        st_ref[...] = jnp.zeros_like(st_ref)

    tril = tril_ref[...]
    heads = [slice(h * DK_B, (h + 1) * DK_B) for h in range(HGRN_HEADS_PER_STEP)]
    lbs = [_hgrn_lower_bound(lb_ref.at[:, hs]) for hs in heads]
    sts = [st_ref[h] for h in range(HGRN_HEADS_PER_STEP)]
    for c in range(HGRN_ROWS // HGRN_CHUNK):
        rs = slice(c * HGRN_CHUNK, (c + 1) * HGRN_CHUNK)
        for h, hs in enumerate(heads):
            qh, kb, logf = _hgrn_gates(qb_ref[rs, hs], fb_ref[rs, hs], lbs[h])
            o, sts[h] = _hgrn_chunk(qh, kb, ib_ref[rs, hs], logf, sts[h], tril)
            o_ref[rs, hs] = _hgrn_out(o, go_ref[...], gb_ref[rs, hs])
    for h in range(HGRN_HEADS_PER_STEP):
        st_ref[h] = sts[h]

    @pl.when(tb == pl.num_programs(2) - 1)
    def _():
        for h in range(HGRN_HEADS_PER_STEP):
            s_ref[0, h] = sts[h].T


def _hgrn_prompt(z, batch, seq, lb, go, tril):
    nt = seq // HGRN_ROWS
    hp = HGRN_HEADS_PER_STEP
    assert N_HEADS_B % hp == 0 and all(c % hp == 0 for c in (COL_QB, COL_FB, COL_IB, COL_GB))
    blk = lambda col: pl.BlockSpec((HGRN_ROWS, hp * DK_B), lambda b, h, t: (b * nt + t, col // hp + h))
    return pl.pallas_call(
        _hgrn_prompt_body,
        grid=(batch, N_HEADS_B // hp, nt),
        in_specs=[blk(COL_QB), blk(COL_FB), blk(COL_IB), blk(COL_GB),
                  pl.BlockSpec((lb.shape[0], hp * DK_B), lambda b, h, t: (0, h)), _full(go.shape), _full(tril.shape)],
        out_specs=[pl.BlockSpec((HGRN_ROWS, hp * DV_B), lambda b, h, t: (b * nt + t, h)),
                   pl.BlockSpec((1, hp, DK_B, DV_B), lambda b, h, t: (b, h, 0, 0))],
        out_shape=[jax.ShapeDtypeStruct((batch * seq, VB), F32),
                   jax.ShapeDtypeStruct((batch, N_HEADS_B, DK_B, DV_B), F32)],
        scratch_shapes=[pltpu.VMEM((hp, DV_B, DK_B), F32)],
        compiler_params=_params(("arbitrary", "arbitrary", "arbitrary")),
        name="hgrn_prompt",
    )(z, z, z, z, lb, go, tril)


def _pad_rows(x, rows):
    return jnp.concatenate([x, jnp.zeros((rows - x.shape[0], x.shape[1]), x.dtype)], axis=0)


def _hgrn_decode_body(z_ref, s0_ref, lb_ref, go_ref, tril_ref, o_ref, s_ref, *, dec_seq):
    rows_n = z_ref.shape[0]
    groups = rows_n // dec_seq
    row = lax.broadcasted_iota(jnp.int32, (rows_n, 1), 0)
    for h in range(N_HEADS_B):
        col = lambda c0: slice((c0 + h) * LANES, (c0 + h + 1) * LANES)
        hs = slice(h * DK_B, (h + 1) * DK_B)
        lb = _hgrn_lower_bound(lb_ref.at[:, hs])
        qh, kb, logf = _hgrn_gates(z_ref[:, col(COL_QB)], z_ref[:, col(COL_FB)], lb)
        ih = z_ref[:, col(COL_IB)]
        b = _dot(tril_ref[...], _pad_rows(logf, LANES), HI)[:rows_n]
        b_t = _pad_rows(b, LANES).T
        ih_pad = _pad_rows(ih, LANES)
        o = jnp.zeros((rows_n, DV_B), F32)
        for e in range(groups):
            mine = (row // dec_seq) == e
            s0 = s0_ref[e, h]
            o = o + _dot(jnp.where(mine, qh * jnp.exp(b), 0.0), s0, HI)
            for s in range(dec_seq):
                r = e * dec_seq + s
                live = mine & (row >= r)
                ex = jnp.exp(jnp.where(live, b - b[r:r + 1], 0.0))
                a = jnp.sum(qh * ex * kb[r:r + 1], axis=-1, keepdims=True)
                o = o + jnp.where(live, a, 0.0) * ih[r:r + 1]
            r_last = (e + 1) * dec_seq - 1
            last = b[r_last:r_last + 1]
            kd = jnp.where(mine, kb * jnp.exp(jnp.minimum(last - b, 0.0)), 0.0)
            s_ref[e, h] = s0 * jnp.exp(b_t[:, r_last:r_last + 1]) + _dot(_pad_rows(kd, LANES).T, ih_pad, HI)
        o_ref[:, hs] = _hgrn_out(o, go_ref[...], z_ref[:, col(COL_GB)])


def _hgrn_decode(z, row0, dec_batch, dec_seq, s0, lb, go, tril):
    groups = 8 // dec_seq
    rows_n = groups * dec_seq
    r0 = row0 // rows_n
    st = pl.BlockSpec((groups, N_HEADS_B, DK_B, DV_B), lambda i: (i, 0, 0, 0))
    return pl.pallas_call(
        functools.partial(_hgrn_decode_body, dec_seq=dec_seq),
        grid=(dec_batch // groups,),
        in_specs=[pl.BlockSpec((rows_n, z.shape[1]), lambda i: (r0 + i, 0)), st,
                  _full(lb.shape), _full(go.shape), _full(tril.shape)],
        out_specs=[pl.BlockSpec((rows_n, VB), lambda i: (i, 0)), st],
        out_shape=[jax.ShapeDtypeStruct((dec_batch * dec_seq, VB), F32),
                   jax.ShapeDtypeStruct(s0.shape, F32)],
        compiler_params=_params(("arbitrary",)),
        name="hgrn_decode",
    )(z, s0, lb, go, tril)


def _conv_prompt_body(cur_ref, prev_ref, w_ref, b_ref, c_ref, ext_ref, sh_ref):
    t = pl.program_id(1)
    ext_ref[:CONV_HALO, :] = jnp.where(t > 0, prev_ref[...], 0.0)
    ext_ref[CONV_HALO:, :] = cur_ref[...]
    lead = CONV_HALO - (CONV_W - 1)
    chunk = CONV_CHUNK

    def cols(ci, carry):
        cs = pl.ds(pl.multiple_of(ci * LANES, LANES), LANES)
        for r in range(CONV_ROWS // chunk):
            acc = jnp.zeros((chunk, LANES), F32) + b_ref[:, cs]
            for res in range(SUBLANES):
                taps = range(res, CONV_W, SUBLANES)
                span = chunk + taps[-1] - res
                sh_ref[:span, :] = ext_ref[pl.ds(r * chunk + lead + res, span), cs]
                for w in taps:
                    acc = acc + sh_ref[w - res:w - res + chunk, :] * w_ref[pl.ds(w, 1), cs]
            c_ref[pl.ds(r * chunk, chunk), cs] = acc
        return carry

    lax.fori_loop(0, D_MODEL // LANES, cols, 0)


def _conv_prompt(u, batch, seq, w, b):
    nt = seq // CONV_ROWS
    per = CONV_ROWS // CONV_HALO
    return pl.pallas_call(
        _conv_prompt_body,
        grid=(batch, nt),
        in_specs=[pl.BlockSpec((CONV_ROWS, D_MODEL), lambda bi, t: (bi * nt + t, 0)),
                  pl.BlockSpec((CONV_HALO, D_MODEL), lambda bi, t: (jnp.maximum((bi * nt + t) * per - 1, 0), 0)),
                  _full(w.shape), _full(b.shape)],
        out_specs=pl.BlockSpec((CONV_ROWS, D_MODEL), lambda bi, t: (bi * nt + t, 0)),
        out_shape=jax.ShapeDtypeStruct((batch * seq, D_MODEL), F32),
        scratch_shapes=[pltpu.VMEM((CONV_HALO + CONV_ROWS, D_MODEL), F32),
                        pltpu.VMEM((CONV_CHUNK + CONV_HALO, LANES), F32)],
        compiler_params=_params(("arbitrary", "arbitrary")),
        name="conv_prompt",
    )(u, u, w, b)


def _conv_decode_body(u_ref, st_ref, w_ref, wshift_ref, b_ref, c_ref, *, dec_seq):
    for e in range(DEC_GROUP):
        past = st_ref[e]
        for t in range(dec_seq):
            acc = jnp.sum(past * wshift_ref[t], axis=0, keepdims=True) + b_ref[...]
            for t2 in range(t + 1):
                wi = CONV_W - 1 - t + t2
                acc = acc + u_ref[e * dec_seq + t2:e * dec_seq + t2 + 1, :] * w_ref[wi:wi + 1, :]
            c_ref[e * dec_seq + t:e * dec_seq + t + 1, :] = acc


def _conv_decode(u, row0, dec_batch, dec_seq, state, w, wshift, b):
    rows_n = DEC_GROUP * dec_seq
    r0 = row0 // rows_n
    return pl.pallas_call(
        functools.partial(_conv_decode_body, dec_seq=dec_seq),
        grid=(dec_batch // DEC_GROUP,),
        in_specs=[pl.BlockSpec((rows_n, D_MODEL), lambda i: (r0 + i, 0)),
                  pl.BlockSpec((DEC_GROUP, CONV_W - 1, D_MODEL), lambda i: (i, 0, 0)),
                  _full(w.shape), _full(wshift.shape), _full(b.shape)],
        out_specs=pl.BlockSpec((rows_n, D_MODEL), lambda i: (i, 0)),
        out_shape=jax.ShapeDtypeStruct((dec_batch * dec_seq, D_MODEL), F32),
        compiler_params=_params(("arbitrary",)),
        name="conv_decode",
    )(u, state, w, wshift, b)


def _extract_top(s, idx, count, none_rank):
    work = s
    rank = jnp.full(s.shape, none_rank, F32)
    vals = []
    for r in range(count):
        m = jnp.max(work, axis=0, keepdims=True)
        first = jnp.min(jnp.where(work == m, idx, np.float32(1e9)), axis=0, keepdims=True)
        hit = idx == first
        rank = jnp.where(hit, np.float32(r), rank)
        work = jnp.where(hit, -jnp.inf, work)
        vals.append(m)
    return rank, vals


_PEER_CAND = [(a, b) for a in range(PEER_TOPK) for b in range(PEER_TOPK) if (a + 1) * (b + 1) <= PEER_TOPK]
_PEER_CAND_ROWS = -(-len(_PEER_CAND) // 8) * 8


def _extract_by_value(s, count):
    work = s
    rank = jnp.full(s.shape, np.float32(count), F32)
    vals = []
    for r in range(count):
        m = jnp.max(work, axis=0, keepdims=True)
        hit = work == m
        rank = jnp.where(hit, np.float32(r), rank)
        work = jnp.where(hit, -jnp.inf, work)
        vals.append(m)
    ranked = jnp.sum(jnp.where(rank < count, 1.0, 0.0), axis=0, keepdims=True)
    return rank, vals, ranked


def _peer_select_body(q_ref, keys_ref, rank_ref, qe_ref, lq_ref, pe_ref):
    ts = q_ref.shape[0]
    kidx = lax.broadcasted_iota(jnp.int32, (N_KEYS, ts), 0).astype(F32)
    cidx = lax.broadcasted_iota(jnp.int32, (_PEER_CAND_ROWS, ts), 0).astype(F32)
    tidx = lax.broadcasted_iota(jnp.int32, (PEER_TOPK, ts), 0).astype(F32)
    bidx = lax.broadcasted_iota(jnp.int32, (SUBLANES, ts), 0).astype(F32)
    pad =jnp.full((_PEER_CAND_ROWS, ts), -jnp.inf, F32)

    def tables(h, s0, s1, exact):
        if exact:
            rank0, v0 = _extract_top(s0, kidx, PEER_TOPK, PEER_TOPK)
            rank1, v1 = _extract_top(s1, kidx, PEER_TOPK, PEER_TOPK)
        else:
            rank0, v0, n0 = _extract_by_value(s0, PEER_TOPK)
            rank1, v1, n1 = _extract_by_value(s1, PEER_TOPK)
        if exact:
            pairs = list(_PEER_CAND)
            cand = pad
            for ci, (a, b) in enumerate(pairs):
                cand = jnp.where(cidx == ci, v0[a] + v1[b], cand)
            crank, _ = _extract_top(cand, cidx, PEER_TOPK, PEER_TOPK)
            off = None
        else:
            col0 = jnp.zeros((PEER_TOPK, ts), F32)
            col1 = jnp.zeros((PEER_TOPK, ts), F32)
            low1 = jnp.zeros((SUBLANES, ts), F32)
            for r in range(PEER_TOPK):
                col0 = jnp.where(tidx == r, v0[r], col0)
                col1 = jnp.where(tidx == r, v1[r], col1)
                if r < SUBLANES:
                    low1 = jnp.where(bidx == r, v1[r], low1)
            ninf = np.float32(-np.inf)
            groups = [v0[0] + col1, jnp.where(tidx == 0, ninf, col0 + v1[0])]
            pairs = [(0, b) for b in range(PEER_TOPK)] + [(a, 0) if a else None for a in range(PEER_TOPK)]
            rest = [p for p in _PEER_CAND if p[0] and p[1]]
            for a in sorted({p[0] for p in rest}):
                mine = [p for p in rest if p[0] == a]
                if len(mine) > 2:
                    groups.append(jnp.where((bidx >= 1) & (bidx <= len(mine)), v0[a] + low1, ninf))
                    pairs += [(a, b) if 1 <= b <= len(mine) else None for b in range(SUBLANES)]
            loose = [p for p in rest if p not in pairs]
            assert len(loose) <= SUBLANES
            tail = jnp.full((SUBLANES, ts), ninf, F32)
            for r, (a, b) in enumerate(loose):
                tail = jnp.where(bidx == r, v0[a] + v1[b], tail)
            groups.append(tail)
            pairs += loose + [None] * (SUBLANES - len(loose))
            assert sorted(p for p in pairs if p) == sorted(_PEER_CAND)
            cand = jnp.concatenate(groups, axis=0)
            crank, _, nc = _extract_by_value(cand, PEER_TOPK)
            off = jnp.max(jnp.abs(n0 - PEER_TOPK) + jnp.abs(n1 - PEER_TOPK) + jnp.abs(nc - PEER_TOPK))
        picked = jnp.where(crank < PEER_TOPK, 1.0, 0.0)
        e = picked * jnp.exp(cand - (v0[0] + v1[0]))
        z = jnp.sum(e, axis=0, keepdims=True)
        lq = jnp.zeros((N_KEYS, ts), F32)
        for a in range(PEER_TOPK):
            cnt = jnp.zeros((1, ts), F32)
            for ci, p in enumerate(pairs):
                if p is not None and p[0] == a:
                    cnt = cnt + picked[ci:ci + 1]
            lq = jnp.where(rank0 == a, cnt, lq)
        rank_b = rank1.astype(BF16)
        qe_b = jnp.exp(s1 - v1[0]).astype(BF16)
        for k in range(N_KEYS // BF16_ROWS):
            rank_ref[h, k] = pltpu.bitcast(rank_b[k * BF16_ROWS:(k + 1) * BF16_ROWS], jnp.uint32)
            qe_ref[h, k] = pltpu.bitcast(qe_b[k * BF16_ROWS:(k + 1) * BF16_ROWS], jnp.uint32)
        lq_ref[h] = lq
        pe_ref[h] = jnp.exp(s0 - v0[0]) / z
        return off

    def head(h, carry):
        c0 = pl.multiple_of(h * 2 * PEER_DKH, 2 * PEER_DKH)
        q0 = q_ref[:, pl.ds(c0, PEER_DKH)].astype(BF16)
        q1 = q_ref[:, pl.ds(c0 + PEER_DKH, PEER_DKH)].astype(BF16)
        s0 = _dot_nt(keys_ref[h, 0], q0)
        s1 = _dot_nt(keys_ref[h, 1], q1)
        off = tables(h, s0, s1, exact=False)

        @pl.when(off > 0.5)
        def _():
            tables(h, s0, s1, exact=True)

        return carry

    lax.fori_loop(0, PEER_HEADS, head, 0)


def _peer_select(q, keys):
    n = q.shape[0]
    out = jax.ShapeDtypeStruct((PEER_HEADS, N_KEYS, n), F32)
    ospec = pl.BlockSpec((PEER_HEADS, N_KEYS, SEL_TILE), lambda i: (0, 0, i))
    packed = (PEER_HEADS, N_KEYS // BF16_ROWS, BF16_ROWS // 2)
    out_b = jax.ShapeDtypeStruct(packed + (n,), jnp.uint32)
    ospec_b = pl.BlockSpec(packed + (SEL_TILE,), lambda i: (0, 0, 0, i))
    return pl.pallas_call(
        _peer_select_body,
        grid=(n // SEL_TILE,),
        in_specs=[pl.BlockSpec((SEL_TILE, q.shape[1]), lambda i: (i, 0)), _full(keys.shape)],
        out_specs=[ospec_b, ospec_b, ospec, ospec],
        out_shape=[out_b, out_b, out, out],
        compiler_params=_params(("arbitrary",)),
        name="peer_select",
    )(q, keys)


def _peer_dense_body(ht_ref, u_ref, vt_ref, rank_ref, qe_ref, lq_ref, pe_ref, x_ref, *rest, prompt_tiles):
    *o_ref, acc_ref, a_ref, w_ref = rest
    step = pl.program_id(1)

    @pl.when(step == 0)
    def _():
        acc_ref[...] = jnp.zeros_like(acc_ref)

    @pl.when(step >= 0)
    def _():
        quarter = PEER_ROWS * N_KEYS // 4
        for qi in range(4):
            rows = slice(qi * quarter, (qi + 1) * quarter)
            a_ref[rows, :] = _dot(u_ref[rows, :], ht_ref[...])

    @pl.when(step >= 0)
    def _():
        sub = (N_KEYS // BF16_ROWS, BF16_ROWS, LANES)
        for il in range(PEER_ROWS):
            for ci in range(TOKEN_TILE // LANES):
                cs = slice(ci * LANES, (ci + 1) * LANES)
                gate = jnp.zeros(sub, BF16)
                for h in range(PEER_HEADS):
                    lq = jnp.broadcast_to(lq_ref[h, il:il + 1, cs], sub[1:]).astype(BF16)
                    pe = jnp.broadcast_to(pe_ref[h, il:il + 1, cs], sub[1:]).astype(BF16)
                    take = pltpu.bitcast(rank_ref[h, :, :, cs], BF16) < lq[None]
                    qe = pltpu.bitcast(qe_ref[h, :, :, cs], BF16)
                    gate = gate + jnp.where(take, qe, jnp.zeros(sub, BF16)) * pe[None]
                for k in range(sub[0]):
                    rs = slice(il * N_KEYS + k * BF16_ROWS, il * N_KEYS + (k + 1) * BF16_ROWS)
                    w_ref[rs, cs] = gate[k] * _gelu(a_ref[rs, cs]).astype(BF16)
        acc_ref[...] += _dot(vt_ref[...], w_ref[...])

    last = step == pl.num_programs(1) - 1
    if prompt_tiles is None:
        @pl.when(last)
        def _():
            o_ref[0][...] = x_ref[...] + acc_ref[...].T
    else:
        tile = pl.program_id(0)
        o_prompt, o_sample = o_ref

        @pl.when(last & (tile < prompt_tiles))
        def _():
            o_prompt[...] = x_ref[...] + acc_ref[...].T

        @pl.when(last & (tile >= prompt_tiles))
        def _():
            o_sample[...] = x_ref[...] + acc_ref[...].T


def _peer_dense(ht, u, vt, rank, qe, lq, pe, x, n_prompt=None):
    n = x.shape[0]
    ex = PEER_ROWS * N_KEYS
    steps = N_KEYS // PEER_ROWS
    tok = pl.BlockSpec((PEER_HEADS, N_KEYS // BF16_ROWS, BF16_ROWS // 2, TOKEN_TILE), lambda t, e: (0, 0, 0, t))
    rowsel = pl.BlockSpec((PEER_HEADS, PEER_ROWS, TOKEN_TILE), lambda t, e: (0, e, t))
    if n_prompt is None:
        tp = None
        out_specs = pl.BlockSpec((TOKEN_TILE, D_MODEL), lambda t, e: (t, 0))
        out_shape = jax.ShapeDtypeStruct((n, D_MODEL), F32)
    else:
        tp = n_prompt // TOKEN_TILE
        out_specs = [pl.BlockSpec((TOKEN_TILE, D_MODEL), lambda t, e: (jnp.minimum(t, tp - 1), 0)),
                     pl.BlockSpec((TOKEN_TILE, D_MODEL), lambda t, e: (jnp.maximum(t - tp, 0), 0))]
        out_shape = [jax.ShapeDtypeStruct((n_prompt, D_MODEL), F32),
                     jax.ShapeDtypeStruct((n - n_prompt, D_MODEL), F32)]
    return pl.pallas_call(
        functools.partial(_peer_dense_body, prompt_tiles=tp),
        grid=(n // TOKEN_TILE, steps),
        in_specs=[pl.BlockSpec((D_MODEL, TOKEN_TILE), lambda t, e: (0, t)),
                  pl.BlockSpec((ex, D_MODEL), lambda t, e: (e, 0)),
                  pl.BlockSpec((D_MODEL, ex), lambda t, e: (0, e)),
                  tok, tok, rowsel, rowsel,
                  pl.BlockSpec((TOKEN_TILE, D_MODEL), lambda t, e: (t, 0))],
        out_specs=out_specs,
        out_shape=out_shape,
        scratch_shapes=[pltpu.VMEM((D_MODEL, TOKEN_TILE), F32), pltpu.VMEM((ex, TOKEN_TILE), F32),
                        pltpu.VMEM((ex, TOKEN_TILE), BF16)],
        compiler_params=_params(("arbitrary", "arbitrary")),
        name="peer_dense",
    )(ht, u, vt, rank, qe, lq, pe, x)


def _peer(x1, ht, q, keys, u, vt, n_prompt=None):
    rank, qe, lq, pe = _peer_select(q, keys)
    return _peer_dense(ht, u, vt, rank, qe, lq, pe, x1, n_prompt)


def _block_diag_mean(width):
    idx = np.arange(width) // HEAD_DIM
    return jnp.asarray((idx[:, None] == idx[None, :]).astype(np.float32) / HEAD_DIM)


def kernel(x_prompt, x_sample, cache_swa_k, cache_swa_v, state_hgrn, state_conv, norm_mix_g, norm_ffn_g, w_in0, attn_q_norm_g, attn_k_norm_g, attn_sinks, hgrn_lb, hgrn_o_norm_g, w_out0, conv_w_pw1, conv_b_pw1, conv_w_dw, conv_b_dw, conv_ln_g, conv_ln_b, conv_w_pw2, peer_w_q, peer_keys, peer_u, peer_v):
    batch, seq, _ = x_prompt.shape
    dec_batch, dec_seq, _ = x_sample.shape
    n_p = batch * seq
    n_s = dec_batch * dec_seq
    assert seq % CONV_ROWS == 0 and seq % HGRN_ROWS == 0 and n_p % TOKEN_TILE == 0 and n_s % TOKEN_TILE == 0
    assert dec_batch % DEC_GROUP == 0 and 8 % dec_seq == 0 and dec_seq <= CONV_W - 1

    x = (x_prompt.reshape(n_p, D_MODEL), x_sample.reshape(n_s, D_MODEL))
    row = lambda v: v.reshape(1, -1).astype(F32)

    z = _in0(x, row(norm_mix_g[0]), w_in0.astype(BF16))
    d = np.arange(LANES) % HEAD_DIM
    inv = ROPE_THETA ** (-jnp.arange(0, ROT_DIM, 2, dtype=F32) / ROT_DIM)
    inv_lane = jnp.where(jnp.asarray(d < ROT_DIM), inv[d % (ROT_DIM // 2)], 0.0).reshape(1, LANES)
    gq = row(jnp.tile(attn_q_norm_g, N_HEADS_A))
    gk = row(jnp.tile(attn_k_norm_g, N_KV_A))
    segq, segk = _block_diag_mean(QA), _block_diag_mean(KVA)
    sinks = attn_sinks.astype(F32)
    attn_p, k_win_p, v_win_p = _attn_prompt(z, batch, seq, gq, gk, inv_lane, segq, segk, sinks)
    kc = cache_swa_k.reshape(dec_batch * WINDOW, KVA)
    vc = cache_swa_v.reshape(dec_batch * WINDOW, KVA)
    attn_s, k_new, v_new = _attn_decode(z, n_p, dec_batch, dec_seq, kc, vc, gq, gk, inv_lane, segq, segk, sinks)

    go = row(hgrn_o_norm_g)
    tril = jnp.asarray(np.tril(np.ones((HGRN_CHUNK, HGRN_CHUNK), np.float32)), BF16)
    hg_p, s_p = _hgrn_prompt(z, batch, seq, hgrn_lb, go, tril)
    r = np.arange(LANES)
    tril_dec = jnp.asarray(((r[:, None] // dec_seq == r[None, :] // dec_seq) & (r[None, :] <= r[:, None]))
                           .astype(np.float32))
    hg_s, s_s = _hgrn_decode(z, n_p, dec_batch, dec_seq, state_hgrn, hgrn_lb, go, tril_dec)

    x1, ht, q = _mix_out(_out0_body, [(attn_p, attn_s), (hg_p, hg_s)], x, [w_out0.astype(BF16)], row(norm_ffn_g[0]),
                         peer_w_q[0].astype(BF16), "mix0_out")
    x2 = _peer(x1, ht, q, peer_keys[0].astype(BF16), peer_u[0].astype(BF16), peer_v[0].astype(BF16).T)

    u = _glu(x2, row(norm_mix_g[1]), conv_w_pw1.astype(BF16), row(conv_b_pw1))
    wdw = jnp.concatenate([conv_w_dw, jnp.zeros((1, D_MODEL), F32)], axis=0)
    bdw = row(conv_b_dw)
    c_p = _conv_prompt(u, batch, seq, wdw, bdw)
    wshift = jnp.stack([jnp.concatenate([jnp.zeros((t, D_MODEL), F32), conv_w_dw[:CONV_W - 1 - t]], axis=0)
                        for t in range(dec_seq)])
    c_s = _conv_decode(u, n_p, dec_batch, dec_seq, state_conv, wdw, wshift, bdw)
    x3, ht, q = _mix_out(_post1_body, [(c_p, c_s)], x2, [row(conv_ln_g), row(conv_ln_b), conv_w_pw2.astype(BF16)],
                         row(norm_ffn_g[1]), peer_w_q[1].astype(BF16), "mix1_out")
    y_p, y_s = _peer(x3, ht, q, peer_keys[1].astype(BF16), peer_u[1].astype(BF16),
                     peer_v[1].astype(BF16).T, n_p)

    kv = lambda t: t.reshape(t.shape[0], WINDOW, N_KV_A, HEAD_DIM)
    k_win_s = jnp.concatenate([cache_swa_k[:, dec_seq:], k_new.reshape(dec_batch, dec_seq, N_KV_A, HEAD_DIM)], axis=1)
    v_win_s = jnp.concatenate([cache_swa_v[:, dec_seq:], v_new.reshape(dec_batch, dec_seq, N_KV_A, HEAD_DIM)], axis=1)
    u_p = u[:n_p].reshape(batch, seq, D_MODEL)
    u_s = u[n_p:].reshape(dec_batch, dec_seq, D_MODEL)
    conv_buf_p = u_p[:, seq - (CONV_W - 1):]
    conv_buf_s = jnp.concatenate([state_conv[:, dec_seq:], u_s], axis=1)
    return (y_p.reshape(batch, seq, D_MODEL), y_s.reshape(dec_batch, dec_seq, D_MODEL),
            kv(k_win_p), kv(v_win_p), s_p, conv_buf_p, k_win_s, v_win_s, s_s, conv_buf_s)
```

```python
import functools

import numpy as np
import jax
import jax.numpy as jnp
from jax import lax
from jax.experimental import pallas as pl
from jax.experimental.pallas import tpu as pltpu

F32 = jnp.float32
BF16 = jnp.bfloat16
HI = lax.Precision.HIGHEST

D_MODEL = 1024
PAST_LEN = 8192
HEAD_DIM = 64
N_HEADS_A = 8
N_KV_A = 2
KV_REP = N_HEADS_A // N_KV_A
WINDOW = 128
ROT_DIM = HEAD_DIM // 4
ROPE_THETA = 500000.0
ATTN_SCALE = HEAD_DIM ** -0.5
NEG_INF = -1e30
N_HEADS_B = 4
DK_B = 128
DV_B = 128
CONV_W = 31
N_KEYS = 128
PEER_HEADS = 8
PEER_TOPK = 16
PEER_DKH = 128
NORM_EPS = 1e-6

QA = N_HEADS_A * HEAD_DIM
KVA = N_KV_A * HEAD_DIM
QB = N_HEADS_B * DK_B
VB = N_HEADS_B * DV_B
IN0_WIDTH = QA + 2 * KVA + 2 * QB + 2 * VB
COL_K = QA // 128
COL_V = (QA + KVA) // 128
COL_QB = (QA + 2 * KVA) // 128
COL_FB = COL_QB + QB // 128
COL_IB = COL_FB + QB // 128
COL_GB = COL_IB + VB // 128

LANES = 128
SUBLANES = 8
TOKEN_TILE = 512
SEL_TILE = 256
HGRN_CHUNK = 64
HGRN_SUB = 16
HGRN_ROWS = 256
HGRN_HEADS_PER_STEP = 2
CONV_ROWS = 512
CONV_HALO = 32
CONV_CHUNK = 128
PEER_ROWS = 16
BF16_ROWS = 16
DEC_GROUP = 8
VMEM_LIMIT = 48 * 1024 * 1024


def _dot(a, b, prec=None):
    return jnp.dot(a, b, preferred_element_type=F32, precision=prec)


def _dot_nt(a, b, prec=None):
    return lax.dot_general(a, b, (((1,), (1,)), ((), ())), preferred_element_type=F32, precision=prec)


def _rms(x, g):
    return x * lax.rsqrt(jnp.mean(x * x, axis=-1, keepdims=True) + NORM_EPS) * g


def _silu(x):
    return x * jax.nn.sigmoid(x)


def _gelu(x):
    return 0.5 * x * (1.0 + lax.erf(x * np.float32(0.7071067811865476)))


def _params(sem, flags=None):
    return pltpu.CompilerParams(dimension_semantics=sem, vmem_limit_bytes=VMEM_LIMIT, flags=flags)


def _full(shape):
    n = len(shape)
    return pl.BlockSpec(shape, lambda *_: (0,) * n)


def _row_inputs(arrs):
    specs, ops, layout = [], [], []
    for a in arrs:
        if isinstance(a, tuple):
            tp = a[0].shape[0] // TOKEN_TILE
            width = a[0].shape[1]
            specs += [pl.BlockSpec((TOKEN_TILE, width), lambda i, tp=tp: (jnp.minimum(i, tp - 1), 0)),
                      pl.BlockSpec((TOKEN_TILE, width), lambda i, tp=tp: (jnp.maximum(i - tp, 0), 0))]
            ops += list(a)
            layout.append(tp)
        else:
            specs.append(pl.BlockSpec((TOKEN_TILE, a.shape[1]), lambda i: (i, 0)))
            ops.append(a)
            layout.append(None)
    return specs, ops, tuple(layout)


def _row_values(layout, refs):
    i = pl.program_id(0)
    vals, k = [], 0
    for tp in layout:
        if tp is None:
            vals.append(refs[k][...])
            k += 1
        else:
            vals.append(jnp.where(i < tp, refs[k][...], refs[k + 1][...]))
            k += 2
    return vals, refs[k:]


def _rows_of(a):
    return a[0].shape[0] + a[1].shape[0] if isinstance(a, tuple) else a.shape[0]


def _in0_body(*refs, layout):
    (x,), (g_ref, w_ref, z_ref) = _row_values(layout, refs)
    h = _rms(x, g_ref[...]).astype(BF16)
    z_ref[...] = _dot(h, w_ref[...])


def _in0(x, g, w):
    n = _rows_of(x)
    width = w.shape[1]
    specs, ops, layout = _row_inputs([x])
    return pl.pallas_call(
        functools.partial(_in0_body, layout=layout),
        grid=(n // TOKEN_TILE,),
        in_specs=specs + [_full((1, D_MODEL)), _full((D_MODEL, width))],
        out_specs=pl.BlockSpec((TOKEN_TILE, width), lambda i: (i, 0)),
        out_shape=jax.ShapeDtypeStruct((n, width), F32),
        compiler_params=_params(("arbitrary",)),
        name="in0_proj",
    )(*ops, g, w)


def _glu_body(x_ref, g_ref, w_ref, b_ref, u_ref):
    h = _rms(x_ref[...], g_ref[...]).astype(BF16)
    a = _dot(h, w_ref[...]) + b_ref[...]
    u_ref[...] = a[:, :D_MODEL] * jax.nn.sigmoid(a[:, D_MODEL:])


def _glu(x, g, w, b):
    n = x.shape[0]
    return pl.pallas_call(
        _glu_body,
        grid=(n // TOKEN_TILE,),
        in_specs=[pl.BlockSpec((TOKEN_TILE, D_MODEL), lambda i: (i, 0)), _full((1, D_MODEL)),
                  _full((D_MODEL, 2 * D_MODEL)), _full((1, 2 * D_MODEL))],
        out_specs=pl.BlockSpec((TOKEN_TILE, D_MODEL), lambda i: (i, 0)),
        out_shape=jax.ShapeDtypeStruct((n, D_MODEL), F32),
        compiler_params=_params(("arbitrary",)),
        name="conv_glu",
    )(x, g, w, b)


def _ffn_query(x1, gf_ref, wq_ref, x1_ref, ht_ref, q_ref):
    x1_ref[...] = x1
    h2 = _rms(x1, gf_ref[...])
    ht_ref[...] = h2.T.astype(BF16)
    q_ref[...] = _dot(h2.astype(BF16), wq_ref[...])


def _out0_body(*refs, layout):
    (attn, hg, x), (w_ref, gf_ref, wq_ref, x1_ref, ht_ref, q_ref) = _row_values(layout, refs)
    m = _dot(attn.astype(BF16), w_ref[:QA, :]) + _dot(hg.astype(BF16), w_ref[QA:, :])
    _ffn_query(x + m, gf_ref, wq_ref, x1_ref, ht_ref, q_ref)


def _post1_body(*refs, layout):
    (c, x), (lg_ref, lb_ref, w_ref, gf_ref, wq_ref, x1_ref, ht_ref, q_ref) = _row_values(layout, refs)
    mu = jnp.mean(c, axis=-1, keepdims=True)
    d = c - mu
    var = jnp.mean(d * d, axis=-1, keepdims=True)
    ln = d * lax.rsqrt(var + NORM_EPS) * lg_ref[...] + lb_ref[...]
    y = _dot(_silu(ln).astype(BF16), w_ref[...])
    _ffn_query(x + y, gf_ref, wq_ref, x1_ref, ht_ref, q_ref)


def _mix_out(body, acts, x, consts, gf, wq, name):
    n = _rows_of(x)
    qw = wq.shape[1]
    row = lambda w: pl.BlockSpec((TOKEN_TILE, w), lambda i: (i, 0))
    specs, ops, layout = _row_inputs(list(acts) + [x])
    return pl.pallas_call(
        functools.partial(body, layout=layout),
        grid=(n // TOKEN_TILE,),
        in_specs=specs + [_full(c.shape) for c in consts] + [_full(gf.shape), _full(wq.shape)],
        out_specs=[row(D_MODEL), pl.BlockSpec((D_MODEL, TOKEN_TILE), lambda i: (0, i)), row(qw)],
        out_shape=[jax.ShapeDtypeStruct((n, D_MODEL), F32), jax.ShapeDtypeStruct((D_MODEL, n), BF16),
                   jax.ShapeDtypeStruct((n, qw), F32)],
        compiler_params=_params(("arbitrary",)),
        name=name,
    )(*ops, *consts, gf, wq)


def _head_norm(x, g, seg):
    ms = _dot(x * x, seg, HI)
    return x * lax.rsqrt(ms + NORM_EPS) * g


def _rope(x, cos, sin, first_half):
    half = ROT_DIM // 2
    width = x.shape[1]
    up = pltpu.roll(x, width - half, axis=1)
    dn = pltpu.roll(x, half, axis=1)
    return x * cos + jnp.where(first_half, -up, dn) * sin


def _rope_tables(pos, inv_ref, reps):
    ang = pos * inv_ref[...]
    c, s = jnp.cos(ang), jnp.sin(ang)
    if reps > 1:
        c, s = (jnp.concatenate([t] * reps, axis=1) for t in (c, s))
    lane = lax.broadcasted_iota(jnp.int32, c.shape, 1)
    return c, s, (lane % HEAD_DIM) < (ROT_DIM // 2)


def _stack_heads(x, g):
    return jnp.concatenate(
        [x[:, (g * KV_REP + r) * HEAD_DIM:(g * KV_REP + r + 1) * HEAD_DIM] for r in range(KV_REP)], axis=0)


def _attn_prompt_body(q_ref, k_ref, v_ref, gq_ref, gk_ref, inv_ref, segq_ref, segk_ref, sink_ref,
                      o_ref, kw_ref, vw_ref, kprev_ref, vprev_ref):
    j = pl.program_id(1)
    w = WINDOW

    @pl.when(j == 0)
    def _():
        kprev_ref[...] = jnp.zeros_like(kprev_ref)
        vprev_ref[...] = jnp.zeros_like(vprev_ref)

    rows = lax.broadcasted_iota(jnp.int32, (w, LANES), 0)
    pos = (j * w + rows).astype(F32)
    ck, sk, fk = _rope_tables(pos, inv_ref, 1)
    cq, sq, fq = _rope_tables(pos, inv_ref, QA // LANES)
    q = _rope(_head_norm(q_ref[...], gq_ref[...], segq_ref[...]), cq, sq, fq)
    k = _rope(_head_norm(k_ref[...], gk_ref[...], segk_ref[...]), ck, sk, fk)
    v = v_ref[...]
    kp = kprev_ref[...]
    vp = vprev_ref[...]

    qi = lax.broadcasted_iota(jnp.int32, (KV_REP * w, w), 0) % w
    ki = lax.broadcasted_iota(jnp.int32, (KV_REP * w, w), 1)
    m_own = ki <= qi
    m_prev = (ki > qi) & (j > 0)
    rep = lax.broadcasted_iota(jnp.int32, (KV_REP * w, 1), 0) // w
    for g in range(N_KV_A):
        sl = slice(g * HEAD_DIM, (g + 1) * HEAD_DIM)
        qg = _stack_heads(q, g).astype(BF16)
        s_own = jnp.where(m_own, _dot_nt(qg, k[:, sl].astype(BF16)) * ATTN_SCALE, NEG_INF)
        s_prev = jnp.where(m_prev, _dot_nt(qg, kp[:, sl].astype(BF16)) * ATTN_SCALE, NEG_INF)
        sink = jnp.zeros((KV_REP * w, 1), F32)
        for r in range(KV_REP):
            sink = jnp.where(rep == r, sink_ref[g * KV_REP + r], sink)
        mx = jnp.maximum(jnp.maximum(jnp.max(s_own, axis=-1, keepdims=True),
                                     jnp.max(s_prev, axis=-1, keepdims=True)), sink)
        e_own = jnp.exp(s_own - mx)
        e_prev = jnp.exp(s_prev - mx)
        den = (jnp.sum(e_own, axis=-1, keepdims=True) + jnp.sum(e_prev, axis=-1, keepdims=True)
               + jnp.exp(sink - mx))
        o = (_dot(e_own.astype(BF16), v[:, sl].astype(BF16))
             + _dot(e_prev.astype(BF16), vp[:, sl].astype(BF16))) / den
        for r in range(KV_REP):
            hq = g * KV_REP + r
            o_ref[:, hq * HEAD_DIM:(hq + 1) * HEAD_DIM] = o[r * w:(r + 1) * w]

    kprev_ref[...] = k
    vprev_ref[...] = v

    @pl.when(j == pl.num_programs(1) - 1)
    def _():
        kw_ref[0] = k
        vw_ref[0] = v


def _attn_prompt(z, batch, seq, gq, gk, inv, segq, segk, sinks):
    nb = seq // WINDOW
    blk = lambda width, col: pl.BlockSpec((WINDOW, width), lambda b, j: (b * nb + j, col))
    return pl.pallas_call(
        _attn_prompt_body,
        grid=(batch, nb),
        in_specs=[blk(QA, 0), blk(KVA, COL_K), blk(KVA, COL_V), _full(gq.shape), _full(gk.shape),
                  _full(inv.shape), _full(segq.shape), _full(segk.shape),
                  pl.BlockSpec(memory_space=pltpu.SMEM)],
        out_specs=[pl.BlockSpec((WINDOW, QA), lambda b, j: (b * nb + j, 0)),
                   pl.BlockSpec((1, WINDOW, KVA), lambda b, j: (b, 0, 0)),
                   pl.BlockSpec((1, WINDOW, KVA), lambda b, j: (b, 0, 0))],
        out_shape=[jax.ShapeDtypeStruct((batch * seq, QA), F32),
                   jax.ShapeDtypeStruct((batch, WINDOW, KVA), F32),
                   jax.ShapeDtypeStruct((batch, WINDOW, KVA), F32)],
        scratch_shapes=[pltpu.VMEM((WINDOW, KVA), F32), pltpu.VMEM((WINDOW, KVA), F32)],
        compiler_params=_params(("arbitrary", "arbitrary")),
        name="swa_prompt",
    )(z, z, z, gq, gk, inv, segq, segk, sinks)


def _attn_decode_body(q_ref, k_ref, v_ref, kc_ref, vc_ref, gq_ref, gk_ref, inv_ref, segq_ref, segk_ref,
                      sink_ref, o_ref, kn_ref, vn_ref, *, dec_seq):
    rows_n = DEC_GROUP * dec_seq
    rows = lax.broadcasted_iota(jnp.int32, (rows_n, LANES), 0)
    pos = (PAST_LEN + rows % dec_seq).astype(F32)
    ck, sk, fk = _rope_tables(pos, inv_ref, 1)
    cq, sq, fq = _rope_tables(pos, inv_ref, QA // LANES)
    q = _rope(_head_norm(q_ref[...], gq_ref[...], segq_ref[...]), cq, sq, fq)
    k = _rope(_head_norm(k_ref[...], gk_ref[...], segk_ref[...]), ck, sk, fk)
    v = v_ref[...]
    kn_ref[...] = k
    vn_ref[...] = v

    nq = KV_REP * rows_n
    nc = DEC_GROUP * WINDOW

    def qrow(shape):
        r = lax.broadcasted_iota(jnp.int32, shape, 0) % rows_n
        return r // dec_seq, r % dec_seq

    bq, tq = qrow((nq, nc))
    col = lax.broadcasted_iota(jnp.int32, (nq, nc), 1)
    m_cache = (col // WINDOW == bq) & (col % WINDOW > tq)
    bq, tq = qrow((nq, rows_n))
    col = lax.broadcasted_iota(jnp.int32, (nq, rows_n), 1)
    m_new = (col // dec_seq == bq) & (col % dec_seq <= tq)
    rep = lax.broadcasted_iota(jnp.int32, (nq, 1), 0) // rows_n
    for g in range(N_KV_A):
        sl = slice(g * HEAD_DIM, (g + 1) * HEAD_DIM)
        qg = _stack_heads(q, g).astype(BF16)
        s_c = jnp.where(m_cache, _dot_nt(qg, kc_ref[:, sl].astype(BF16)) * ATTN_SCALE, NEG_INF)
        s_n = jnp.where(m_new, _dot_nt(qg, k[:, sl].astype(BF16)) * ATTN_SCALE, NEG_INF)
        sink = jnp.zeros((nq, 1), F32)
        for r in range(KV_REP):
            sink = jnp.where(rep == r, sink_ref[g * KV_REP + r], sink)
        mx = jnp.maximum(jnp.maximum(jnp.max(s_c, axis=-1, keepdims=True),
                                     jnp.max(s_n, axis=-1, keepdims=True)), sink)
        e_c = jnp.exp(s_c - mx)
        e_n = jnp.exp(s_n - mx)
        den = jnp.sum(e_c, axis=-1, keepdims=True) + jnp.sum(e_n, axis=-1, keepdims=True) + jnp.exp(sink - mx)
        o = (_dot(e_c.astype(BF16), vc_ref[:, sl].astype(BF16))
             + _dot(e_n.astype(BF16), v[:, sl].astype(BF16))) / den
        for r in range(KV_REP):
            hq = g * KV_REP + r
            o_ref[:, hq * HEAD_DIM:(hq + 1) * HEAD_DIM] = o[r * rows_n:(r + 1) * rows_n]


def _attn_decode(z, row0, dec_batch, dec_seq, kc, vc, gq, gk, inv, segq, segk, sinks):
    rows_n = DEC_GROUP * dec_seq
    r0 = row0 // rows_n
    blk = lambda width, col: pl.BlockSpec((rows_n, width), lambda i: (r0 + i, col))
    cache = pl.BlockSpec((DEC_GROUP * WINDOW, KVA), lambda i: (i, 0))
    n = dec_batch * dec_seq
    return pl.pallas_call(
        functools.partial(_attn_decode_body, dec_seq=dec_seq),
        grid=(dec_batch // DEC_GROUP,),
        in_specs=[blk(QA, 0), blk(KVA, COL_K), blk(KVA, COL_V), cache, cache, _full(gq.shape),
                  _full(gk.shape), _full(inv.shape), _full(segq.shape), _full(segk.shape),
                  pl.BlockSpec(memory_space=pltpu.SMEM)],
        out_specs=[pl.BlockSpec((rows_n, QA), lambda i: (i, 0)), pl.BlockSpec((rows_n, KVA), lambda i: (i, 0)),
                   pl.BlockSpec((rows_n, KVA), lambda i: (i, 0))],
        out_shape=[jax.ShapeDtypeStruct((n, QA), F32), jax.ShapeDtypeStruct((n, KVA), F32),
                   jax.ShapeDtypeStruct((n, KVA), F32)],
        compiler_params=_params(("arbitrary",)),
        name="swa_decode",
    )(z, z, z, kc, vc, gq, gk, inv, segq, segk, sinks)


def _hgrn_lower_bound(lb_ref):
    l = lb_ref[...]
    e = jnp.exp(l - jnp.max(l, axis=0, keepdims=True))
    return e[0:1] / jnp.sum(e, axis=0, keepdims=True)


def _hgrn_gates(qb, fb, lb):
    logf = jnp.log(lb + (1.0 - lb) * jax.nn.sigmoid(fb))
    kb = (1.0 - lb) * jax.nn.sigmoid(-fb)
    return _silu(qb), kb, logf


def _hgrn_out(o, g, gate):
    return _rms(o, g) * _silu(gate)


def _hgrn_chunk(qh, kb, ih, logf, st, tril):
    c = qh.shape[0]
    hi = logf.astype(BF16)
    rest = logf - hi.astype(F32)
    mid = rest.astype(BF16)
    b = _dot(tril, hi) + _dot(tril, mid) + _dot(tril, (rest - mid.astype(F32)).astype(BF16))
    ones = jnp.ones((DK_B, LANES), BF16)
    st_b = st.astype(BF16)
    ih_b = ih.astype(BF16)
    o_parts = []
    row = lax.broadcasted_iota(jnp.int32, (c, 1), 0)
    trow = lax.broadcasted_iota(jnp.int32, (HGRN_SUB, 1), 0)
    for blk in range(c // HGRN_SUB):
        t0 = blk * HGRN_SUB
        bi = b[t0:t0 + HGRN_SUB]
        qi = qh[t0:t0 + HGRN_SUB]
        if blk == 0:
            oi = _dot_nt((qi * jnp.exp(bi)).astype(BF16), st_b)
        else:
            base = b[t0 - 1:t0]
            qd = qi * jnp.exp(bi - base)
            kd = jnp.where(row < t0, kb * jnp.exp(jnp.minimum(base - b, 0.0)), 0.0)
            a_off = _dot_nt(qd.astype(BF16), kd.astype(BF16))
            oi = _dot_nt((qd * jnp.exp(base)).astype(BF16), st_b) + _dot(a_off.astype(BF16), ih_b)
        prods = []
        for s in range(HGRN_SUB):
            e = jnp.exp(jnp.where(trow >= s, bi - bi[s:s + 1], 0.0))
            prods.append(qi * e * kb[t0 + s:t0 + s + 1])
        a_diag = _dot(jnp.concatenate(prods, axis=0).astype(BF16), ones)
        for s in range(HGRN_SUB):
            a = a_diag[s * HGRN_SUB:(s + 1) * HGRN_SUB]
            oi = oi + jnp.where(trow >= s, a, 0.0) * ih[t0 + s:t0 + s + 1]
        o_parts.append(oi)
    last = b[c - 1:c]
    st_new = st * jnp.exp(last) + _dot(ih.T.astype(BF16), (kb * jnp.exp(last - b)).astype(BF16))
    return jnp.concatenate(o_parts, axis=0), st_new


def _hgrn_prompt_body(qb_ref, fb_ref, ib_ref, gb_ref, lb_ref, go_ref, tril_ref, o_ref, s_ref, st_ref):
    tb = pl.program_id(2)

    @pl.when(tb == 0)
    def _():
        st_ref[...] = jnp.zeros_like(st_ref)

    tril = tril_ref[...]
    heads = [slice(h * DK_B, (h + 1) * DK_B) for h in range(HGRN_HEADS_PER_STEP)]
    lbs = [_hgrn_lower_bound(lb_ref.at[:, hs]) for hs in heads]
    sts = [st_ref[h] for h in range(HGRN_HEADS_PER_STEP)]
    for c in range(HGRN_ROWS // HGRN_CHUNK):
        rs = slice(c * HGRN_CHUNK, (c + 1) * HGRN_CHUNK)
        for h, hs in enumerate(heads):
            qh, kb, logf = _hgrn_gates(qb_ref[rs, hs], fb_ref[rs, hs], lbs[h])
            o, sts[h] = _hgrn_chunk(qh, kb, ib_ref[rs, hs], logf, sts[h], tril)
            o_ref[rs, hs] = _hgrn_out(o, go_ref[...], gb_ref[rs, hs])
    for h in range(HGRN_HEADS_PER_STEP):
        st_ref[h] = sts[h]

    @pl.when(tb == pl.num_programs(2) - 1)
    def _():
        for h in range(HGRN_HEADS_PER_STEP):
            s_ref[0, h] = sts[h].T


def _hgrn_prompt(z, batch, seq, lb, go, tril):
    nt = seq // HGRN_ROWS
    hp = HGRN_HEADS_PER_STEP
    assert N_HEADS_B % hp == 0 and all(c % hp == 0 for c in (COL_QB, COL_FB, COL_IB, COL_GB))
    blk = lambda col: pl.BlockSpec((HGRN_ROWS, hp * DK_B), lambda b, h, t: (b * nt + t, col // hp + h))
    return pl.pallas_call(
        _hgrn_prompt_body,
        grid=(batch, N_HEADS_B // hp, nt),
        in_specs=[blk(COL_QB), blk(COL_FB), blk(COL_IB), blk(COL_GB),
                  pl.BlockSpec((lb.shape[0], hp * DK_B), lambda b, h, t: (0, h)), _full(go.shape), _full(tril.shape)],
        out_specs=[pl.BlockSpec((HGRN_ROWS, hp * DV_B), lambda b, h, t: (b * nt + t, h)),
                   pl.BlockSpec((1, hp, DK_B, DV_B), lambda b, h, t: (b, h, 0, 0))],
        out_shape=[jax.ShapeDtypeStruct((batch * seq, VB), F32),
                   jax.ShapeDtypeStruct((batch, N_HEADS_B, DK_B, DV_B), F32)],
        scratch_shapes=[pltpu.VMEM((hp, DV_B, DK_B), F32)],
        compiler_params=_params(("arbitrary", "arbitrary", "arbitrary")),
        name="hgrn_prompt",
    )(z, z, z, z, lb, go, tril)


def _pad_rows(x, rows):
    return jnp.concatenate([x, jnp.zeros((rows - x.shape[0], x.shape[1]), x.dtype)], axis=0)


def _hgrn_decode_body(z_ref, s0_ref, lb_ref, go_ref, tril_ref, o_ref, s_ref, *, dec_seq):
    rows_n = z_ref.shape[0]
    groups = rows_n // dec_seq
    row = lax.broadcasted_iota(jnp.int32, (rows_n, 1), 0)
    for h in range(N_HEADS_B):
        col = lambda c0: slice((c0 + h) * LANES, (c0 + h + 1) * LANES)
        hs = slice(h * DK_B, (h + 1) * DK_B)
        lb = _hgrn_lower_bound(lb_ref.at[:, hs])
        qh, kb, logf = _hgrn_gates(z_ref[:, col(COL_QB)], z_ref[:, col(COL_FB)], lb)
        ih = z_ref[:, col(COL_IB)]
        b = _dot(tril_ref[...], _pad_rows(logf, LANES), HI)[:rows_n]
        b_t = _pad_rows(b, LANES).T
        ih_pad = _pad_rows(ih, LANES)
        o = jnp.zeros((rows_n, DV_B), F32)
        for e in range(groups):
            mine = (row // dec_seq) == e
            s0 = s0_ref[e, h]
            o = o + _dot(jnp.where(mine, qh * jnp.exp(b), 0.0), s0, HI)
            for s in range(dec_seq):
                r = e * dec_seq + s
                live = mine & (row >= r)
                ex = jnp.exp(jnp.where(live, b - b[r:r + 1], 0.0))
                a = jnp.sum(qh * ex * kb[r:r + 1], axis=-1, keepdims=True)
                o = o + jnp.where(live, a, 0.0) * ih[r:r + 1]
            r_last = (e + 1) * dec_seq - 1
            last = b[r_last:r_last + 1]
            kd = jnp.where(mine, kb * jnp.exp(jnp.minimum(last - b, 0.0)), 0.0)
            s_ref[e, h] = s0 * jnp.exp(b_t[:, r_last:r_last + 1]) + _dot(_pad_rows(kd, LANES).T, ih_pad, HI)
        o_ref[:, hs] = _hgrn_out(o, go_ref[...], z_ref[:, col(COL_GB)])


def _hgrn_decode(z, row0, dec_batch, dec_seq, s0, lb, go, tril):
    groups = 8 // dec_seq
    rows_n = groups * dec_seq
    r0 = row0 // rows_n
    st = pl.BlockSpec((groups, N_HEADS_B, DK_B, DV_B), lambda i: (i, 0, 0, 0))
    return pl.pallas_call(
        functools.partial(_hgrn_decode_body, dec_seq=dec_seq),
        grid=(dec_batch // groups,),
        in_specs=[pl.BlockSpec((rows_n, z.shape[1]), lambda i: (r0 + i, 0)), st,
                  _full(lb.shape), _full(go.shape), _full(tril.shape)],
        out_specs=[pl.BlockSpec((rows_n, VB), lambda i: (i, 0)), st],
        out_shape=[jax.ShapeDtypeStruct((dec_batch * dec_seq, VB), F32),
                   jax.ShapeDtypeStruct(s0.shape, F32)],
        compiler_params=_params(("arbitrary",)),
        name="hgrn_decode",
    )(z, s0, lb, go, tril)


def _conv_prompt_body(cur_ref, prev_ref, w_ref, b_ref, c_ref, ext_ref, sh_ref):
    t = pl.program_id(1)
    ext_ref[:CONV_HALO, :] = jnp.where(t > 0, prev_ref[...], 0.0)
    ext_ref[CONV_HALO:, :] = cur_ref[...]
    lead = CONV_HALO - (CONV_W - 1)
    chunk = CONV_CHUNK

    def cols(ci, carry):
        cs = pl.ds(pl.multiple_of(ci * LANES, LANES), LANES)
        for r in range(CONV_ROWS // chunk):
            acc = jnp.zeros((chunk, LANES), F32) + b_ref[:, cs]
            for res in range(SUBLANES):
                taps = range(res, CONV_W, SUBLANES)
                span = chunk + taps[-1] - res
                sh_ref[:span, :] = ext_ref[pl.ds(r * chunk + lead + res, span), cs]
                for w in taps:
                    acc = acc + sh_ref[w - res:w - res + chunk, :] * w_ref[pl.ds(w, 1), cs]
            c_ref[pl.ds(r * chunk, chunk), cs] = acc
        return carry

    lax.fori_loop(0, D_MODEL // LANES, cols, 0)


def _conv_prompt(u, batch, seq, w, b):
    nt = seq // CONV_ROWS
    per = CONV_ROWS // CONV_HALO
    return pl.pallas_call(
        _conv_prompt_body,
        grid=(batch, nt),
        in_specs=[pl.BlockSpec((CONV_ROWS, D_MODEL), lambda bi, t: (bi * nt + t, 0)),
                  pl.BlockSpec((CONV_HALO, D_MODEL), lambda bi, t: (jnp.maximum((bi * nt + t) * per - 1, 0), 0)),
                  _full(w.shape), _full(b.shape)],
        out_specs=pl.BlockSpec((CONV_ROWS, D_MODEL), lambda bi, t: (bi * nt + t, 0)),
        out_shape=jax.ShapeDtypeStruct((batch * seq, D_MODEL), F32),
        scratch_shapes=[pltpu.VMEM((CONV_HALO + CONV_ROWS, D_MODEL), F32),
                        pltpu.VMEM((CONV_CHUNK + CONV_HALO, LANES), F32)],
        compiler_params=_params(("arbitrary", "arbitrary")),
        name="conv_prompt",
    )(u, u, w, b)


def _conv_decode_body(u_ref, st_ref, w_ref, wshift_ref, b_ref, c_ref, *, dec_seq):
    for e in range(DEC_GROUP):
        past = st_ref[e]
        for t in range(dec_seq):
            acc = jnp.sum(past * wshift_ref[t], axis=0, keepdims=True) + b_ref[...]
            for t2 in range(t + 1):
                wi = CONV_W - 1 - t + t2
                acc = acc + u_ref[e * dec_seq + t2:e * dec_seq + t2 + 1, :] * w_ref[wi:wi + 1, :]
            c_ref[e * dec_seq + t:e * dec_seq + t + 1, :] = acc


def _conv_decode(u, row0, dec_batch, dec_seq, state, w, wshift, b):
    rows_n = DEC_GROUP * dec_seq
    r0 = row0 // rows_n
    return pl.pallas_call(
        functools.partial(_conv_decode_body, dec_seq=dec_seq),
        grid=(dec_batch // DEC_GROUP,),
        in_specs=[pl.BlockSpec((rows_n, D_MODEL), lambda i: (r0 + i, 0)),
                  pl.BlockSpec((DEC_GROUP, CONV_W - 1, D_MODEL), lambda i: (i, 0, 0)),
                  _full(w.shape), _full(wshift.shape), _full(b.shape)],
        out_specs=pl.BlockSpec((rows_n, D_MODEL), lambda i: (i, 0)),
        out_shape=jax.ShapeDtypeStruct((dec_batch * dec_seq, D_MODEL), F32),
        compiler_params=_params(("arbitrary",)),
        name="conv_decode",
    )(u, state, w, wshift, b)


def _extract_top(s, idx, count, none_rank):
    work = s
    rank = jnp.full(s.shape, none_rank, F32)
    vals = []
    for r in range(count):
        m = jnp.max(work, axis=0, keepdims=True)
        first = jnp.min(jnp.where(work == m, idx, np.float32(1e9)), axis=0, keepdims=True)
        hit = idx == first
        rank = jnp.where(hit, np.float32(r), rank)
        work = jnp.where(hit, -jnp.inf, work)
        vals.append(m)
    return rank, vals


_PEER_CAND = [(a, b) for a in range(PEER_TOPK) for b in range(PEER_TOPK) if (a + 1) * (b + 1) <= PEER_TOPK]
_PEER_CAND_ROWS = -(-len(_PEER_CAND) // 8) * 8


def _extract_by_value(s, count):
    work = s
    rank = jnp.full(s.shape, np.float32(count), F32)
    vals = []
    for r in range(count):
        m = jnp.max(work, axis=0, keepdims=True)
        hit = work == m
        rank = jnp.where(hit, np.float32(r), rank)
        work = jnp.where(hit, -jnp.inf, work)
        vals.append(m)
    ranked = jnp.sum(jnp.where(rank < count, 1.0, 0.0), axis=0, keepdims=True)
    return rank, vals, ranked


def _peer_select_body(q_ref, keys_ref, rank_ref, qe_ref, lq_ref, pe_ref):
    ts = q_ref.shape[0]
    kidx = lax.broadcasted_iota(jnp.int32, (N_KEYS, ts), 0).astype(F32)
    cidx = lax.broadcasted_iota(jnp.int32, (_PEER_CAND_ROWS, ts), 0).astype(F32)
    tidx = lax.broadcasted_iota(jnp.int32, (PEER_TOPK, ts), 0).astype(F32)
    bidx = lax.broadcasted_iota(jnp.int32, (SUBLANES, ts), 0).astype(F32)
    pad =jnp.full((_PEER_CAND_ROWS, ts), -jnp.inf, F32)

    def tables(h, s0, s1, exact):
        if exact:
            rank0, v0 = _extract_top(s0, kidx, PEER_TOPK, PEER_TOPK)
            rank1, v1 = _extract_top(s1, kidx, PEER_TOPK, PEER_TOPK)
        else:
            rank0, v0, n0 = _extract_by_value(s0, PEER_TOPK)
            rank1, v1, n1 = _extract_by_value(s1, PEER_TOPK)
        if exact:
            pairs = list(_PEER_CAND)
            cand = pad
            for ci, (a, b) in enumerate(pairs):
                cand = jnp.where(cidx == ci, v0[a] + v1[b], cand)
            crank, _ = _extract_top(cand, cidx, PEER_TOPK, PEER_TOPK)
            off = None
        else:
            col0 = jnp.zeros((PEER_TOPK, ts), F32)
            col1 = jnp.zeros((PEER_TOPK, ts), F32)
            low1 = jnp.zeros((SUBLANES, ts), F32)
            for r in range(PEER_TOPK):
                col0 = jnp.where(tidx == r, v0[r], col0)
                col1 = jnp.where(tidx == r, v1[r], col1)
                if r < SUBLANES:
                    low1 = jnp.where(bidx == r, v1[r], low1)
            ninf = np.float32(-np.inf)
            groups = [v0[0] + col1, jnp.where(tidx == 0, ninf, col0 + v1[0])]
            pairs = [(0, b) for b in range(PEER_TOPK)] + [(a, 0) if a else None for a in range(PEER_TOPK)]
            rest = [p for p in _PEER_CAND if p[0] and p[1]]
            for a in sorted({p[0] for p in rest}):
                mine = [p for p in rest if p[0] == a]
                if len(mine) > 2:
                    groups.append(jnp.where((bidx >= 1) & (bidx <= len(mine)), v0[a] + low1, ninf))
                    pairs += [(a, b) if 1 <= b <= len(mine) else None for b in range(SUBLANES)]
            loose = [p for p in rest if p not in pairs]
            assert len(loose) <= SUBLANES
            tail = jnp.full((SUBLANES, ts), ninf, F32)
            for r, (a, b) in enumerate(loose):
                tail = jnp.where(bidx == r, v0[a] + v1[b], tail)
            groups.append(tail)
            pairs += loose + [None] * (SUBLANES - len(loose))
            assert sorted(p for p in pairs if p) == sorted(_PEER_CAND)
            cand = jnp.concatenate(groups, axis=0)
            crank, _, nc = _extract_by_value(cand, PEER_TOPK)
            off = jnp.max(jnp.abs(n0 - PEER_TOPK) + jnp.abs(n1 - PEER_TOPK) + jnp.abs(nc - PEER_TOPK))
        picked = jnp.where(crank < PEER_TOPK, 1.0, 0.0)
        e = picked * jnp.exp(cand - (v0[0] + v1[0]))
        z = jnp.sum(e, axis=0, keepdims=True)
        lq = jnp.zeros((N_KEYS, ts), F32)
        for a in range(PEER_TOPK):
            cnt = jnp.zeros((1, ts), F32)
            for ci, p in enumerate(pairs):
                if p is not None and p[0] == a:
                    cnt = cnt + picked[ci:ci + 1]
            lq = jnp.where(rank0 == a, cnt, lq)
        rank_b = rank1.astype(BF16)
        qe_b = jnp.exp(s1 - v1[0]).astype(BF16)
        for k in range(N_KEYS // BF16_ROWS):
            rank_ref[h, k] = pltpu.bitcast(rank_b[k * BF16_ROWS:(k + 1) * BF16_ROWS], jnp.uint32)
            qe_ref[h, k] = pltpu.bitcast(qe_b[k * BF16_ROWS:(k + 1) * BF16_ROWS], jnp.uint32)
        lq_ref[h] = lq
        pe_ref[h] = jnp.exp(s0 - v0[0]) / z
        return off

    def head(h, carry):
        c0 = pl.multiple_of(h * 2 * PEER_DKH, 2 * PEER_DKH)
        q0 = q_ref[:, pl.ds(c0, PEER_DKH)].astype(BF16)
        q1 = q_ref[:, pl.ds(c0 + PEER_DKH, PEER_DKH)].astype(BF16)
        s0 = _dot_nt(keys_ref[h, 0], q0)
        s1 = _dot_nt(keys_ref[h, 1], q1)
        off = tables(h, s0, s1, exact=False)

        @pl.when(off > 0.5)
        def _():
            tables(h, s0, s1, exact=True)

        return carry

    lax.fori_loop(0, PEER_HEADS, head, 0)


def _peer_select(q, keys):
    n = q.shape[0]
    out = jax.ShapeDtypeStruct((PEER_HEADS, N_KEYS, n), F32)
    ospec = pl.BlockSpec((PEER_HEADS, N_KEYS, SEL_TILE), lambda i: (0, 0, i))
    packed = (PEER_HEADS, N_KEYS // BF16_ROWS, BF16_ROWS // 2)
    out_b = jax.ShapeDtypeStruct(packed + (n,), jnp.uint32)
    ospec_b = pl.BlockSpec(packed + (SEL_TILE,), lambda i: (0, 0, 0, i))
    return pl.pallas_call(
        _peer_select_body,
        grid=(n // SEL_TILE,),
        in_specs=[pl.BlockSpec((SEL_TILE, q.shape[1]), lambda i: (i, 0)), _full(keys.shape)],
        out_specs=[ospec_b, ospec_b, ospec, ospec],
        out_shape=[out_b, out_b, out, out],
        compiler_params=_params(("arbitrary",)),
        name="peer_select",
    )(q, keys)


def _peer_dense_body(ht_ref, u_ref, vt_ref, rank_ref, qe_ref, lq_ref, pe_ref, x_ref, *rest, prompt_tiles):
    *o_ref, acc_ref, a_ref, w_ref = rest
    step = pl.program_id(1)

    @pl.when(step == 0)
    def _():
        acc_ref[...] = jnp.zeros_like(acc_ref)

    @pl.when(step >= 0)
    def _():
        quarter = PEER_ROWS * N_KEYS // 4
        for qi in range(4):
            rows = slice(qi * quarter, (qi + 1) * quarter)
            a_ref[rows, :] = _dot(u_ref[rows, :], ht_ref[...])

    @pl.when(step >= 0)
    def _():
        sub = (N_KEYS // BF16_ROWS, BF16_ROWS, LANES)
        for il in range(PEER_ROWS):
            for ci in range(TOKEN_TILE // LANES):
                cs = slice(ci * LANES, (ci + 1) * LANES)
                gate = jnp.zeros(sub, BF16)
                for h in range(PEER_HEADS):
                    lq = jnp.broadcast_to(lq_ref[h, il:il + 1, cs], sub[1:]).astype(BF16)
                    pe = jnp.broadcast_to(pe_ref[h, il:il + 1, cs], sub[1:]).astype(BF16)
                    take = pltpu.bitcast(rank_ref[h, :, :, cs], BF16) < lq[None]
                    qe = pltpu.bitcast(qe_ref[h, :, :, cs], BF16)
                    gate = gate + jnp.where(take, qe, jnp.zeros(sub, BF16)) * pe[None]
                for k in range(sub[0]):
                    rs = slice(il * N_KEYS + k * BF16_ROWS, il * N_KEYS + (k + 1) * BF16_ROWS)
                    w_ref[rs, cs] = gate[k] * _gelu(a_ref[rs, cs]).astype(BF16)
        acc_ref[...] += _dot(vt_ref[0], w_ref[...])

    last = step == pl.num_programs(1) - 1
    if prompt_tiles is None:
        @pl.when(last)
        def _():
            o_ref[0][...] = x_ref[...] + acc_ref[...].T
    else:
        tile = pl.program_id(0)
        o_prompt, o_sample = o_ref

        @pl.when(last & (tile < prompt_tiles))
        def _():
            o_prompt[...] = x_ref[...] + acc_ref[...].T

        @pl.when(last & (tile >= prompt_tiles))
        def _():
            o_sample[...] = x_ref[...] + acc_ref[...].T


def _peer_dense(ht, u, vt, rank, qe, lq, pe, x, n_prompt=None):
    n = x.shape[0]
    ex = PEER_ROWS * N_KEYS
    steps = N_KEYS // PEER_ROWS
    tok = pl.BlockSpec((PEER_HEADS, N_KEYS // BF16_ROWS, BF16_ROWS // 2, TOKEN_TILE), lambda t, e: (0, 0, 0, t))
    rowsel = pl.BlockSpec((PEER_HEADS, PEER_ROWS, TOKEN_TILE), lambda t, e: (0, e, t))
    if n_prompt is None:
        tp = None
        out_specs = pl.BlockSpec((TOKEN_TILE, D_MODEL), lambda t, e: (t, 0))
        out_shape = jax.ShapeDtypeStruct((n, D_MODEL), F32)
    else:
        tp = n_prompt // TOKEN_TILE
        out_specs = [pl.BlockSpec((TOKEN_TILE, D_MODEL), lambda t, e: (jnp.minimum(t, tp - 1), 0)),
                     pl.BlockSpec((TOKEN_TILE, D_MODEL), lambda t, e: (jnp.maximum(t - tp, 0), 0))]
        out_shape = [jax.ShapeDtypeStruct((n_prompt, D_MODEL), F32),
                     jax.ShapeDtypeStruct((n - n_prompt, D_MODEL), F32)]
    return pl.pallas_call(
        functools.partial(_peer_dense_body, prompt_tiles=tp),
        grid=(n // TOKEN_TILE, steps),
        in_specs=[pl.BlockSpec((D_MODEL, TOKEN_TILE), lambda t, e: (0, t)),
                  pl.BlockSpec((ex, D_MODEL), lambda t, e: (e, 0)),
                  pl.BlockSpec((1, D_MODEL, ex), lambda t, e: (e, 0, 0)),
                  tok, tok, rowsel, rowsel,
                  pl.BlockSpec((TOKEN_TILE, D_MODEL), lambda t, e: (t, 0))],
        out_specs=out_specs,
        out_shape=out_shape,
        scratch_shapes=[pltpu.VMEM((D_MODEL, TOKEN_TILE), F32), pltpu.VMEM((ex, TOKEN_TILE), F32),
                        pltpu.VMEM((ex, TOKEN_TILE), BF16)],
        compiler_params=_params(("arbitrary", "arbitrary")),
        name="peer_dense",
    )(ht, u, vt, rank, qe, lq, pe, x)


def _peer(x1, ht, q, keys, u, vt, n_prompt=None):
    rank, qe, lq, pe = _peer_select(q, keys)
    return _peer_dense(ht, u, vt, rank, qe, lq, pe, x1, n_prompt)


def _value_blocks(v):
    ex = PEER_ROWS * N_KEYS
    return v.astype(BF16).reshape(v.shape[0] // ex, ex, v.shape[1]).transpose(0, 2, 1)


def _block_diag_mean(width):
    idx = np.arange(width) // HEAD_DIM
    return jnp.asarray((idx[:, None] == idx[None, :]).astype(np.float32) / HEAD_DIM)


def kernel(x_prompt, x_sample, cache_swa_k, cache_swa_v, state_hgrn, state_conv, norm_mix_g, norm_ffn_g, w_in0, attn_q_norm_g, attn_k_norm_g, attn_sinks, hgrn_lb, hgrn_o_norm_g, w_out0, conv_w_pw1, conv_b_pw1, conv_w_dw, conv_b_dw, conv_ln_g, conv_ln_b, conv_w_pw2, peer_w_q, peer_keys, peer_u, peer_v):
    batch, seq, _ = x_prompt.shape
    dec_batch, dec_seq, _ = x_sample.shape
    n_p = batch * seq
    n_s = dec_batch * dec_seq
    assert seq % CONV_ROWS == 0 and seq % HGRN_ROWS == 0 and n_p % TOKEN_TILE == 0 and n_s % TOKEN_TILE == 0
    assert dec_batch % DEC_GROUP == 0 and 8 % dec_seq == 0 and dec_seq <= CONV_W - 1

    x = (x_prompt.reshape(n_p, D_MODEL), x_sample.reshape(n_s, D_MODEL))
    row = lambda v: v.reshape(1, -1).astype(F32)

    z = _in0(x, row(norm_mix_g[0]), w_in0.astype(BF16))
    d = np.arange(LANES) % HEAD_DIM
    inv = ROPE_THETA ** (-jnp.arange(0, ROT_DIM, 2, dtype=F32) / ROT_DIM)
    inv_lane = jnp.where(jnp.asarray(d < ROT_DIM), inv[d % (ROT_DIM // 2)], 0.0).reshape(1, LANES)
    gq = row(jnp.tile(attn_q_norm_g, N_HEADS_A))
    gk = row(jnp.tile(attn_k_norm_g, N_KV_A))
    segq, segk = _block_diag_mean(QA), _block_diag_mean(KVA)
    sinks = attn_sinks.astype(F32)
    attn_p, k_win_p, v_win_p = _attn_prompt(z, batch, seq, gq, gk, inv_lane, segq, segk, sinks)
    kc = cache_swa_k.reshape(dec_batch * WINDOW, KVA)
    vc = cache_swa_v.reshape(dec_batch * WINDOW, KVA)
    attn_s, k_new, v_new = _attn_decode(z, n_p, dec_batch, dec_seq, kc, vc, gq, gk, inv_lane, segq, segk, sinks)

    go = row(hgrn_o_norm_g)
    tril = jnp.asarray(np.tril(np.ones((HGRN_CHUNK, HGRN_CHUNK), np.float32)), BF16)
    hg_p, s_p = _hgrn_prompt(z, batch, seq, hgrn_lb, go, tril)
    r = np.arange(LANES)
    tril_dec = jnp.asarray(((r[:, None] // dec_seq == r[None, :] // dec_seq) & (r[None, :] <= r[:, None]))
                           .astype(np.float32))
    hg_s, s_s = _hgrn_decode(z, n_p, dec_batch, dec_seq, state_hgrn, hgrn_lb, go, tril_dec)

    x1, ht, q = _mix_out(_out0_body, [(attn_p, attn_s), (hg_p, hg_s)], x, [w_out0.astype(BF16)], row(norm_ffn_g[0]),
                         peer_w_q[0].astype(BF16), "mix0_out")
    x2 = _peer(x1, ht, q, peer_keys[0].astype(BF16), peer_u[0].astype(BF16), _value_blocks(peer_v[0]))

    u = _glu(x2, row(norm_mix_g[1]), conv_w_pw1.astype(BF16), row(conv_b_pw1))
    wdw = jnp.concatenate([conv_w_dw, jnp.zeros((1, D_MODEL), F32)], axis=0)
    bdw = row(conv_b_dw)
    c_p = _conv_prompt(u, batch, seq, wdw, bdw)
    wshift = jnp.stack([jnp.concatenate([jnp.zeros((t, D_MODEL), F32), conv_w_dw[:CONV_W - 1 - t]], axis=0)
                        for t in range(dec_seq)])
    c_s = _conv_decode(u, n_p, dec_batch, dec_seq, state_conv, wdw, wshift, bdw)
    x3, ht, q = _mix_out(_post1_body, [(c_p, c_s)], x2, [row(conv_ln_g), row(conv_ln_b), conv_w_pw2.astype(BF16)],
                         row(norm_ffn_g[1]), peer_w_q[1].astype(BF16), "mix1_out")
    y_p, y_s = _peer(x3, ht, q, peer_keys[1].astype(BF16), peer_u[1].astype(BF16),
                     _value_blocks(peer_v[1]), n_p)

    kv = lambda t: t.reshape(t.shape[0], WINDOW, N_KV_A, HEAD_DIM)
    k_win_s = jnp.concatenate([cache_swa_k[:, dec_seq:], k_new.reshape(dec_batch, dec_seq, N_KV_A, HEAD_DIM)], axis=1)
    v_win_s = jnp.concatenate([cache_swa_v[:, dec_seq:], v_new.reshape(dec_batch, dec_seq, N_KV_A, HEAD_DIM)], axis=1)
    u_p = u[:n_p].reshape(batch, seq, D_MODEL)
    u_s = u[n_p:].reshape(dec_batch, dec_seq, D_MODEL)
    conv_buf_p = u_p[:, seq - (CONV_W - 1):]
    conv_buf_s = jnp.concatenate([state_conv[:, dec_seq:], u_s], axis=1)
    return (y_p.reshape(batch, seq, D_MODEL), y_s.reshape(dec_batch, dec_seq, D_MODEL),
            kv(k_win_p), kv(v_win_p), s_p, conv_buf_p, k_win_s, v_win_s, s_s, conv_buf_s)
```

```python
import functools

import numpy as np
import jax
import jax.numpy as jnp
from jax import lax
from jax.experimental import pallas as pl
from jax.experimental.pallas import tpu as pltpu

F32 = jnp.float32
BF16 = jnp.bfloat16
HI = lax.Precision.HIGHEST

D_MODEL = 1024
PAST_LEN = 8192
HEAD_DIM = 64
N_HEADS_A = 8
N_KV_A = 2
KV_REP = N_HEADS_A // N_KV_A
WINDOW = 128
ROT_DIM = HEAD_DIM // 4
ROPE_THETA = 500000.0
ATTN_SCALE = HEAD_DIM ** -0.5
NEG_INF = -1e30
N_HEADS_B = 4
DK_B = 128
DV_B = 128
CONV_W = 31
N_KEYS = 128
PEER_HEADS = 8
PEER_TOPK = 16
PEER_DKH = 128
NORM_EPS = 1e-6

QA = N_HEADS_A * HEAD_DIM
KVA = N_KV_A * HEAD_DIM
QB = N_HEADS_B * DK_B
VB = N_HEADS_B * DV_B
IN0_WIDTH = QA + 2 * KVA + 2 * QB + 2 * VB
COL_K = QA // 128
COL_V = (QA + KVA) // 128
COL_QB = (QA + 2 * KVA) // 128
COL_FB = COL_QB + QB // 128
COL_IB = COL_FB + QB // 128
COL_GB = COL_IB + VB // 128

LANES = 128
SUBLANES = 8
TOKEN_TILE = 512
SEL_TILE = 256
HGRN_CHUNK = 64
HGRN_SUB = 16
HGRN_ROWS = 256
HGRN_HEADS_PER_STEP = 2
CONV_ROWS = 512
CONV_HALO = 32
CONV_CHUNK = 128
PEER_ROWS = 16
BF16_ROWS = 16
DEC_GROUP = 8
VMEM_LIMIT = 48 * 1024 * 1024


def _dot(a, b, prec=None):
    return jnp.dot(a, b, preferred_element_type=F32, precision=prec)


def _dot_nt(a, b, prec=None):
    return lax.dot_general(a, b, (((1,), (1,)), ((), ())), preferred_element_type=F32, precision=prec)


def _rms(x, g):
    return x * lax.rsqrt(jnp.mean(x * x, axis=-1, keepdims=True) + NORM_EPS) * g


def _silu(x):
    return x * jax.nn.sigmoid(x)


def _gelu(x):
    return 0.5 * x * (1.0 + lax.erf(x * np.float32(0.7071067811865476)))


def _params(sem, flags=None):
    return pltpu.CompilerParams(dimension_semantics=sem, vmem_limit_bytes=VMEM_LIMIT, flags=flags)


def _full(shape):
    n = len(shape)
    return pl.BlockSpec(shape, lambda *_: (0,) * n)


def _row_inputs(arrs):
    specs, ops, layout = [], [], []
    for a in arrs:
        if isinstance(a, tuple):
            tp = a[0].shape[0] // TOKEN_TILE
            width = a[0].shape[1]
            specs += [pl.BlockSpec((TOKEN_TILE, width), lambda i, tp=tp: (jnp.minimum(i, tp - 1), 0)),
                      pl.BlockSpec((TOKEN_TILE, width), lambda i, tp=tp: (jnp.maximum(i - tp, 0), 0))]
            ops += list(a)
            layout.append(tp)
        else:
            specs.append(pl.BlockSpec((TOKEN_TILE, a.shape[1]), lambda i: (i, 0)))
            ops.append(a)
            layout.append(None)
    return specs, ops, tuple(layout)


def _row_values(layout, refs):
    i = pl.program_id(0)
    vals, k = [], 0
    for tp in layout:
        if tp is None:
            vals.append(refs[k][...])
            k += 1
        else:
            vals.append(jnp.where(i < tp, refs[k][...], refs[k + 1][...]))
            k += 2
    return vals, refs[k:]


def _rows_of(a):
    return a[0].shape[0] + a[1].shape[0] if isinstance(a, tuple) else a.shape[0]


def _in0_body(*refs, layout):
    (x,), (g_ref, w_ref, z_ref) = _row_values(layout, refs)
    h = _rms(x, g_ref[...]).astype(BF16)
    z_ref[...] = _dot(h, w_ref[...])


def _in0(x, g, w):
    n = _rows_of(x)
    width = w.shape[1]
    specs, ops, layout = _row_inputs([x])
    return pl.pallas_call(
        functools.partial(_in0_body, layout=layout),
        grid=(n // TOKEN_TILE,),
        in_specs=specs + [_full((1, D_MODEL)), _full((D_MODEL, width))],
        out_specs=pl.BlockSpec((TOKEN_TILE, width), lambda i: (i, 0)),
        out_shape=jax.ShapeDtypeStruct((n, width), F32),
        compiler_params=_params(("arbitrary",)),
        name="in0_proj",
    )(*ops, g, w)


def _glu_body(x_ref, g_ref, w_ref, b_ref, u_ref):
    h = _rms(x_ref[...], g_ref[...]).astype(BF16)
    a = _dot(h, w_ref[...]) + b_ref[...]
    u_ref[...] = a[:, :D_MODEL] * jax.nn.sigmoid(a[:, D_MODEL:])


def _glu(x, g, w, b):
    n = x.shape[0]
    return pl.pallas_call(
        _glu_body,
        grid=(n // TOKEN_TILE,),
        in_specs=[pl.BlockSpec((TOKEN_TILE, D_MODEL), lambda i: (i, 0)), _full((1, D_MODEL)),
                  _full((D_MODEL, 2 * D_MODEL)), _full((1, 2 * D_MODEL))],
        out_specs=pl.BlockSpec((TOKEN_TILE, D_MODEL), lambda i: (i, 0)),
        out_shape=jax.ShapeDtypeStruct((n, D_MODEL), F32),
        compiler_params=_params(("arbitrary",)),
        name="conv_glu",
    )(x, g, w, b)


def _ffn_query(x1, gf_ref, wq_ref, x1_ref, ht_ref, q_ref):
    x1_ref[...] = x1
    h2 = _rms(x1, gf_ref[...])
    ht_ref[...] = h2.T.astype(BF16)
    q_ref[...] = _dot(h2.astype(BF16), wq_ref[...])


def _out0_body(*refs, layout):
    (attn, hg, x), (w_ref, gf_ref, wq_ref, x1_ref, ht_ref, q_ref) = _row_values(layout, refs)
    m = _dot(attn.astype(BF16), w_ref[:QA, :]) + _dot(hg.astype(BF16), w_ref[QA:, :])
    _ffn_query(x + m, gf_ref, wq_ref, x1_ref, ht_ref, q_ref)


def _post1_body(*refs, layout):
    (c, x), (lg_ref, lb_ref, w_ref, gf_ref, wq_ref, x1_ref, ht_ref, q_ref) = _row_values(layout, refs)
    mu = jnp.mean(c, axis=-1, keepdims=True)
    d = c - mu
    var = jnp.mean(d * d, axis=-1, keepdims=True)
    ln = d * lax.rsqrt(var + NORM_EPS) * lg_ref[...] + lb_ref[...]
    y = _dot(_silu(ln).astype(BF16), w_ref[...])
    _ffn_query(x + y, gf_ref, wq_ref, x1_ref, ht_ref, q_ref)


def _mix_out(body, acts, x, consts, gf, wq, name):
    n = _rows_of(x)
    qw = wq.shape[1]
    row = lambda w: pl.BlockSpec((TOKEN_TILE, w), lambda i: (i, 0))
    specs, ops, layout = _row_inputs(list(acts) + [x])
    return pl.pallas_call(
        functools.partial(body, layout=layout),
        grid=(n // TOKEN_TILE,),
        in_specs=specs + [_full(c.shape) for c in consts] + [_full(gf.shape), _full(wq.shape)],
        out_specs=[row(D_MODEL), pl.BlockSpec((D_MODEL, TOKEN_TILE), lambda i: (0, i)), row(qw)],
        out_shape=[jax.ShapeDtypeStruct((n, D_MODEL), F32), jax.ShapeDtypeStruct((D_MODEL, n), BF16),
                   jax.ShapeDtypeStruct((n, qw), F32)],
        compiler_params=_params(("arbitrary",)),
        name=name,
    )(*ops, *consts, gf, wq)


def _head_norm(x, g, seg):
    xx = x * x
    hi = xx.astype(BF16)
    rest = xx - hi.astype(F32)
    mid = rest.astype(BF16)
    ms = _dot(hi, seg) + _dot(mid, seg) + _dot((rest - mid.astype(F32)).astype(BF16), seg)
    return x * lax.rsqrt(ms + NORM_EPS) * g


def _rope(x, cos, sin, first_half):
    half = ROT_DIM // 2
    width = x.shape[1]
    up = pltpu.roll(x, width - half, axis=1)
    dn = pltpu.roll(x, half, axis=1)
    return x * cos + jnp.where(first_half, -up, dn) * sin


def _rope_tables(c, s, reps):
    out = []
    for n in (1, reps):
        cn, sn = (jnp.concatenate([t] * n, axis=1) if n > 1 else t for t in (c, s))
        lane = lax.broadcasted_iota(jnp.int32, cn.shape, 1)
        out.append((cn, sn, (lane % HEAD_DIM) < (ROT_DIM // 2)))
    return out


def _stack_heads(x, g):
    return jnp.concatenate(
        [x[:, (g * KV_REP + r) * HEAD_DIM:(g * KV_REP + r + 1) * HEAD_DIM] for r in range(KV_REP)], axis=0)


def _attn_prompt_body(q_ref, k_ref, v_ref, gq_ref, gk_ref, inv_ref, segq_ref, segk_ref, sink_ref,
                      o_ref, kw_ref, vw_ref, kprev_ref, vprev_ref, cos_ref, sin_ref):
    j = pl.program_id(1)
    w = WINDOW

    @pl.when(j == 0)
    def _():
        kprev_ref[...] = jnp.zeros_like(kprev_ref)
        vprev_ref[...] = jnp.zeros_like(vprev_ref)
        within = lax.broadcasted_iota(jnp.int32, (w, LANES), 0).astype(F32) * inv_ref[...]
        cos_ref[...] = jnp.cos(within)
        sin_ref[...] = jnp.sin(within)

    start = (j * w).astype(F32) * inv_ref[...]
    cs, ss = jnp.cos(start), jnp.sin(start)
    cw, sw = cos_ref[...], sin_ref[...]
    (ck, sk, fk), (cq, sq, fq) = _rope_tables(cs * cw - ss * sw, ss * cw + cs * sw, QA // LANES)
    q = _rope(_head_norm(q_ref[...], gq_ref[...], segq_ref[...]), cq, sq, fq)
    k = _rope(_head_norm(k_ref[...], gk_ref[...], segk_ref[...]), ck, sk, fk)
    v = v_ref[...]
    kp = kprev_ref[...]
    vp = vprev_ref[...]

    qi = lax.broadcasted_iota(jnp.int32, (KV_REP * w, w), 0) % w
    ki = lax.broadcasted_iota(jnp.int32, (KV_REP * w, w), 1)
    m_own = ki <= qi
    m_prev = (ki > qi) & (j > 0)
    rep = lax.broadcasted_iota(jnp.int32, (KV_REP * w, 1), 0) // w
    for g in range(N_KV_A):
        sl = slice(g * HEAD_DIM, (g + 1) * HEAD_DIM)
        qg = _stack_heads(q, g).astype(BF16)
        s_own = jnp.where(m_own, _dot_nt(qg, k[:, sl].astype(BF16)) * ATTN_SCALE, NEG_INF)
        s_prev = jnp.where(m_prev, _dot_nt(qg, kp[:, sl].astype(BF16)) * ATTN_SCALE, NEG_INF)
        sink = jnp.zeros((KV_REP * w, 1), F32)
        for r in range(KV_REP):
            sink = jnp.where(rep == r, sink_ref[g * KV_REP + r], sink)
        mx = jnp.maximum(jnp.maximum(jnp.max(s_own, axis=-1, keepdims=True),
                                     jnp.max(s_prev, axis=-1, keepdims=True)), sink)
        e_own = jnp.exp(s_own - mx)
        e_prev = jnp.exp(s_prev - mx)
        den = (jnp.sum(e_own, axis=-1, keepdims=True) + jnp.sum(e_prev, axis=-1, keepdims=True)
               + jnp.exp(sink - mx))
        o = (_dot(e_own.astype(BF16), v[:, sl].astype(BF16))
             + _dot(e_prev.astype(BF16), vp[:, sl].astype(BF16))) / den
        for r in range(KV_REP):
            hq = g * KV_REP + r
            o_ref[:, hq * HEAD_DIM:(hq + 1) * HEAD_DIM] = o[r * w:(r + 1) * w]

    kprev_ref[...] = k
    vprev_ref[...] = v

    @pl.when(j == pl.num_programs(1) - 1)
    def _():
        kw_ref[0] = k
        vw_ref[0] = v


def _attn_prompt(z, batch, seq, gq, gk, inv, segq, segk, sinks):
    nb = seq // WINDOW
    blk = lambda width, col: pl.BlockSpec((WINDOW, width), lambda b, j: (b * nb + j, col))
    return pl.pallas_call(
        _attn_prompt_body,
        grid=(batch, nb),
        in_specs=[blk(QA, 0), blk(KVA, COL_K), blk(KVA, COL_V), _full(gq.shape), _full(gk.shape),
                  _full(inv.shape), _full(segq.shape), _full(segk.shape),
                  pl.BlockSpec(memory_space=pltpu.SMEM)],
        out_specs=[pl.BlockSpec((WINDOW, QA), lambda b, j: (b * nb + j, 0)),
                   pl.BlockSpec((1, WINDOW, KVA), lambda b, j: (b, 0, 0)),
                   pl.BlockSpec((1, WINDOW, KVA), lambda b, j: (b, 0, 0))],
        out_shape=[jax.ShapeDtypeStruct((batch * seq, QA), F32),
                   jax.ShapeDtypeStruct((batch, WINDOW, KVA), F32),
                   jax.ShapeDtypeStruct((batch, WINDOW, KVA), F32)],
        scratch_shapes=[pltpu.VMEM((WINDOW, KVA), F32), pltpu.VMEM((WINDOW, KVA), F32),
                        pltpu.VMEM((WINDOW, LANES), F32), pltpu.VMEM((WINDOW, LANES), F32)],
        compiler_params=_params(("arbitrary", "arbitrary")),
        name="swa_prompt",
    )(z, z, z, gq, gk, inv, segq, segk, sinks)


def _attn_decode_body(q_ref, k_ref, v_ref, kc_ref, vc_ref, gq_ref, gk_ref, inv_ref, segq_ref, segk_ref,
                      sink_ref, o_ref, kn_ref, vn_ref, *, dec_seq):
    rows_n = DEC_GROUP * dec_seq
    rows = lax.broadcasted_iota(jnp.int32, (rows_n, LANES), 0)
    pos = (PAST_LEN + rows % dec_seq).astype(F32)
    ang = pos * inv_ref[...]
    (ck, sk, fk), (cq, sq, fq) = _rope_tables(jnp.cos(ang), jnp.sin(ang), QA // LANES)
    q = _rope(_head_norm(q_ref[...], gq_ref[...], segq_ref[...]), cq, sq, fq)
    k = _rope(_head_norm(k_ref[...], gk_ref[...], segk_ref[...]), ck, sk, fk)
    v = v_ref[...]
    kn_ref[...] = k
    vn_ref[...] = v

    nq = KV_REP * rows_n
    nc = DEC_GROUP * WINDOW

    def qrow(shape):
        r = lax.broadcasted_iota(jnp.int32, shape, 0) % rows_n
        return r // dec_seq, r % dec_seq

    bq, tq = qrow((nq, nc))
    col = lax.broadcasted_iota(jnp.int32, (nq, nc), 1)
    m_cache = (col // WINDOW == bq) & (col % WINDOW > tq)
    bq, tq = qrow((nq, rows_n))
    col = lax.broadcasted_iota(jnp.int32, (nq, rows_n), 1)
    m_new = (col // dec_seq == bq) & (col % dec_seq <= tq)
    rep = lax.broadcasted_iota(jnp.int32, (nq, 1), 0) // rows_n
    for g in range(N_KV_A):
        sl = slice(g * HEAD_DIM, (g + 1) * HEAD_DIM)
        qg = _stack_heads(q, g).astype(BF16)
        s_c = jnp.where(m_cache, _dot_nt(qg, kc_ref[:, sl].astype(BF16)) * ATTN_SCALE, NEG_INF)
        s_n = jnp.where(m_new, _dot_nt(qg, k[:, sl].astype(BF16)) * ATTN_SCALE, NEG_INF)
        sink = jnp.zeros((nq, 1), F32)
        for r in range(KV_REP):
            sink = jnp.where(rep == r, sink_ref[g * KV_REP + r], sink)
        mx = jnp.maximum(jnp.maximum(jnp.max(s_c, axis=-1, keepdims=True),
                                     jnp.max(s_n, axis=-1, keepdims=True)), sink)
        e_c = jnp.exp(s_c - mx)
        e_n = jnp.exp(s_n - mx)
        den = jnp.sum(e_c, axis=-1, keepdims=True) + jnp.sum(e_n, axis=-1, keepdims=True) + jnp.exp(sink - mx)
        o = (_dot(e_c.astype(BF16), vc_ref[:, sl].astype(BF16))
             + _dot(e_n.astype(BF16), v[:, sl].astype(BF16))) / den
        for r in range(KV_REP):
            hq = g * KV_REP + r
            o_ref[:, hq * HEAD_DIM:(hq + 1) * HEAD_DIM] = o[r * rows_n:(r + 1) * rows_n]


def _attn_decode(z, row0, dec_batch, dec_seq, kc, vc, gq, gk, inv, segq, segk, sinks):
    rows_n = DEC_GROUP * dec_seq
    r0 = row0 // rows_n
    blk = lambda width, col: pl.BlockSpec((rows_n, width), lambda i: (r0 + i, col))
    cache = pl.BlockSpec((DEC_GROUP * WINDOW, KVA), lambda i: (i, 0))
    n = dec_batch * dec_seq
    return pl.pallas_call(
        functools.partial(_attn_decode_body, dec_seq=dec_seq),
        grid=(dec_batch // DEC_GROUP,),
        in_specs=[blk(QA, 0), blk(KVA, COL_K), blk(KVA, COL_V), cache, cache, _full(gq.shape),
                  _full(gk.shape), _full(inv.shape), _full(segq.shape), _full(segk.shape),
                  pl.BlockSpec(memory_space=pltpu.SMEM)],
        out_specs=[pl.BlockSpec((rows_n, QA), lambda i: (i, 0)), pl.BlockSpec((rows_n, KVA), lambda i: (i, 0)),
                   pl.BlockSpec((rows_n, KVA), lambda i: (i, 0))],
        out_shape=[jax.ShapeDtypeStruct((n, QA), F32), jax.ShapeDtypeStruct((n, KVA), F32),
                   jax.ShapeDtypeStruct((n, KVA), F32)],
        compiler_params=_params(("arbitrary",)),
        name="swa_decode",
    )(z, z, z, kc, vc, gq, gk, inv, segq, segk, sinks)


def _hgrn_lower_bound(lb_ref):
    l = lb_ref[...]
    e = jnp.exp(l - jnp.max(l, axis=0, keepdims=True))
    return e[0:1] / jnp.sum(e, axis=0, keepdims=True)


def _hgrn_gates(qb, fb, lb):
    logf = jnp.log(lb + (1.0 - lb) * jax.nn.sigmoid(fb))
    kb = (1.0 - lb) * jax.nn.sigmoid(-fb)
    return _silu(qb), kb, logf


def _hgrn_out(o, g, gate):
    return _rms(o, g) * _silu(gate)


def _hgrn_chunk(qh, kb, ih, logf, st, tril):
    c = qh.shape[0]
    hi = logf.astype(BF16)
    rest = logf - hi.astype(F32)
    mid = rest.astype(BF16)
    b = _dot(tril, hi) + _dot(tril, mid) + _dot(tril, (rest - mid.astype(F32)).astype(BF16))
    ones = jnp.ones((DK_B, LANES), BF16)
    st_b = st.astype(BF16)
    ih_b = ih.astype(BF16)
    o_parts = []
    row = lax.broadcasted_iota(jnp.int32, (c, 1), 0)
    trow = lax.broadcasted_iota(jnp.int32, (HGRN_SUB, 1), 0)
    for blk in range(c // HGRN_SUB):
        t0 = blk * HGRN_SUB
        bi = b[t0:t0 + HGRN_SUB]
        qi = qh[t0:t0 + HGRN_SUB]
        if blk == 0:
            oi = _dot_nt((qi * jnp.exp(bi)).astype(BF16), st_b)
        else:
            base = b[t0 - 1:t0]
            qd = qi * jnp.exp(bi - base)
            kd = jnp.where(row < t0, kb * jnp.exp(jnp.minimum(base - b, 0.0)), 0.0)
            a_off = _dot_nt(qd.astype(BF16), kd.astype(BF16))
            oi = _dot_nt((qd * jnp.exp(base)).astype(BF16), st_b) + _dot(a_off.astype(BF16), ih_b)
        prods = []
        for s in range(HGRN_SUB):
            e = jnp.exp(jnp.where(trow >= s, bi - bi[s:s + 1], 0.0))
            prods.append(qi * e * kb[t0 + s:t0 + s + 1])
        a_diag = _dot(jnp.concatenate(prods, axis=0).astype(BF16), ones)
        for s in range(HGRN_SUB):
            a = a_diag[s * HGRN_SUB:(s + 1) * HGRN_SUB]
            oi = oi + jnp.where(trow >= s, a, 0.0) * ih[t0 + s:t0 + s + 1]
        o_parts.append(oi)
    last = b[c - 1:c]
    st_new = st * jnp.exp(last) + _dot(ih.T.astype(BF16), (kb * jnp.exp(last - b)).astype(BF16))
    return jnp.concatenate(o_parts, axis=0), st_new


def _hgrn_prompt_body(qb_ref, fb_ref, ib_ref, gb_ref, lb_ref, go_ref, tril_ref, o_ref, s_ref, st_ref):
    tb = pl.program_id(2)

    @pl.when(tb == 0)
    def _():
        st_ref[...] = jnp.zeros_like(st_ref)

    tril = tril_ref[...]
    heads = [slice(h * DK_B, (h + 1) * DK_B) for h in range(HGRN_HEADS_PER_STEP)]
    lbs = [_hgrn_lower_bound(lb_ref.at[:, hs]) for hs in heads]
    sts = [st_ref[h] for h in range(HGRN_HEADS_PER_STEP)]
    for c in range(HGRN_ROWS // HGRN_CHUNK):
        rs = slice(c * HGRN_CHUNK, (c + 1) * HGRN_CHUNK)
        for h, hs in enumerate(heads):
            qh, kb, logf = _hgrn_gates(qb_ref[rs, hs], fb_ref[rs, hs], lbs[h])
            o, sts[h] = _hgrn_chunk(qh, kb, ib_ref[rs, hs], logf, sts[h], tril)
            o_ref[rs, hs] = _hgrn_out(o, go_ref[...], gb_ref[rs, hs])
    for h in range(HGRN_HEADS_PER_STEP):
        st_ref[h] = sts[h]

    @pl.when(tb == pl.num_programs(2) - 1)
    def _():
        for h in range(HGRN_HEADS_PER_STEP):
            s_ref[0, h] = sts[h].T


def _hgrn_prompt(z, batch, seq, lb, go, tril):
    nt = seq // HGRN_ROWS
    hp = HGRN_HEADS_PER_STEP
    assert N_HEADS_B % hp == 0 and all(c % hp == 0 for c in (COL_QB, COL_FB, COL_IB, COL_GB))
    blk = lambda col: pl.BlockSpec((HGRN_ROWS, hp * DK_B), lambda b, h, t: (b * nt + t, col // hp + h))
    return pl.pallas_call(
        _hgrn_prompt_body,
        grid=(batch, N_HEADS_B // hp, nt),
        in_specs=[blk(COL_QB), blk(COL_FB), blk(COL_IB), blk(COL_GB),
                  pl.BlockSpec((lb.shape[0], hp * DK_B), lambda b, h, t: (0, h)), _full(go.shape), _full(tril.shape)],
        out_specs=[pl.BlockSpec((HGRN_ROWS, hp * DV_B), lambda b, h, t: (b * nt + t, h)),
                   pl.BlockSpec((1, hp, DK_B, DV_B), lambda b, h, t: (b, h, 0, 0))],
        out_shape=[jax.ShapeDtypeStruct((batch * seq, VB), F32),
                   jax.ShapeDtypeStruct((batch, N_HEADS_B, DK_B, DV_B), F32)],
        scratch_shapes=[pltpu.VMEM((hp, DV_B, DK_B), F32)],
        compiler_params=_params(("arbitrary", "arbitrary", "arbitrary")),
        name="hgrn_prompt",
    )(z, z, z, z, lb, go, tril)


def _pad_rows(x, rows):
    return jnp.concatenate([x, jnp.zeros((rows - x.shape[0], x.shape[1]), x.dtype)], axis=0)


def _hgrn_decode_body(z_ref, s0_ref, lb_ref, go_ref, tril_ref, o_ref, s_ref, *, dec_seq):
    rows_n = z_ref.shape[0]
    groups = rows_n // dec_seq
    row = lax.broadcasted_iota(jnp.int32, (rows_n, 1), 0)
    for h in range(N_HEADS_B):
        col = lambda c0: slice((c0 + h) * LANES, (c0 + h + 1) * LANES)
        hs = slice(h * DK_B, (h + 1) * DK_B)
        lb = _hgrn_lower_bound(lb_ref.at[:, hs])
        qh, kb, logf = _hgrn_gates(z_ref[:, col(COL_QB)], z_ref[:, col(COL_FB)], lb)
        ih = z_ref[:, col(COL_IB)]
        b = _dot(tril_ref[...], _pad_rows(logf, LANES), HI)[:rows_n]
        b_t = _pad_rows(b, LANES).T
        ih_pad = _pad_rows(ih, LANES)
        o = jnp.zeros((rows_n, DV_B), F32)
        for e in range(groups):
            mine = (row // dec_seq) == e
            s0 = s0_ref[e, h]
            o = o + _dot(jnp.where(mine, qh * jnp.exp(b), 0.0), s0, HI)
            for s in range(dec_seq):
                r = e * dec_seq + s
                live = mine & (row >= r)
                ex = jnp.exp(jnp.where(live, b - b[r:r + 1], 0.0))
                a = jnp.sum(qh * ex * kb[r:r + 1], axis=-1, keepdims=True)
                o = o + jnp.where(live, a, 0.0) * ih[r:r + 1]
            r_last = (e + 1) * dec_seq - 1
            last = b[r_last:r_last + 1]
            kd = jnp.where(mine, kb * jnp.exp(jnp.minimum(last - b, 0.0)), 0.0)
            s_ref[e, h] = s0 * jnp.exp(b_t[:, r_last:r_last + 1]) + _dot(_pad_rows(kd, LANES).T, ih_pad, HI)
        o_ref[:, hs] = _hgrn_out(o, go_ref[...], z_ref[:, col(COL_GB)])


def _hgrn_decode(z, row0, dec_batch, dec_seq, s0, lb, go, tril):
    groups = 8 // dec_seq
    rows_n = groups * dec_seq
    r0 = row0 // rows_n
    st = pl.BlockSpec((groups, N_HEADS_B, DK_B, DV_B), lambda i: (i, 0, 0, 0))
    return pl.pallas_call(
        functools.partial(_hgrn_decode_body, dec_seq=dec_seq),
        grid=(dec_batch // groups,),
        in_specs=[pl.BlockSpec((rows_n, z.shape[1]), lambda i: (r0 + i, 0)), st,
                  _full(lb.shape), _full(go.shape), _full(tril.shape)],
        out_specs=[pl.BlockSpec((rows_n, VB), lambda i: (i, 0)), st],
        out_shape=[jax.ShapeDtypeStruct((dec_batch * dec_seq, VB), F32),
                   jax.ShapeDtypeStruct(s0.shape, F32)],
        compiler_params=_params(("arbitrary",)),
        name="hgrn_decode",
    )(z, s0, lb, go, tril)


def _conv_prompt_body(cur_ref, prev_ref, w_ref, b_ref, c_ref, ext_ref, sh_ref):
    t = pl.program_id(1)
    ext_ref[:CONV_HALO, :] = jnp.where(t > 0, prev_ref[...], 0.0)
    ext_ref[CONV_HALO:, :] = cur_ref[...]
    lead = CONV_HALO - (CONV_W - 1)
    chunk = CONV_CHUNK

    def cols(ci, carry):
        cs = pl.ds(pl.multiple_of(ci * LANES, LANES), LANES)
        for r in range(CONV_ROWS // chunk):
            acc = jnp.zeros((chunk, LANES), F32) + b_ref[:, cs]
            for res in range(SUBLANES):
                taps = range(res, CONV_W, SUBLANES)
                span = chunk + taps[-1] - res
                sh_ref[:span, :] = ext_ref[pl.ds(r * chunk + lead + res, span), cs]
                for w in taps:
                    acc = acc + sh_ref[w - res:w - res + chunk, :] * w_ref[pl.ds(w, 1), cs]
            c_ref[pl.ds(r * chunk, chunk), cs] = acc
        return carry

    lax.fori_loop(0, D_MODEL // LANES, cols, 0)


def _conv_prompt(u, batch, seq, w, b):
    nt = seq // CONV_ROWS
    per = CONV_ROWS // CONV_HALO
    return pl.pallas_call(
        _conv_prompt_body,
        grid=(batch, nt),
        in_specs=[pl.BlockSpec((CONV_ROWS, D_MODEL), lambda bi, t: (bi * nt + t, 0)),
                  pl.BlockSpec((CONV_HALO, D_MODEL), lambda bi, t: (jnp.maximum((bi * nt + t) * per - 1, 0), 0)),
                  _full(w.shape), _full(b.shape)],
        out_specs=pl.BlockSpec((CONV_ROWS, D_MODEL), lambda bi, t: (bi * nt + t, 0)),
        out_shape=jax.ShapeDtypeStruct((batch * seq, D_MODEL), F32),
        scratch_shapes=[pltpu.VMEM((CONV_HALO + CONV_ROWS, D_MODEL), F32),
                        pltpu.VMEM((CONV_CHUNK + CONV_HALO, LANES), F32)],
        compiler_params=_params(("arbitrary", "arbitrary")),
        name="conv_prompt",
    )(u, u, w, b)


def _conv_decode_body(u_ref, st_ref, w_ref, wshift_ref, b_ref, c_ref, *, dec_seq):
    for e in range(DEC_GROUP):
        past = st_ref[e]
        for t in range(dec_seq):
            acc = jnp.sum(past * wshift_ref[t], axis=0, keepdims=True) + b_ref[...]
            for t2 in range(t + 1):
                wi = CONV_W - 1 - t + t2
                acc = acc + u_ref[e * dec_seq + t2:e * dec_seq + t2 + 1, :] * w_ref[wi:wi + 1, :]
            c_ref[e * dec_seq + t:e * dec_seq + t + 1, :] = acc


def _conv_decode(u, row0, dec_batch, dec_seq, state, w, wshift, b):
    rows_n = DEC_GROUP * dec_seq
    r0 = row0 // rows_n
    return pl.pallas_call(
        functools.partial(_conv_decode_body, dec_seq=dec_seq),
        grid=(dec_batch // DEC_GROUP,),
        in_specs=[pl.BlockSpec((rows_n, D_MODEL), lambda i: (r0 + i, 0)),
                  pl.BlockSpec((DEC_GROUP, CONV_W - 1, D_MODEL), lambda i: (i, 0, 0)),
                  _full(w.shape), _full(wshift.shape), _full(b.shape)],
        out_specs=pl.BlockSpec((rows_n, D_MODEL), lambda i: (i, 0)),
        out_shape=jax.ShapeDtypeStruct((dec_batch * dec_seq, D_MODEL), F32),
        compiler_params=_params(("arbitrary",)),
        name="conv_decode",
    )(u, state, w, wshift, b)


def _extract_top(s, idx, count, none_rank):
    work = s
    rank = jnp.full(s.shape, none_rank, F32)
    vals = []
    for r in range(count):
        m = jnp.max(work, axis=0, keepdims=True)
        first = jnp.min(jnp.where(work == m, idx, np.float32(1e9)), axis=0, keepdims=True)
        hit = idx == first
        rank = jnp.where(hit, np.float32(r), rank)
        work = jnp.where(hit, -jnp.inf, work)
        vals.append(m)
    return rank, vals


_PEER_CAND = [(a, b) for a in range(PEER_TOPK) for b in range(PEER_TOPK) if (a + 1) * (b + 1) <= PEER_TOPK]
_PEER_CAND_ROWS = -(-len(_PEER_CAND) // 8) * 8


def _extract_by_value(s, count):
    work = s
    rank = jnp.full(s.shape, np.float32(count), F32)
    vals = []
    for r in range(count):
        m = jnp.max(work, axis=0, keepdims=True)
        hit = work == m
        rank = jnp.where(hit, np.float32(r), rank)
        work = jnp.where(hit, -jnp.inf, work)
        vals.append(m)
    ranked = jnp.sum(jnp.where(rank < count, 1.0, 0.0), axis=0, keepdims=True)
    return rank, vals, ranked


def _peer_select_body(q_ref, keys_ref, rank_ref, qe_ref, lq_ref, pe_ref):
    ts = q_ref.shape[0]
    kidx = lax.broadcasted_iota(jnp.int32, (N_KEYS, ts), 0).astype(F32)
    cidx = lax.broadcasted_iota(jnp.int32, (_PEER_CAND_ROWS, ts), 0).astype(F32)
    tidx = lax.broadcasted_iota(jnp.int32, (PEER_TOPK, ts), 0).astype(F32)
    bidx = lax.broadcasted_iota(jnp.int32, (SUBLANES, ts), 0).astype(F32)
    pad =jnp.full((_PEER_CAND_ROWS, ts), -jnp.inf, F32)

    def tables(h, s0, s1, exact):
        if exact:
            rank0, v0 = _extract_top(s0, kidx, PEER_TOPK, PEER_TOPK)
            rank1, v1 = _extract_top(s1, kidx, PEER_TOPK, PEER_TOPK)
        else:
            rank0, v0, n0 = _extract_by_value(s0, PEER_TOPK)
            rank1, v1, n1 = _extract_by_value(s1, PEER_TOPK)
        if exact:
            pairs = list(_PEER_CAND)
            cand = pad
            for ci, (a, b) in enumerate(pairs):
                cand = jnp.where(cidx == ci, v0[a] + v1[b], cand)
            crank, _ = _extract_top(cand, cidx, PEER_TOPK, PEER_TOPK)
            off = None
        else:
            col0 = jnp.zeros((PEER_TOPK, ts), F32)
            col1 = jnp.zeros((PEER_TOPK, ts), F32)
            low1 = jnp.zeros((SUBLANES, ts), F32)
            for r in range(PEER_TOPK):
                col0 = jnp.where(tidx == r, v0[r], col0)
                col1 = jnp.where(tidx == r, v1[r], col1)
                if r < SUBLANES:
                    low1 = jnp.where(bidx == r, v1[r], low1)
            ninf = np.float32(-np.inf)
            groups = [v0[0] + col1, jnp.where(tidx == 0, ninf, col0 + v1[0])]
            pairs = [(0, b) for b in range(PEER_TOPK)] + [(a, 0) if a else None for a in range(PEER_TOPK)]
            rest = [p for p in _PEER_CAND if p[0] and p[1]]
            for a in sorted({p[0] for p in rest}):
                mine = [p for p in rest if p[0] == a]
                if len(mine) > 2:
                    groups.append(jnp.where((bidx >= 1) & (bidx <= len(mine)), v0[a] + low1, ninf))
                    pairs += [(a, b) if 1 <= b <= len(mine) else None for b in range(SUBLANES)]
            loose = [p for p in rest if p not in pairs]
            assert len(loose) <= SUBLANES
            tail = jnp.full((SUBLANES, ts), ninf, F32)
            for r, (a, b) in enumerate(loose):
                tail = jnp.where(bidx == r, v0[a] + v1[b], tail)
            groups.append(tail)
            pairs += loose + [None] * (SUBLANES - len(loose))
            assert sorted(p for p in pairs if p) == sorted(_PEER_CAND)
            cand = jnp.concatenate(groups, axis=0)
            crank, _, nc = _extract_by_value(cand, PEER_TOPK)
            off = jnp.max(jnp.abs(n0 - PEER_TOPK) + jnp.abs(n1 - PEER_TOPK) + jnp.abs(nc - PEER_TOPK))
        picked = jnp.where(crank < PEER_TOPK, 1.0, 0.0)
        e = picked * jnp.exp(cand - (v0[0] + v1[0]))
        z = jnp.sum(e, axis=0, keepdims=True)
        lq = jnp.zeros((N_KEYS, ts), F32)
        for a in range(PEER_TOPK):
            cnt = jnp.zeros((1, ts), F32)
            for ci, p in enumerate(pairs):
                if p is not None and p[0] == a:
                    cnt = cnt + picked[ci:ci + 1]
            lq = jnp.where(rank0 == a, cnt, lq)
        rank_b = rank1.astype(BF16)
        qe_b = jnp.exp(s1 - v1[0]).astype(BF16)
        for k in range(N_KEYS // BF16_ROWS):
            rank_ref[h, k] = pltpu.bitcast(rank_b[k * BF16_ROWS:(k + 1) * BF16_ROWS], jnp.uint32)
            qe_ref[h, k] = pltpu.bitcast(qe_b[k * BF16_ROWS:(k + 1) * BF16_ROWS], jnp.uint32)
        lq_ref[h] = lq
        pe_ref[h] = jnp.exp(s0 - v0[0]) / z
        return off

    def head(h, carry):
        c0 = pl.multiple_of(h * 2 * PEER_DKH, 2 * PEER_DKH)
        q0 = q_ref[:, pl.ds(c0, PEER_DKH)].astype(BF16)
        q1 = q_ref[:, pl.ds(c0 + PEER_DKH, PEER_DKH)].astype(BF16)
        s0 = _dot_nt(keys_ref[h, 0], q0)
        s1 = _dot_nt(keys_ref[h, 1], q1)
        off = tables(h, s0, s1, exact=False)

        @pl.when(off > 0.5)
        def _():
            tables(h, s0, s1, exact=True)

        return carry

    lax.fori_loop(0, PEER_HEADS, head, 0)


def _peer_select(q, keys):
    n = q.shape[0]
    out = jax.ShapeDtypeStruct((PEER_HEADS, N_KEYS, n), F32)
    ospec = pl.BlockSpec((PEER_HEADS, N_KEYS, SEL_TILE), lambda i: (0, 0, i))
    packed = (PEER_HEADS, N_KEYS // BF16_ROWS, BF16_ROWS // 2)
    out_b = jax.ShapeDtypeStruct(packed + (n,), jnp.uint32)
    ospec_b = pl.BlockSpec(packed + (SEL_TILE,), lambda i: (0, 0, 0, i))
    return pl.pallas_call(
        _peer_select_body,
        grid=(n // SEL_TILE,),
        in_specs=[pl.BlockSpec((SEL_TILE, q.shape[1]), lambda i: (i, 0)), _full(keys.shape)],
        out_specs=[ospec_b, ospec_b, ospec, ospec],
        out_shape=[out_b, out_b, out, out],
        compiler_params=_params(("arbitrary",)),
        name="peer_select",
    )(q, keys)


def _peer_dense_body(ht_ref, u_ref, vt_ref, rank_ref, qe_ref, lq_ref, pe_ref, x_ref, *rest, prompt_tiles):
    *o_ref, acc_ref, a_ref, w_ref = rest
    step = pl.program_id(1)

    @pl.when(step == 0)
    def _():
        acc_ref[...] = jnp.zeros_like(acc_ref)

    @pl.when(pl.program_id(0) >= 0)
    def _():
        quarter = PEER_ROWS * N_KEYS // 4
        for qi in range(4):
            rows = slice(qi * quarter, (qi + 1) * quarter)
            a_ref[rows, :] = _dot(u_ref[rows, :], ht_ref[...])

    @pl.when(step >= 0)
    def _():
        sub = (N_KEYS // BF16_ROWS, BF16_ROWS, LANES)
        for il in range(PEER_ROWS):
            for ci in range(TOKEN_TILE // LANES):
                cs = slice(ci * LANES, (ci + 1) * LANES)
                gate = jnp.zeros(sub, BF16)
                for h in range(PEER_HEADS):
                    lq = jnp.broadcast_to(lq_ref[h, il:il + 1, cs], sub[1:]).astype(BF16)
                    pe = jnp.broadcast_to(pe_ref[h, il:il + 1, cs], sub[1:]).astype(BF16)
                    take = pltpu.bitcast(rank_ref[h, :, :, cs], BF16) < lq[None]
                    qe = pltpu.bitcast(qe_ref[h, :, :, cs], BF16)
                    gate = gate + jnp.where(take, qe, jnp.zeros(sub, BF16)) * pe[None]
                for k in range(sub[0]):
                    rs = slice(il * N_KEYS + k * BF16_ROWS, il * N_KEYS + (k + 1) * BF16_ROWS)
                    w_ref[rs, cs] = gate[k] * _gelu(a_ref[rs, cs]).astype(BF16)
        acc_ref[...] += _dot(vt_ref[0], w_ref[...])

    last = step == pl.num_programs(1) - 1
    if prompt_tiles is None:
        @pl.when(last)
        def _():
            o_ref[0][...] = x_ref[...] + acc_ref[...].T
    else:
        tile = pl.program_id(0)
        o_prompt, o_sample = o_ref

        @pl.when(last & (tile < prompt_tiles))
        def _():
            o_prompt[...] = x_ref[...] + acc_ref[...].T

        @pl.when(last & (tile >= prompt_tiles))
        def _():
            o_sample[...] = x_ref[...] + acc_ref[...].T


def _peer_dense(ht, u, vt, rank, qe, lq, pe, x, n_prompt=None):
    n = x.shape[0]
    ex = PEER_ROWS * N_KEYS
    steps = N_KEYS // PEER_ROWS
    tok = pl.BlockSpec((PEER_HEADS, N_KEYS // BF16_ROWS, BF16_ROWS // 2, TOKEN_TILE), lambda t, e: (0, 0, 0, t))
    rowsel = pl.BlockSpec((PEER_HEADS, PEER_ROWS, TOKEN_TILE), lambda t, e: (0, e, t))
    if n_prompt is None:
        tp = None
        out_specs = pl.BlockSpec((TOKEN_TILE, D_MODEL), lambda t, e: (t, 0))
        out_shape = jax.ShapeDtypeStruct((n, D_MODEL), F32)
    else:
        tp = n_prompt // TOKEN_TILE
        out_specs = [pl.BlockSpec((TOKEN_TILE, D_MODEL), lambda t, e: (jnp.minimum(t, tp - 1), 0)),
                     pl.BlockSpec((TOKEN_TILE, D_MODEL), lambda t, e: (jnp.maximum(t - tp, 0), 0))]
        out_shape = [jax.ShapeDtypeStruct((n_prompt, D_MODEL), F32),
                     jax.ShapeDtypeStruct((n - n_prompt, D_MODEL), F32)]
    return pl.pallas_call(
        functools.partial(_peer_dense_body, prompt_tiles=tp),
        grid=(n // TOKEN_TILE, steps),
        in_specs=[pl.BlockSpec((D_MODEL, TOKEN_TILE), lambda t, e: (0, t)),
                  pl.BlockSpec((ex, D_MODEL), lambda t, e: (e, 0)),
                  pl.BlockSpec((1, D_MODEL, ex), lambda t, e: (e, 0, 0)),
                  tok, tok, rowsel, rowsel,
                  pl.BlockSpec((TOKEN_TILE, D_MODEL), lambda t, e: (t, 0))],
        out_specs=out_specs,
        out_shape=out_shape,
        scratch_shapes=[pltpu.VMEM((D_MODEL, TOKEN_TILE), F32), pltpu.VMEM((ex, TOKEN_TILE), F32),
                        pltpu.VMEM((ex, TOKEN_TILE), BF16)],
        compiler_params=_params(("arbitrary", "arbitrary")),
        name="peer_dense",
    )(ht, u, vt, rank, qe, lq, pe, x)


def _peer(x1, ht, q, keys, u, vt, n_prompt=None):
    rank, qe, lq, pe = _peer_select(q, keys)
    return _peer_dense(ht, u, vt, rank, qe, lq, pe, x1, n_prompt)


def _value_blocks(v):
    ex = PEER_ROWS * N_KEYS
    return v.astype(BF16).reshape(v.shape[0] // ex, ex, v.shape[1]).transpose(0, 2, 1)


def _block_diag_mean(width):
    idx = np.arange(width) // HEAD_DIM
    return jnp.asarray((idx[:, None] == idx[None, :]).astype(np.float32) / HEAD_DIM, BF16)


def kernel(x_prompt, x_sample, cache_swa_k, cache_swa_v, state_hgrn, state_conv, norm_mix_g, norm_ffn_g, w_in0, attn_q_norm_g, attn_k_norm_g, attn_sinks, hgrn_lb, hgrn_o_norm_g, w_out0, conv_w_pw1, conv_b_pw1, conv_w_dw, conv_b_dw, conv_ln_g, conv_ln_b, conv_w_pw2, peer_w_q, peer_keys, peer_u, peer_v):
    batch, seq, _ = x_prompt.shape
    dec_batch, dec_seq, _ = x_sample.shape
    n_p = batch * seq
    n_s = dec_batch * dec_seq
    assert seq % CONV_ROWS == 0 and seq % HGRN_ROWS == 0 and n_p % TOKEN_TILE == 0 and n_s % TOKEN_TILE == 0
    assert dec_batch % DEC_GROUP == 0 and 8 % dec_seq == 0 and dec_seq <= CONV_W - 1

    x = (x_prompt.reshape(n_p, D_MODEL), x_sample.reshape(n_s, D_MODEL))
    row = lambda v: v.reshape(1, -1).astype(F32)

    z = _in0(x, row(norm_mix_g[0]), w_in0.astype(BF16))
    d = np.arange(LANES) % HEAD_DIM
    inv = ROPE_THETA ** (-jnp.arange(0, ROT_DIM, 2, dtype=F32) / ROT_DIM)
    inv_lane = jnp.where(jnp.asarray(d < ROT_DIM), inv[d % (ROT_DIM // 2)], 0.0).reshape(1, LANES)
    gq = row(jnp.tile(attn_q_norm_g, N_HEADS_A))
    gk = row(jnp.tile(attn_k_norm_g, N_KV_A))
    segq, segk = _block_diag_mean(QA), _block_diag_mean(KVA)
    sinks = attn_sinks.astype(F32)
    attn_p, k_win_p, v_win_p = _attn_prompt(z, batch, seq, gq, gk, inv_lane, segq, segk, sinks)
    kc = cache_swa_k.reshape(dec_batch * WINDOW, KVA)
    vc = cache_swa_v.reshape(dec_batch * WINDOW, KVA)
    attn_s, k_new, v_new = _attn_decode(z, n_p, dec_batch, dec_seq, kc, vc, gq, gk, inv_lane, segq, segk, sinks)

    go = row(hgrn_o_norm_g)
    tril = jnp.asarray(np.tril(np.ones((HGRN_CHUNK, HGRN_CHUNK), np.float32)), BF16)
    hg_p, s_p = _hgrn_prompt(z, batch, seq, hgrn_lb, go, tril)
    r = np.arange(LANES)
    tril_dec = jnp.asarray(((r[:, None] // dec_seq == r[None, :] // dec_seq) & (r[None, :] <= r[:, None]))
                           .astype(np.float32))
    hg_s, s_s = _hgrn_decode(z, n_p, dec_batch, dec_seq, state_hgrn, hgrn_lb, go, tril_dec)

    x1, ht, q = _mix_out(_out0_body, [(attn_p, attn_s), (hg_p, hg_s)], x, [w_out0.astype(BF16)], row(norm_ffn_g[0]),
                         peer_w_q[0].astype(BF16), "mix0_out")
    x2 = _peer(x1, ht, q, peer_keys[0].astype(BF16), peer_u[0].astype(BF16), _value_blocks(peer_v[0]))

    u = _glu(x2, row(norm_mix_g[1]), conv_w_pw1.astype(BF16), row(conv_b_pw1))
    wdw = jnp.concatenate([conv_w_dw, jnp.zeros((1, D_MODEL), F32)], axis=0)
    bdw = row(conv_b_dw)
    c_p = _conv_prompt(u, batch, seq, wdw, bdw)
    wshift = jnp.stack([jnp.concatenate([jnp.zeros((t, D_MODEL), F32), conv_w_dw[:CONV_W - 1 - t]], axis=0)
                        for t in range(dec_seq)])
    c_s = _conv_decode(u, n_p, dec_batch, dec_seq, state_conv, wdw, wshift, bdw)
    x3, ht, q = _mix_out(_post1_body, [(c_p, c_s)], x2, [row(conv_ln_g), row(conv_ln_b), conv_w_pw2.astype(BF16)],
                         row(norm_ffn_g[1]), peer_w_q[1].astype(BF16), "mix1_out")
    y_p, y_s = _peer(x3, ht, q, peer_keys[1].astype(BF16), peer_u[1].astype(BF16),
                     _value_blocks(peer_v[1]), n_p)

    kv = lambda t: t.reshape(t.shape[0], WINDOW, N_KV_A, HEAD_DIM)
    k_win_s = jnp.concatenate([cache_swa_k[:, dec_seq:], k_new.reshape(dec_batch, dec_seq, N_KV_A, HEAD_DIM)], axis=1)
    v_win_s = jnp.concatenate([cache_swa_v[:, dec_seq:], v_new.reshape(dec_batch, dec_seq, N_KV_A, HEAD_DIM)], axis=1)
    u_p = u[:n_p].reshape(batch, seq, D_MODEL)
    u_s = u[n_p:].reshape(dec_batch, dec_seq, D_MODEL)
    conv_buf_p = u_p[:, seq - (CONV_W - 1):]
    conv_buf_s = jnp.concatenate([state_conv[:, dec_seq:], u_s], axis=1)
    return (y_p.reshape(batch, seq, D_MODEL), y_s.reshape(dec_batch, dec_seq, D_MODEL),
            kv(k_win_p), kv(v_win_p), s_p, conv_buf_p, k_win_s, v_win_s, s_s, conv_buf_s)
```

```python
import functools

import numpy as np
import jax
import jax.numpy as jnp
from jax import lax
from jax.experimental import pallas as pl
from jax.experimental.pallas import tpu as pltpu

F32 = jnp.float32
BF16 = jnp.bfloat16
HI = lax.Precision.HIGHEST

D_MODEL = 1024
PAST_LEN = 8192
HEAD_DIM = 64
N_HEADS_A = 8
N_KV_A = 2
KV_REP = N_HEADS_A // N_KV_A
WINDOW = 128
ROT_DIM = HEAD_DIM // 4
ROPE_THETA = 500000.0
ATTN_SCALE = HEAD_DIM ** -0.5
NEG_INF = -1e30
N_HEADS_B = 4
DK_B = 128
DV_B = 128
CONV_W = 31
N_KEYS = 128
PEER_HEADS = 8
PEER_TOPK = 16
PEER_DKH = 128
NORM_EPS = 1e-6

QA = N_HEADS_A * HEAD_DIM
KVA = N_KV_A * HEAD_DIM
QB = N_HEADS_B * DK_B
VB = N_HEADS_B * DV_B
IN0_WIDTH = QA + 2 * KVA + 2 * QB + 2 * VB
COL_K = QA // 128
COL_V = (QA + KVA) // 128
COL_QB = (QA + 2 * KVA) // 128
COL_FB = COL_QB + QB // 128
COL_IB = COL_FB + QB // 128
COL_GB = COL_IB + VB // 128

LANES = 128
SUBLANES = 8
TOKEN_TILE = 512
SEL_TILE = 256
HGRN_CHUNK = 64
HGRN_SUB = 16
HGRN_ROWS = 256
HGRN_HEADS_PER_STEP = 2
CONV_ROWS = 512
CONV_HALO = 32
CONV_CHUNK = 128
PEER_ROWS = 16
BF16_ROWS = 16
DEC_GROUP = 8
VMEM_LIMIT = 48 * 1024 * 1024


def _dot(a, b, prec=None):
    return jnp.dot(a, b, preferred_element_type=F32, precision=prec)


def _dot_nt(a, b, prec=None):
    return lax.dot_general(a, b, (((1,), (1,)), ((), ())), preferred_element_type=F32, precision=prec)


def _rms(x, g):
    return x * lax.rsqrt(jnp.mean(x * x, axis=-1, keepdims=True) + NORM_EPS) * g


def _silu(x):
    return x * jax.nn.sigmoid(x)


def _gelu(x):
    return 0.5 * x * (1.0 + lax.erf(x * 0.7071067811865476))


def _params(sem, flags=None):
    return pltpu.CompilerParams(dimension_semantics=sem, vmem_limit_bytes=VMEM_LIMIT, flags=flags)


def _full(shape):
    n = len(shape)
    return pl.BlockSpec(shape, lambda *_: (0,) * n)


def _row_inputs(arrs):
    specs, ops, layout = [], [], []
    for a in arrs:
        if isinstance(a, tuple):
            tp = a[0].shape[0] // TOKEN_TILE
            width = a[0].shape[1]
            specs += [pl.BlockSpec((TOKEN_TILE, width), lambda i, tp=tp: (jnp.minimum(i, tp - 1), 0)),
                      pl.BlockSpec((TOKEN_TILE, width), lambda i, tp=tp: (jnp.maximum(i - tp, 0), 0))]
            ops += list(a)
            layout.append(tp)
        else:
            specs.append(pl.BlockSpec((TOKEN_TILE, a.shape[1]), lambda i: (i, 0)))
            ops.append(a)
            layout.append(None)
    return specs, ops, tuple(layout)


def _row_values(layout, refs):
    i = pl.program_id(0)
    vals, k = [], 0
    for tp in layout:
        if tp is None:
            vals.append(refs[k][...])
            k += 1
        else:
            vals.append(jnp.where(i < tp, refs[k][...], refs[k + 1][...]))
            k += 2
    return vals, refs[k:]


def _rows_of(a):
    return a[0].shape[0] + a[1].shape[0] if isinstance(a, tuple) else a.shape[0]


def _in0_body(*refs, layout):
    (x,), (g_ref, w_ref, z_ref) = _row_values(layout, refs)
    h = _rms(x, g_ref[...]).astype(BF16)
    z_ref[...] = _dot(h, w_ref[...])


def _in0(x, g, w):
    n = _rows_of(x)
    width = w.shape[1]
    specs, ops, layout = _row_inputs([x])
    return pl.pallas_call(
        functools.partial(_in0_body, layout=layout),
        grid=(n // TOKEN_TILE,),
        in_specs=specs + [_full((1, D_MODEL)), _full((D_MODEL, width))],
        out_specs=pl.BlockSpec((TOKEN_TILE, width), lambda i: (i, 0)),
        out_shape=jax.ShapeDtypeStruct((n, width), F32),
        compiler_params=_params(("arbitrary",)),
        name="in0_proj",
    )(*ops, g, w)


def _glu_body(x_ref, g_ref, w_ref, b_ref, u_ref):
    h = _rms(x_ref[...], g_ref[...]).astype(BF16)
    a = _dot(h, w_ref[...]) + b_ref[...]
    u_ref[...] = a[:, :D_MODEL] * jax.nn.sigmoid(a[:, D_MODEL:])


def _glu(x, g, w, b):
    n = x.shape[0]
    return pl.pallas_call(
        _glu_body,
        grid=(n // TOKEN_TILE,),
        in_specs=[pl.BlockSpec((TOKEN_TILE, D_MODEL), lambda i: (i, 0)), _full((1, D_MODEL)),
                  _full((D_MODEL, 2 * D_MODEL)), _full((1, 2 * D_MODEL))],
        out_specs=pl.BlockSpec((TOKEN_TILE, D_MODEL), lambda i: (i, 0)),
        out_shape=jax.ShapeDtypeStruct((n, D_MODEL), F32),
        compiler_params=_params(("arbitrary",)),
        name="conv_glu",
    )(x, g, w, b)


def _ffn_query(x1, gf_ref, wq_ref, x1_ref, ht_ref, q_ref):
    x1_ref[...] = x1
    h2 = _rms(x1, gf_ref[...])
    ht_ref[...] = h2.T.astype(BF16)
    q_ref[...] = _dot(h2.astype(BF16), wq_ref[...])


def _out0_body(*refs, layout):
    (attn, hg, x), (w_ref, gf_ref, wq_ref, x1_ref, ht_ref, q_ref) = _row_values(layout, refs)
    m = _dot(attn.astype(BF16), w_ref[:QA, :]) + _dot(hg.astype(BF16), w_ref[QA:, :])
    _ffn_query(x + m, gf_ref, wq_ref, x1_ref, ht_ref, q_ref)


def _post1_body(*refs, layout):
    (c, x), (lg_ref, lb_ref, w_ref, gf_ref, wq_ref, x1_ref, ht_ref, q_ref) = _row_values(layout, refs)
    mu = jnp.mean(c, axis=-1, keepdims=True)
    d = c - mu
    var = jnp.mean(d * d, axis=-1, keepdims=True)
    ln = d * lax.rsqrt(var + NORM_EPS) * lg_ref[...] + lb_ref[...]
    y = _dot(_silu(ln).astype(BF16), w_ref[...])
    _ffn_query(x + y, gf_ref, wq_ref, x1_ref, ht_ref, q_ref)


def _mix_out(body, acts, x, consts, gf, wq, name):
    n = _rows_of(x)
    qw = wq.shape[1]
    row = lambda w: pl.BlockSpec((TOKEN_TILE, w), lambda i: (i, 0))
    specs, ops, layout = _row_inputs(list(acts) + [x])
    return pl.pallas_call(
        functools.partial(body, layout=layout),
        grid=(n // TOKEN_TILE,),
        in_specs=specs + [_full(c.shape) for c in consts] + [_full(gf.shape), _full(wq.shape)],
        out_specs=[row(D_MODEL), pl.BlockSpec((D_MODEL, TOKEN_TILE), lambda i: (0, i)), row(qw)],
        out_shape=[jax.ShapeDtypeStruct((n, D_MODEL), F32), jax.ShapeDtypeStruct((D_MODEL, n), BF16),
                   jax.ShapeDtypeStruct((n, qw), F32)],
        compiler_params=_params(("arbitrary",)),
        name=name,
    )(*ops, *consts, gf, wq)


def _head_norm(x, g, seg):
    xx = x * x
    hi = xx.astype(BF16)
    rest = xx - hi.astype(F32)
    mid = rest.astype(BF16)
    ms = _dot(hi, seg) + _dot(mid, seg) + _dot((rest - mid.astype(F32)).astype(BF16), seg)
    return x * lax.rsqrt(ms + NORM_EPS) * g


def _rope(x, cos, sin, first_half):
    half = ROT_DIM // 2
    width = x.shape[1]
    up = pltpu.roll(x, width - half, axis=1)
    dn = pltpu.roll(x, half, axis=1)
    return x * cos + jnp.where(first_half, -up, dn) * sin


def _rope_tables(c, s, reps):
    out = []
    for n in (1, reps):
        cn, sn = (jnp.concatenate([t] * n, axis=1) if n > 1 else t for t in (c, s))
        lane = lax.broadcasted_iota(jnp.int32, cn.shape, 1)
        out.append((cn, sn, (lane % HEAD_DIM) < (ROT_DIM // 2)))
    return out


def _stack_heads(x, g):
    return jnp.concatenate(
        [x[:, (g * KV_REP + r) * HEAD_DIM:(g * KV_REP + r + 1) * HEAD_DIM] for r in range(KV_REP)], axis=0)


def _attn_prompt_body(q_ref, k_ref, v_ref, gq_ref, gk_ref, inv_ref, segq_ref, segk_ref, sink_ref,
                      o_ref, kw_ref, vw_ref, kprev_ref, vprev_ref, cos_ref, sin_ref):
    j = pl.program_id(1)
    w = WINDOW

    @pl.when(j == 0)
    def _():
        kprev_ref[...] = jnp.zeros_like(kprev_ref)
        vprev_ref[...] = jnp.zeros_like(vprev_ref)
        within = lax.broadcasted_iota(jnp.int32, (w, LANES), 0).astype(F32) * inv_ref[...]
        cos_ref[...] = jnp.cos(within)
        sin_ref[...] = jnp.sin(within)

    start = (j * w).astype(F32) * inv_ref[...]
    cs, ss = jnp.cos(start), jnp.sin(start)
    cw, sw = cos_ref[...], sin_ref[...]
    (ck, sk, fk), (cq, sq, fq) = _rope_tables(cs * cw - ss * sw, ss * cw + cs * sw, QA // LANES)
    q = _rope(_head_norm(q_ref[...], gq_ref[...], segq_ref[...]), cq, sq, fq)
    k = _rope(_head_norm(k_ref[...], gk_ref[...], segk_ref[...]), ck, sk, fk)
    v = v_ref[...]
    kp = kprev_ref[...]
    vp = vprev_ref[...]

    qi = lax.broadcasted_iota(jnp.int32, (KV_REP * w, w), 0) % w
    ki = lax.broadcasted_iota(jnp.int32, (KV_REP * w, w), 1)
    m_own = ki <= qi
    m_prev = (ki > qi) & (j > 0)
    rep = lax.broadcasted_iota(jnp.int32, (KV_REP * w, 1), 0) // w
    for g in range(N_KV_A):
        sl = slice(g * HEAD_DIM, (g + 1) * HEAD_DIM)
        qg = _stack_heads(q, g).astype(BF16)
        s_own = jnp.where(m_own, _dot_nt(qg, k[:, sl].astype(BF16)) * ATTN_SCALE, NEG_INF)
        s_prev = jnp.where(m_prev, _dot_nt(qg, kp[:, sl].astype(BF16)) * ATTN_SCALE, NEG_INF)
        sink = jnp.zeros((KV_REP * w, 1), F32)
        for r in range(KV_REP):
            sink = jnp.where(rep == r, sink_ref[g * KV_REP + r], sink)
        mx = jnp.maximum(jnp.maximum(jnp.max(s_own, axis=-1, keepdims=True),
                                     jnp.max(s_prev, axis=-1, keepdims=True)), sink)
        e_own = jnp.exp(s_own - mx)
        e_prev = jnp.exp(s_prev - mx)
        den = (jnp.sum(e_own, axis=-1, keepdims=True) + jnp.sum(e_prev, axis=-1, keepdims=True)
               + jnp.exp(sink - mx))
        o = (_dot(e_own.astype(BF16), v[:, sl].astype(BF16))
             + _dot(e_prev.astype(BF16), vp[:, sl].astype(BF16))) / den
        for r in range(KV_REP):
            hq = g * KV_REP + r
            o_ref[:, hq * HEAD_DIM:(hq + 1) * HEAD_DIM] = o[r * w:(r + 1) * w]

    kprev_ref[...] = k
    vprev_ref[...] = v

    @pl.when(j == pl.num_programs(1) - 1)
    def _():
        kw_ref[0] = k
        vw_ref[0] = v


def _attn_prompt(z, batch, seq, gq, gk, inv, segq, segk, sinks):
    nb = seq // WINDOW
    blk = lambda width, col: pl.BlockSpec((WINDOW, width), lambda b, j: (b * nb + j, col))
    return pl.pallas_call(
        _attn_prompt_body,
        grid=(batch, nb),
        in_specs=[blk(QA, 0), blk(KVA, COL_K), blk(KVA, COL_V), _full(gq.shape), _full(gk.shape),
                  _full(inv.shape), _full(segq.shape), _full(segk.shape),
                  pl.BlockSpec(memory_space=pltpu.SMEM)],
        out_specs=[pl.BlockSpec((WINDOW, QA), lambda b, j: (b * nb + j, 0)),
                   pl.BlockSpec((1, WINDOW, KVA), lambda b, j: (b, 0, 0)),
                   pl.BlockSpec((1, WINDOW, KVA), lambda b, j: (b, 0, 0))],
        out_shape=[jax.ShapeDtypeStruct((batch * seq, QA), F32),
                   jax.ShapeDtypeStruct((batch, WINDOW, KVA), F32),
                   jax.ShapeDtypeStruct((batch, WINDOW, KVA), F32)],
        scratch_shapes=[pltpu.VMEM((WINDOW, KVA), F32), pltpu.VMEM((WINDOW, KVA), F32),
                        pltpu.VMEM((WINDOW, LANES), F32), pltpu.VMEM((WINDOW, LANES), F32)],
        compiler_params=_params(("arbitrary", "arbitrary")),
        name="swa_prompt",
    )(z, z, z, gq, gk, inv, segq, segk, sinks)


def _attn_decode_body(q_ref, k_ref, v_ref, kc_ref, vc_ref, gq_ref, gk_ref, inv_ref, segq_ref, segk_ref,
                      sink_ref, o_ref, kn_ref, vn_ref, *, dec_seq):
    rows_n = DEC_GROUP * dec_seq
    rows = lax.broadcasted_iota(jnp.int32, (rows_n, LANES), 0)
    pos = (PAST_LEN + rows % dec_seq).astype(F32)
    ang = pos * inv_ref[...]
    (ck, sk, fk), (cq, sq, fq) = _rope_tables(jnp.cos(ang), jnp.sin(ang), QA // LANES)
    q = _rope(_head_norm(q_ref[...], gq_ref[...], segq_ref[...]), cq, sq, fq)
    k = _rope(_head_norm(k_ref[...], gk_ref[...], segk_ref[...]), ck, sk, fk)
    v = v_ref[...]
    kn_ref[...] = k
    vn_ref[...] = v

    nq = KV_REP * rows_n
    nc = DEC_GROUP * WINDOW

    def qrow(shape):
        r = lax.broadcasted_iota(jnp.int32, shape, 0) % rows_n
        return r // dec_seq, r % dec_seq

    bq, tq = qrow((nq, nc))
    col = lax.broadcasted_iota(jnp.int32, (nq, nc), 1)
    m_cache = (col // WINDOW == bq) & (col % WINDOW > tq)
    bq, tq = qrow((nq, rows_n))
    col = lax.broadcasted_iota(jnp.int32, (nq, rows_n), 1)
    m_new = (col // dec_seq == bq) & (col % dec_seq <= tq)
    rep = lax.broadcasted_iota(jnp.int32, (nq, 1), 0) // rows_n
    for g in range(N_KV_A):
        sl = slice(g * HEAD_DIM, (g + 1) * HEAD_DIM)
        qg = _stack_heads(q, g).astype(BF16)
        s_c = jnp.where(m_cache, _dot_nt(qg, kc_ref[:, sl].astype(BF16)) * ATTN_SCALE, NEG_INF)
        s_n = jnp.where(m_new, _dot_nt(qg, k[:, sl].astype(BF16)) * ATTN_SCALE, NEG_INF)
        sink = jnp.zeros((nq, 1), F32)
        for r in range(KV_REP):
            sink = jnp.where(rep == r, sink_ref[g * KV_REP + r], sink)
        mx = jnp.maximum(jnp.maximum(jnp.max(s_c, axis=-1, keepdims=True),
                                     jnp.max(s_n, axis=-1, keepdims=True)), sink)
        e_c = jnp.exp(s_c - mx)
        e_n = jnp.exp(s_n - mx)
        den = jnp.sum(e_c, axis=-1, keepdims=True) + jnp.sum(e_n, axis=-1, keepdims=True) + jnp.exp(sink - mx)
        o = (_dot(e_c.astype(BF16), vc_ref[:, sl].astype(BF16))
             + _dot(e_n.astype(BF16), v[:, sl].astype(BF16))) / den
        for r in range(KV_REP):
            hq = g * KV_REP + r
            o_ref[:, hq * HEAD_DIM:(hq + 1) * HEAD_DIM] = o[r * rows_n:(r + 1) * rows_n]


def _attn_decode(z, row0, dec_batch, dec_seq, kc, vc, gq, gk, inv, segq, segk, sinks):
    rows_n = DEC_GROUP * dec_seq
    r0 = row0 // rows_n
    blk = lambda width, col: pl.BlockSpec((rows_n, width), lambda i: (r0 + i, col))
    cache = pl.BlockSpec((DEC_GROUP * WINDOW, KVA), lambda i: (i, 0))
    n = dec_batch * dec_seq
    return pl.pallas_call(
        functools.partial(_attn_decode_body, dec_seq=dec_seq),
        grid=(dec_batch // DEC_GROUP,),
        in_specs=[blk(QA, 0), blk(KVA, COL_K), blk(KVA, COL_V), cache, cache, _full(gq.shape),
                  _full(gk.shape), _full(inv.shape), _full(segq.shape), _full(segk.shape),
                  pl.BlockSpec(memory_space=pltpu.SMEM)],
        out_specs=[pl.BlockSpec((rows_n, QA), lambda i: (i, 0)), pl.BlockSpec((rows_n, KVA), lambda i: (i, 0)),
                   pl.BlockSpec((rows_n, KVA), lambda i: (i, 0))],
        out_shape=[jax.ShapeDtypeStruct((n, QA), F32), jax.ShapeDtypeStruct((n, KVA), F32),
                   jax.ShapeDtypeStruct((n, KVA), F32)],
        compiler_params=_params(("arbitrary",)),
        name="swa_decode",
    )(z, z, z, kc, vc, gq, gk, inv, segq, segk, sinks)


def _hgrn_lower_bound(lb_ref):
    l = lb_ref[...]
    e = jnp.exp(l - jnp.max(l, axis=0, keepdims=True))
    return e[0:1] / jnp.sum(e, axis=0, keepdims=True)


def _hgrn_gates(qb, fb, lb):
    logf = jnp.log(lb + (1.0 - lb) * jax.nn.sigmoid(fb))
    kb = (1.0 - lb) * jax.nn.sigmoid(-fb)
    return _silu(qb), kb, logf


def _hgrn_out(o, g, gate):
    return _rms(o, g) * _silu(gate)


def _hgrn_chunk(qh, kb, ih, logf, st, tril):
    c = qh.shape[0]
    hi = logf.astype(BF16)
    rest = logf - hi.astype(F32)
    mid = rest.astype(BF16)
    b = _dot(tril, hi) + _dot(tril, mid) + _dot(tril, (rest - mid.astype(F32)).astype(BF16))
    ones = jnp.ones((DK_B, LANES), BF16)
    st_b = st.astype(BF16)
    ih_b = ih.astype(BF16)
    o_parts = []
    row = lax.broadcasted_iota(jnp.int32, (c, 1), 0)
    trow = lax.broadcasted_iota(jnp.int32, (HGRN_SUB, 1), 0)
    for blk in range(c // HGRN_SUB):
        t0 = blk * HGRN_SUB
        bi = b[t0:t0 + HGRN_SUB]
        qi = qh[t0:t0 + HGRN_SUB]
        if blk == 0:
            oi = _dot_nt((qi * jnp.exp(bi)).astype(BF16), st_b)
        else:
            base = b[t0 - 1:t0]
            qd = qi * jnp.exp(bi - base)
            kd = jnp.where(row < t0, kb * jnp.exp(jnp.minimum(base - b, 0.0)), 0.0)
            a_off = _dot_nt(qd.astype(BF16), kd.astype(BF16))
            oi = _dot_nt((qd * jnp.exp(base)).astype(BF16), st_b) + _dot(a_off.astype(BF16), ih_b)
        prods = []
        for s in range(HGRN_SUB):
            e = jnp.exp(jnp.where(trow >= s, bi - bi[s:s + 1], 0.0))
            prods.append(qi * e * kb[t0 + s:t0 + s + 1])
        a_diag = _dot(jnp.concatenate(prods, axis=0).astype(BF16), ones)
        for s in range(HGRN_SUB):
            a = a_diag[s * HGRN_SUB:(s + 1) * HGRN_SUB]
            oi = oi + jnp.where(trow >= s, a, 0.0) * ih[t0 + s:t0 + s + 1]
        o_parts.append(oi)
    last = b[c - 1:c]
    st_new = st * jnp.exp(last) + _dot(ih.T.astype(BF16), (kb * jnp.exp(last - b)).astype(BF16))
    return jnp.concatenate(o_parts, axis=0), st_new


def _hgrn_prompt_body(qb_ref, fb_ref, ib_ref, gb_ref, lb_ref, go_ref, tril_ref, o_ref, s_ref, st_ref):
    tb = pl.program_id(2)

    @pl.when(tb == 0)
    def _():
        st_ref[...] = jnp.zeros_like(st_ref)

    tril = tril_ref[...]
    heads = [slice(h * DK_B, (h + 1) * DK_B) for h in range(HGRN_HEADS_PER_STEP)]
    lbs = [_hgrn_lower_bound(lb_ref.at[:, hs]) for hs in heads]
    sts = [st_ref[h] for h in range(HGRN_HEADS_PER_STEP)]
    for c in range(HGRN_ROWS // HGRN_CHUNK):
        rs = slice(c * HGRN_CHUNK, (c + 1) * HGRN_CHUNK)
        for h, hs in enumerate(heads):
            qh, kb, logf = _hgrn_gates(qb_ref[rs, hs], fb_ref[rs, hs], lbs[h])
            o, sts[h] = _hgrn_chunk(qh, kb, ib_ref[rs, hs], logf, sts[h], tril)
            o_ref[rs, hs] = _hgrn_out(o, go_ref[...], gb_ref[rs, hs])
    for h in range(HGRN_HEADS_PER_STEP):
        st_ref[h] = sts[h]

    @pl.when(tb == pl.num_programs(2) - 1)
    def _():
        for h in range(HGRN_HEADS_PER_STEP):
            s_ref[0, h] = sts[h].T


def _hgrn_prompt(z, batch, seq, lb, go, tril):
    nt = seq // HGRN_ROWS
    hp = HGRN_HEADS_PER_STEP
    assert N_HEADS_B % hp == 0 and all(c % hp == 0 for c in (COL_QB, COL_FB, COL_IB, COL_GB))
    blk = lambda col: pl.BlockSpec((HGRN_ROWS, hp * DK_B), lambda b, h, t: (b * nt + t, col // hp + h))
    return pl.pallas_call(
        _hgrn_prompt_body,
        grid=(batch, N_HEADS_B // hp, nt),
        in_specs=[blk(COL_QB), blk(COL_FB), blk(COL_IB), blk(COL_GB),
                  pl.BlockSpec((lb.shape[0], hp * DK_B), lambda b, h, t: (0, h)), _full(go.shape), _full(tril.shape)],
        out_specs=[pl.BlockSpec((HGRN_ROWS, hp * DV_B), lambda b, h, t: (b * nt + t, h)),
                   pl.BlockSpec((1, hp, DK_B, DV_B), lambda b, h, t: (b, h, 0, 0))],
        out_shape=[jax.ShapeDtypeStruct((batch * seq, VB), F32),
                   jax.ShapeDtypeStruct((batch, N_HEADS_B, DK_B, DV_B), F32)],
        scratch_shapes=[pltpu.VMEM((hp, DV_B, DK_B), F32)],
        compiler_params=_params(("arbitrary", "arbitrary", "arbitrary")),
        name="hgrn_prompt",
    )(z, z, z, z, lb, go, tril)


def _pad_rows(x, rows):
    return jnp.concatenate([x, jnp.zeros((rows - x.shape[0], x.shape[1]), x.dtype)], axis=0)


def _hgrn_decode_body(z_ref, s0_ref, lb_ref, go_ref, tril_ref, o_ref, s_ref, *, dec_seq):
    rows_n = z_ref.shape[0]
    groups = rows_n // dec_seq
    row = lax.broadcasted_iota(jnp.int32, (rows_n, 1), 0)
    for h in range(N_HEADS_B):
        col = lambda c0: slice((c0 + h) * LANES, (c0 + h + 1) * LANES)
        hs = slice(h * DK_B, (h + 1) * DK_B)
        lb = _hgrn_lower_bound(lb_ref.at[:, hs])
        qh, kb, logf = _hgrn_gates(z_ref[:, col(COL_QB)], z_ref[:, col(COL_FB)], lb)
        ih = z_ref[:, col(COL_IB)]
        b = _dot(tril_ref[...], _pad_rows(logf, LANES), HI)[:rows_n]
        b_t = _pad_rows(b, LANES).T
        ih_pad = _pad_rows(ih, LANES)
        o = jnp.zeros((rows_n, DV_B), F32)
        for e in range(groups):
            mine = (row // dec_seq) == e
            s0 = s0_ref[e, h]
            o = o + _dot(jnp.where(mine, qh * jnp.exp(b), 0.0), s0, HI)
            for s in range(dec_seq):
                r = e * dec_seq + s
                live = mine & (row >= r)
                ex = jnp.exp(jnp.where(live, b - b[r:r + 1], 0.0))
                a = jnp.sum(qh * ex * kb[r:r + 1], axis=-1, keepdims=True)
                o = o + jnp.where(live, a, 0.0) * ih[r:r + 1]
            r_last = (e + 1) * dec_seq - 1
            last = b[r_last:r_last + 1]
            kd = jnp.where(mine, kb * jnp.exp(jnp.minimum(last - b, 0.0)), 0.0)
            s_ref[e, h] = s0 * jnp.exp(b_t[:, r_last:r_last + 1]) + _dot(_pad_rows(kd, LANES).T, ih_pad, HI)
        o_ref[:, hs] = _hgrn_out(o, go_ref[...], z_ref[:, col(COL_GB)])


def _hgrn_decode(z, row0, dec_batch, dec_seq, s0, lb, go, tril):
    groups = 8 // dec_seq
    rows_n = groups * dec_seq
    r0 = row0 // rows_n
    st = pl.BlockSpec((groups, N_HEADS_B, DK_B, DV_B), lambda i: (i, 0, 0, 0))
    return pl.pallas_call(
        functools.partial(_hgrn_decode_body, dec_seq=dec_seq),
        grid=(dec_batch // groups,),
        in_specs=[pl.BlockSpec((rows_n, z.shape[1]), lambda i: (r0 + i, 0)), st,
                  _full(lb.shape), _full(go.shape), _full(tril.shape)],
        out_specs=[pl.BlockSpec((rows_n, VB), lambda i: (i, 0)), st],
        out_shape=[jax.ShapeDtypeStruct((dec_batch * dec_seq, VB), F32),
                   jax.ShapeDtypeStruct(s0.shape, F32)],
        compiler_params=_params(("arbitrary",)),
        name="hgrn_decode",
    )(z, s0, lb, go, tril)


def _conv_prompt_body(cur_ref, prev_ref, w_ref, b_ref, c_ref, ext_ref, sh_ref):
    t = pl.program_id(1)
    ext_ref[:CONV_HALO, :] = jnp.where(t > 0, prev_ref[...], 0.0)
    ext_ref[CONV_HALO:, :] = cur_ref[...]
    lead = CONV_HALO - (CONV_W - 1)
    chunk = CONV_CHUNK

    def cols(ci, carry):
        cs = pl.ds(pl.multiple_of(ci * LANES, LANES), LANES)
        for r in range(CONV_ROWS // chunk):
            acc = jnp.zeros((chunk, LANES), F32) + b_ref[:, cs]
            for res in range(SUBLANES):
                taps = range(res, CONV_W, SUBLANES)
                span = chunk + taps[-1] - res
                sh_ref[:span, :] = ext_ref[pl.ds(r * chunk + lead + res, span), cs]
                for w in taps:
                    acc = acc + sh_ref[w - res:w - res + chunk, :] * w_ref[pl.ds(w, 1), cs]
            c_ref[pl.ds(r * chunk, chunk), cs] = acc
        return carry

    lax.fori_loop(0, D_MODEL // LANES, cols, 0)


def _conv_prompt(u, batch, seq, w, b):
    nt = seq // CONV_ROWS
    per = CONV_ROWS // CONV_HALO
    return pl.pallas_call(
        _conv_prompt_body,
        grid=(batch, nt),
        in_specs=[pl.BlockSpec((CONV_ROWS, D_MODEL), lambda bi, t: (bi * nt + t, 0)),
                  pl.BlockSpec((CONV_HALO, D_MODEL), lambda bi, t: (jnp.maximum((bi * nt + t) * per - 1, 0), 0)),
                  _full(w.shape), _full(b.shape)],
        out_specs=pl.BlockSpec((CONV_ROWS, D_MODEL), lambda bi, t: (bi * nt + t, 0)),
        out_shape=jax.ShapeDtypeStruct((batch * seq, D_MODEL), F32),
        scratch_shapes=[pltpu.VMEM((CONV_HALO + CONV_ROWS, D_MODEL), F32),
                        pltpu.VMEM((CONV_CHUNK + CONV_HALO, LANES), F32)],
        compiler_params=_params(("arbitrary", "arbitrary")),
        name="conv_prompt",
    )(u, u, w, b)


def _conv_decode_body(u_ref, st_ref, w_ref, wshift_ref, b_ref, c_ref, *, dec_seq):
    for e in range(DEC_GROUP):
        past = st_ref[e]
        for t in range(dec_seq):
            acc = jnp.sum(past * wshift_ref[t], axis=0, keepdims=True) + b_ref[...]
            for t2 in range(t + 1):
                wi = CONV_W - 1 - t + t2
                acc = acc + u_ref[e * dec_seq + t2:e * dec_seq + t2 + 1, :] * w_ref[wi:wi + 1, :]
            c_ref[e * dec_seq + t:e * dec_seq + t + 1, :] = acc


def _conv_decode(u, row0, dec_batch, dec_seq, state, w, wshift, b):
    rows_n = DEC_GROUP * dec_seq
    r0 = row0 // rows_n
    return pl.pallas_call(
        functools.partial(_conv_decode_body, dec_seq=dec_seq),
        grid=(dec_batch // DEC_GROUP,),
        in_specs=[pl.BlockSpec((rows_n, D_MODEL), lambda i: (r0 + i, 0)),
                  pl.BlockSpec((DEC_GROUP, CONV_W - 1, D_MODEL), lambda i: (i, 0, 0)),
                  _full(w.shape), _full(wshift.shape), _full(b.shape)],
        out_specs=pl.BlockSpec((rows_n, D_MODEL), lambda i: (i, 0)),
        out_shape=jax.ShapeDtypeStruct((dec_batch * dec_seq, D_MODEL), F32),
        compiler_params=_params(("arbitrary",)),
        name="conv_decode",
    )(u, state, w, wshift, b)


def _extract_top(s, idx, count, none_rank):
    work = s
    rank = jnp.full(s.shape, none_rank, F32)
    vals = []
    for r in range(count):
        m = jnp.max(work, axis=0, keepdims=True)
        first = jnp.min(jnp.where(work == m, idx, np.float32(1e9)), axis=0, keepdims=True)
        hit = idx == first
        rank = jnp.where(hit, np.float32(r), rank)
        work = jnp.where(hit, -jnp.inf, work)
        vals.append(m)
    return rank, vals


_PEER_CAND = [(a, b) for a in range(PEER_TOPK) for b in range(PEER_TOPK) if (a + 1) * (b + 1) <= PEER_TOPK]
_PEER_CAND_ROWS = -(-len(_PEER_CAND) // 8) * 8


def _extract_by_value(s, count):
    work = s
    rank = jnp.full(s.shape, np.float32(count), F32)
    vals = []
    for r in range(count):
        m = jnp.max(work, axis=0, keepdims=True)
        hit = work == m
        rank = jnp.where(hit, np.float32(r), rank)
        work = jnp.where(hit, -jnp.inf, work)
        vals.append(m)
    ranked = jnp.sum(jnp.where(rank < count, 1.0, 0.0), axis=0, keepdims=True)
    return rank, vals, ranked


def _peer_select_body(q_ref, keys_ref, rank_ref, qe_ref, lq_ref, pe_ref):
    ts = q_ref.shape[0]
    kidx = lax.broadcasted_iota(jnp.int32, (N_KEYS, ts), 0).astype(F32)
    cidx = lax.broadcasted_iota(jnp.int32, (_PEER_CAND_ROWS, ts), 0).astype(F32)
    tidx = lax.broadcasted_iota(jnp.int32, (PEER_TOPK, ts), 0).astype(F32)
    bidx = lax.broadcasted_iota(jnp.int32, (SUBLANES, ts), 0).astype(F32)
    pad =jnp.full((_PEER_CAND_ROWS, ts), -jnp.inf, F32)

    def tables(h, s0, s1, exact):
        if exact:
            rank0, v0 = _extract_top(s0, kidx, PEER_TOPK, PEER_TOPK)
            rank1, v1 = _extract_top(s1, kidx, PEER_TOPK, PEER_TOPK)
        else:
            rank0, v0, n0 = _extract_by_value(s0, PEER_TOPK)
            rank1, v1, n1 = _extract_by_value(s1, PEER_TOPK)
        if exact:
            pairs = list(_PEER_CAND)
            cand = pad
            for ci, (a, b) in enumerate(pairs):
                cand = jnp.where(cidx == ci, v0[a] + v1[b], cand)
            crank, _ = _extract_top(cand, cidx, PEER_TOPK, PEER_TOPK)
            off = None
        else:
            col0 = jnp.zeros((PEER_TOPK, ts), F32)
            col1 = jnp.zeros((PEER_TOPK, ts), F32)
            low1 = jnp.zeros((SUBLANES, ts), F32)
            for r in range(PEER_TOPK):
                col0 = jnp.where(tidx == r, v0[r], col0)
                col1 = jnp.where(tidx == r, v1[r], col1)
                if r < SUBLANES:
                    low1 = jnp.where(bidx == r, v1[r], low1)
            ninf = np.float32(-np.inf)
            groups = [v0[0] + col1, jnp.where(tidx == 0, ninf, col0 + v1[0])]
            pairs = [(0, b) for b in range(PEER_TOPK)] + [(a, 0) if a else None for a in range(PEER_TOPK)]
            rest = [p for p in _PEER_CAND if p[0] and p[1]]
            for a in sorted({p[0] for p in rest}):
                mine = [p for p in rest if p[0] == a]
                if len(mine) > 2:
                    groups.append(jnp.where((bidx >= 1) & (bidx <= len(mine)), v0[a] + low1, ninf))
                    pairs += [(a, b) if 1 <= b <= len(mine) else None for b in range(SUBLANES)]
            loose = [p for p in rest if p not in pairs]
            assert len(loose) <= SUBLANES
            tail = jnp.full((SUBLANES, ts), ninf, F32)
            for r, (a, b) in enumerate(loose):
                tail = jnp.where(bidx == r, v0[a] + v1[b], tail)
            groups.append(tail)
            pairs += loose + [None] * (SUBLANES - len(loose))
            assert sorted(p for p in pairs if p) == sorted(_PEER_CAND)
            cand = jnp.concatenate(groups, axis=0)
            crank, _, nc = _extract_by_value(cand, PEER_TOPK)
            off = jnp.max(jnp.abs(n0 - PEER_TOPK) + jnp.abs(n1 - PEER_TOPK) + jnp.abs(nc - PEER_TOPK))
        picked = jnp.where(crank < PEER_TOPK, 1.0, 0.0)
        e = picked * jnp.exp(cand - (v0[0] + v1[0]))
        z = jnp.sum(e, axis=0, keepdims=True)
        lq = jnp.zeros((N_KEYS, ts), F32)
        for a in range(PEER_TOPK):
            cnt = jnp.zeros((1, ts), F32)
            for ci, p in enumerate(pairs):
                if p is not None and p[0] == a:
                    cnt = cnt + picked[ci:ci + 1]
            lq = jnp.where(rank0 == a, cnt, lq)
        rank_b = rank1.astype(BF16)
        qe_b = jnp.exp(s1 - v1[0]).astype(BF16)
        for k in range(N_KEYS // BF16_ROWS):
            rank_ref[h, k] = pltpu.bitcast(rank_b[k * BF16_ROWS:(k + 1) * BF16_ROWS], jnp.uint32)
            qe_ref[h, k] = pltpu.bitcast(qe_b[k * BF16_ROWS:(k + 1) * BF16_ROWS], jnp.uint32)
        lq_ref[h] = lq
        pe_ref[h] = jnp.exp(s0 - v0[0]) / z
        return off

    def head(h, carry):
        c0 = pl.multiple_of(h * 2 * PEER_DKH, 2 * PEER_DKH)
        q0 = q_ref[:, pl.ds(c0, PEER_DKH)].astype(BF16)
        q1 = q_ref[:, pl.ds(c0 + PEER_DKH, PEER_DKH)].astype(BF16)
        s0 = _dot_nt(keys_ref[h, 0], q0)
        s1 = _dot_nt(keys_ref[h, 1], q1)
        off = tables(h, s0, s1, exact=False)

        @pl.when(off > 0.5)
        def _():
            tables(h, s0, s1, exact=True)

        return carry

    lax.fori_loop(0, PEER_HEADS, head, 0)


def _peer_select(q, keys):
    n = q.shape[0]
    out = jax.ShapeDtypeStruct((PEER_HEADS, N_KEYS, n), F32)
    ospec = pl.BlockSpec((PEER_HEADS, N_KEYS, SEL_TILE), lambda i: (0, 0, i))
    packed = (PEER_HEADS, N_KEYS // BF16_ROWS, BF16_ROWS // 2)
    out_b = jax.ShapeDtypeStruct(packed + (n,), jnp.uint32)
    ospec_b = pl.BlockSpec(packed + (SEL_TILE,), lambda i: (0, 0, 0, i))
    return pl.pallas_call(
        _peer_select_body,
        grid=(n // SEL_TILE,),
        in_specs=[pl.BlockSpec((SEL_TILE, q.shape[1]), lambda i: (i, 0)), _full(keys.shape)],
        out_specs=[ospec_b, ospec_b, ospec, ospec],
        out_shape=[out_b, out_b, out, out],
        compiler_params=_params(("arbitrary",)),
        name="peer_select",
    )(q, keys)


def _peer_dense_body(ht_ref, u_ref, vt_ref, rank_ref, qe_ref, lq_ref, pe_ref, x_ref, *rest, prompt_tiles):
    *o_ref, acc_ref, a_ref, w_ref = rest
    step = pl.program_id(1)

    @pl.when(step == 0)
    def _():
        acc_ref[...] = jnp.zeros_like(acc_ref)

    @pl.when(step >= 0)
    def _():
        quarter = PEER_ROWS * N_KEYS // 4
        for qi in range(4):
            rows = slice(qi * quarter, (qi + 1) * quarter)
            a_ref[rows, :] = _dot(u_ref[rows, :], ht_ref[...])

    @pl.when(step >= 0)
    def _():
        sub = (N_KEYS // BF16_ROWS, BF16_ROWS, LANES)
        for il in range(PEER_ROWS):
            for ci in range(TOKEN_TILE // LANES):
                cs = slice(ci * LANES, (ci + 1) * LANES)
                gate = jnp.zeros(sub, BF16)
                for h in range(PEER_HEADS):
                    lq = jnp.broadcast_to(lq_ref[h, il:il + 1, cs], sub[1:]).astype(BF16)
                    pe = jnp.broadcast_to(pe_ref[h, il:il + 1, cs], sub[1:]).astype(BF16)
                    take = pltpu.bitcast(rank_ref[h, :, :, cs], BF16) < lq[None]
                    qe = pltpu.bitcast(qe_ref[h, :, :, cs], BF16)
                    gate = gate + jnp.where(take, qe, jnp.zeros(sub, BF16)) * pe[None]
                for k in range(sub[0]):
                    rs = slice(il * N_KEYS + k * BF16_ROWS, il * N_KEYS + (k + 1) * BF16_ROWS)
                    w_ref[rs, cs] = gate[k] * _gelu(a_ref[rs, cs].astype(BF16))
        acc_ref[...] += _dot(vt_ref[0], w_ref[...])

    last = step == pl.num_programs(1) - 1
    if prompt_tiles is None:
        @pl.when(last)
        def _():
            o_ref[0][...] = x_ref[...] + acc_ref[...].T
    else:
        tile = pl.program_id(0)
        o_prompt, o_sample = o_ref

        @pl.when(last & (tile < prompt_tiles))
        def _():
            o_prompt[...] = x_ref[...] + acc_ref[...].T

        @pl.when(last & (tile >= prompt_tiles))
        def _():
            o_sample[...] = x_ref[...] + acc_ref[...].T


def _peer_dense(ht, u, vt, rank, qe, lq, pe, x, n_prompt=None):
    n = x.shape[0]
    ex = PEER_ROWS * N_KEYS
    steps = N_KEYS // PEER_ROWS
    tok = pl.BlockSpec((PEER_HEADS, N_KEYS // BF16_ROWS, BF16_ROWS // 2, TOKEN_TILE), lambda t, e: (0, 0, 0, t))
    rowsel = pl.BlockSpec((PEER_HEADS, PEER_ROWS, TOKEN_TILE), lambda t, e: (0, e, t))
    if n_prompt is None:
        tp = None
        out_specs = pl.BlockSpec((TOKEN_TILE, D_MODEL), lambda t, e: (t, 0))
        out_shape = jax.ShapeDtypeStruct((n, D_MODEL), F32)
    else:
        tp = n_prompt // TOKEN_TILE
        out_specs = [pl.BlockSpec((TOKEN_TILE, D_MODEL), lambda t, e: (jnp.minimum(t, tp - 1), 0)),
                     pl.BlockSpec((TOKEN_TILE, D_MODEL), lambda t, e: (jnp.maximum(t - tp, 0), 0))]
        out_shape = [jax.ShapeDtypeStruct((n_prompt, D_MODEL), F32),
                     jax.ShapeDtypeStruct((n - n_prompt, D_MODEL), F32)]
    return pl.pallas_call(
        functools.partial(_peer_dense_body, prompt_tiles=tp),
        grid=(n // TOKEN_TILE, steps),
        in_specs=[pl.BlockSpec((D_MODEL, TOKEN_TILE), lambda t, e: (0, t)),
                  pl.BlockSpec((ex, D_MODEL), lambda t, e: (e, 0)),
                  pl.BlockSpec((1, D_MODEL, ex), lambda t, e: (e, 0, 0)),
                  tok, tok, rowsel, rowsel,
                  pl.BlockSpec((TOKEN_TILE, D_MODEL), lambda t, e: (t, 0))],
        out_specs=out_specs,
        out_shape=out_shape,
        scratch_shapes=[pltpu.VMEM((D_MODEL, TOKEN_TILE), F32), pltpu.VMEM((ex, TOKEN_TILE), F32),
                        pltpu.VMEM((ex, TOKEN_TILE), BF16)],
        compiler_params=_params(("arbitrary", "arbitrary")),
        name="peer_dense",
    )(ht, u, vt, rank, qe, lq, pe, x)


def _peer(x1, ht, q, keys, u, vt, n_prompt=None):
    rank, qe, lq, pe = _peer_select(q, keys)
    return _peer_dense(ht, u, vt, rank, qe, lq, pe, x1, n_prompt)


def _value_blocks(v):
    ex = PEER_ROWS * N_KEYS
    return v.astype(BF16).reshape(v.shape[0] // ex, ex, v.shape[1]).transpose(0, 2, 1)


def _block_diag_mean(width):
    idx = np.arange(width) // HEAD_DIM
    return jnp.asarray((idx[:, None] == idx[None, :]).astype(np.float32) / HEAD_DIM, BF16)


def kernel(x_prompt, x_sample, cache_swa_k, cache_swa_v, state_hgrn, state_conv, norm_mix_g, norm_ffn_g, w_in0, attn_q_norm_g, attn_k_norm_g, attn_sinks, hgrn_lb, hgrn_o_norm_g, w_out0, conv_w_pw1, conv_b_pw1, conv_w_dw, conv_b_dw, conv_ln_g, conv_ln_b, conv_w_pw2, peer_w_q, peer_keys, peer_u, peer_v):
    batch, seq, _ = x_prompt.shape
    dec_batch, dec_seq, _ = x_sample.shape
    n_p = batch * seq
    n_s = dec_batch * dec_seq
    assert seq % CONV_ROWS == 0 and seq % HGRN_ROWS == 0 and n_p % TOKEN_TILE == 0 and n_s % TOKEN_TILE == 0
    assert dec_batch % DEC_GROUP == 0 and 8 % dec_seq == 0 and dec_seq <= CONV_W - 1

    x = (x_prompt.reshape(n_p, D_MODEL), x_sample.reshape(n_s, D_MODEL))
    row = lambda v: v.reshape(1, -1).astype(F32)

    z = _in0(x, row(norm_mix_g[0]), w_in0.astype(BF16))
    d = np.arange(LANES) % HEAD_DIM
    inv = ROPE_THETA ** (-jnp.arange(0, ROT_DIM, 2, dtype=F32) / ROT_DIM)
    inv_lane = jnp.where(jnp.asarray(d < ROT_DIM), inv[d % (ROT_DIM // 2)], 0.0).reshape(1, LANES)
    gq = row(jnp.tile(attn_q_norm_g, N_HEADS_A))
    gk = row(jnp.tile(attn_k_norm_g, N_KV_A))
    segq, segk = _block_diag_mean(QA), _block_diag_mean(KVA)
    sinks = attn_sinks.astype(F32)
    attn_p, k_win_p, v_win_p = _attn_prompt(z, batch, seq, gq, gk, inv_lane, segq, segk, sinks)
    kc = cache_swa_k.reshape(dec_batch * WINDOW, KVA)
    vc = cache_swa_v.reshape(dec_batch * WINDOW, KVA)
    attn_s, k_new, v_new = _attn_decode(z, n_p, dec_batch, dec_seq, kc, vc, gq, gk, inv_lane, segq, segk, sinks)

    go = row(hgrn_o_norm_g)
    tril = jnp.asarray(np.tril(np.ones((HGRN_CHUNK, HGRN_CHUNK), np.float32)), BF16)
    hg_p, s_p = _hgrn_prompt(z, batch, seq, hgrn_lb, go, tril)
    r = np.arange(LANES)
    tril_dec = jnp.asarray(((r[:, None] // dec_seq == r[None, :] // dec_seq) & (r[None, :] <= r[:, None]))
                           .astype(np.float32))
    hg_s, s_s = _hgrn_decode(z, n_p, dec_batch, dec_seq, state_hgrn, hgrn_lb, go, tril_dec)

    x1, ht, q = _mix_out(_out0_body, [(attn_p, attn_s), (hg_p, hg_s)], x, [w_out0.astype(BF16)], row(norm_ffn_g[0]),
                         peer_w_q[0].astype(BF16), "mix0_out")
    x2 = _peer(x1, ht, q, peer_keys[0].astype(BF16), peer_u[0].astype(BF16), _value_blocks(peer_v[0]))

    u = _glu(x2, row(norm_mix_g[1]), conv_w_pw1.astype(BF16), row(conv_b_pw1))
    wdw = jnp.concatenate([conv_w_dw, jnp.zeros((1, D_MODEL), F32)], axis=0)
    bdw = row(conv_b_dw)
    c_p = _conv_prompt(u, batch, seq, wdw, bdw)
    wshift = jnp.stack([jnp.concatenate([jnp.zeros((t, D_MODEL), F32), conv_w_dw[:CONV_W - 1 - t]], axis=0)
                        for t in range(dec_seq)])
    c_s = _conv_decode(u, n_p, dec_batch, dec_seq, state_conv, wdw, wshift, bdw)
    x3, ht, q = _mix_out(_post1_body, [(c_p, c_s)], x2, [row(conv_ln_g), row(conv_ln_b), conv_w_pw2.astype(BF16)],
                         row(norm_ffn_g[1]), peer_w_q[1].astype(BF16), "mix1_out")
    y_p, y_s = _peer(x3, ht, q, peer_keys[1].astype(BF16), peer_u[1].astype(BF16),
                     _value_blocks(peer_v[1]), n_p)

    kv = lambda t: t.reshape(t.shape[0], WINDOW, N_KV_A, HEAD_DIM)
    k_win_s = jnp.concatenate([cache_swa_k[:, dec_seq:], k_new.reshape(dec_batch, dec_seq, N_KV_A, HEAD_DIM)], axis=1)
    v_win_s = jnp.concatenate([cache_swa_v[:, dec_seq:], v_new.reshape(dec_batch, dec_seq, N_KV_A, HEAD_DIM)], axis=1)
    u_p = u[:n_p].reshape(batch, seq, D_MODEL)
    u_s = u[n_p:].reshape(dec_batch, dec_seq, D_MODEL)
    conv_buf_p = u_p[:, seq - (CONV_W - 1):]
    conv_buf_s = jnp.concatenate([state_conv[:, dec_seq:], u_s], axis=1)
    return (y_p.reshape(batch, seq, D_MODEL), y_s.reshape(dec_batch, dec_seq, D_MODEL),
            kv(k_win_p), kv(v_win_p), s_p, conv_buf_p, k_win_s, v_win_s, s_s, conv_buf_s)
```

```python
import functools

import numpy as np
import jax
import jax.numpy as jnp
from jax import lax
from jax.experimental import pallas as pl
from jax.experimental.pallas import tpu as pltpu

F32 = jnp.float32
BF16 = jnp.bfloat16
HI = lax.Precision.HIGHEST

D_MODEL = 1024
PAST_LEN = 8192
HEAD_DIM = 64
N_HEADS_A = 8
N_KV_A = 2
KV_REP = N_HEADS_A // N_KV_A
WINDOW = 128
ROT_DIM = HEAD_DIM // 4
ROPE_THETA = 500000.0
ATTN_SCALE = HEAD_DIM ** -0.5
NEG_INF = -1e30
N_HEADS_B = 4
DK_B = 128
DV_B = 128
CONV_W = 31
N_KEYS = 128
PEER_HEADS = 8
PEER_TOPK = 16
PEER_DKH = 128
NORM_EPS = 1e-6

QA = N_HEADS_A * HEAD_DIM
KVA = N_KV_A * HEAD_DIM
QB = N_HEADS_B * DK_B
VB = N_HEADS_B * DV_B
IN0_WIDTH = QA + 2 * KVA + 2 * QB + 2 * VB
COL_K = QA // 128
COL_V = (QA + KVA) // 128
COL_QB = (QA + 2 * KVA) // 128
COL_FB = COL_QB + QB // 128
COL_IB = COL_FB + QB // 128
COL_GB = COL_IB + VB // 128

LANES = 128
SUBLANES = 8
TOKEN_TILE = 512
SEL_TILE = 256
HGRN_CHUNK = 64
HGRN_SUB = 16
HGRN_ROWS = 256
HGRN_HEADS_PER_STEP = 2
CONV_ROWS = 512
CONV_HALO = 32
CONV_CHUNK = 128
PEER_ROWS = 16
BF16_ROWS = 16
DEC_GROUP = 8
VMEM_LIMIT = 48 * 1024 * 1024


def _dot(a, b, prec=None):
    return jnp.dot(a, b, preferred_element_type=F32, precision=prec)


def _dot_nt(a, b, prec=None):
    return lax.dot_general(a, b, (((1,), (1,)), ((), ())), preferred_element_type=F32, precision=prec)


def _rms(x, g):
    return x * lax.rsqrt(jnp.mean(x * x, axis=-1, keepdims=True) + NORM_EPS) * g


def _silu(x):
    return x * jax.nn.sigmoid(x)


def _gelu(x):
    return 0.5 * x * (1.0 + lax.erf(x * 0.7071067811865476))


def _params(sem, flags=None):
    return pltpu.CompilerParams(dimension_semantics=sem, vmem_limit_bytes=VMEM_LIMIT, flags=flags)


def _full(shape):
    n = len(shape)
    return pl.BlockSpec(shape, lambda *_: (0,) * n)


def _row_inputs(arrs):
    specs, ops, layout = [], [], []
    for a in arrs:
        if isinstance(a, tuple):
            tp = a[0].shape[0] // TOKEN_TILE
            width = a[0].shape[1]
            specs += [pl.BlockSpec((TOKEN_TILE, width), lambda i, tp=tp: (jnp.minimum(i, tp - 1), 0)),
                      pl.BlockSpec((TOKEN_TILE, width), lambda i, tp=tp: (jnp.maximum(i - tp, 0), 0))]
            ops += list(a)
            layout.append(tp)
        else:
            specs.append(pl.BlockSpec((TOKEN_TILE, a.shape[1]), lambda i: (i, 0)))
            ops.append(a)
            layout.append(None)
    return specs, ops, tuple(layout)


def _row_values(layout, refs):
    i = pl.program_id(0)
    vals, k = [], 0
    for tp in layout:
        if tp is None:
            vals.append(refs[k][...])
            k += 1
        else:
            vals.append(jnp.where(i < tp, refs[k][...], refs[k + 1][...]))
            k += 2
    return vals, refs[k:]


def _rows_of(a):
    return a[0].shape[0] + a[1].shape[0] if isinstance(a, tuple) else a.shape[0]


def _in0_body(*refs, layout):
    (x,), (g_ref, w_ref, z_ref) = _row_values(layout, refs)
    h = _rms(x, g_ref[...]).astype(BF16)
    z_ref[...] = _dot(h, w_ref[...])


def _in0(x, g, w):
    n = _rows_of(x)
    width = w.shape[1]
    specs, ops, layout = _row_inputs([x])
    return pl.pallas_call(
        functools.partial(_in0_body, layout=layout),
        grid=(n // TOKEN_TILE,),
        in_specs=specs + [_full((1, D_MODEL)), _full((D_MODEL, width))],
        out_specs=pl.BlockSpec((TOKEN_TILE, width), lambda i: (i, 0)),
        out_shape=jax.ShapeDtypeStruct((n, width), F32),
        compiler_params=_params(("arbitrary",)),
        name="in0_proj",
    )(*ops, g, w)


def _glu_body(x_ref, g_ref, w_ref, b_ref, u_ref):
    h = _rms(x_ref[...], g_ref[...]).astype(BF16)
    a = _dot(h, w_ref[...]) + b_ref[...]
    u_ref[...] = a[:, :D_MODEL] * jax.nn.sigmoid(a[:, D_MODEL:])


def _glu(x, g, w, b):
    n = x.shape[0]
    return pl.pallas_call(
        _glu_body,
        grid=(n // TOKEN_TILE,),
        in_specs=[pl.BlockSpec((TOKEN_TILE, D_MODEL), lambda i: (i, 0)), _full((1, D_MODEL)),
                  _full((D_MODEL, 2 * D_MODEL)), _full((1, 2 * D_MODEL))],
        out_specs=pl.BlockSpec((TOKEN_TILE, D_MODEL), lambda i: (i, 0)),
        out_shape=jax.ShapeDtypeStruct((n, D_MODEL), F32),
        compiler_params=_params(("arbitrary",)),
        name="conv_glu",
    )(x, g, w, b)


def _ffn_query(x1, gf_ref, wq_ref, x1_ref, ht_ref, q_ref):
    x1_ref[...] = x1
    h2 = _rms(x1, gf_ref[...])
    ht_ref[...] = h2.T.astype(BF16)
    q_ref[...] = _dot(h2.astype(BF16), wq_ref[...])


def _out0_body(*refs, layout):
    (attn, hg, x), (w_ref, gf_ref, wq_ref, x1_ref, ht_ref, q_ref) = _row_values(layout, refs)
    m = _dot(attn.astype(BF16), w_ref[:QA, :]) + _dot(hg.astype(BF16), w_ref[QA:, :])
    _ffn_query(x + m, gf_ref, wq_ref, x1_ref, ht_ref, q_ref)


def _post1_body(*refs, layout):
    (c, x), (lg_ref, lb_ref, w_ref, gf_ref, wq_ref, x1_ref, ht_ref, q_ref) = _row_values(layout, refs)
    mu = jnp.mean(c, axis=-1, keepdims=True)
    d = c - mu
    var = jnp.mean(d * d, axis=-1, keepdims=True)
    ln = d * lax.rsqrt(var + NORM_EPS) * lg_ref[...] + lb_ref[...]
    y = _dot(_silu(ln).astype(BF16), w_ref[...])
    _ffn_query(x + y, gf_ref, wq_ref, x1_ref, ht_ref, q_ref)


def _mix_out(body, acts, x, consts, gf, wq, name):
    n = _rows_of(x)
    qw = wq.shape[1]
    row = lambda w: pl.BlockSpec((TOKEN_TILE, w), lambda i: (i, 0))
    specs, ops, layout = _row_inputs(list(acts) + [x])
    return pl.pallas_call(
        functools.partial(body, layout=layout),
        grid=(n // TOKEN_TILE,),
        in_specs=specs + [_full(c.shape) for c in consts] + [_full(gf.shape), _full(wq.shape)],
        out_specs=[row(D_MODEL), pl.BlockSpec((D_MODEL, TOKEN_TILE), lambda i: (0, i)), row(qw)],
        out_shape=[jax.ShapeDtypeStruct((n, D_MODEL), F32), jax.ShapeDtypeStruct((D_MODEL, n), BF16),
                   jax.ShapeDtypeStruct((n, qw), F32)],
        compiler_params=_params(("arbitrary",)),
        name=name,
    )(*ops, *consts, gf, wq)


def _head_norm(x, g, seg):
    xx = x * x
    hi = xx.astype(BF16)
    rest = xx - hi.astype(F32)
    mid = rest.astype(BF16)
    ms = _dot(hi, seg) + _dot(mid, seg) + _dot((rest - mid.astype(F32)).astype(BF16), seg)
    return x * lax.rsqrt(ms + NORM_EPS) * g


def _rope(x, cos, sin, first_half):
    half = ROT_DIM // 2
    width = x.shape[1]
    up = pltpu.roll(x, width - half, axis=1)
    dn = pltpu.roll(x, half, axis=1)
    return x * cos + jnp.where(first_half, -up, dn) * sin


def _rope_tables(c, s, reps):
    out = []
    for n in (1, reps):
        cn, sn = (jnp.concatenate([t] * n, axis=1) if n > 1 else t for t in (c, s))
        lane = lax.broadcasted_iota(jnp.int32, cn.shape, 1)
        out.append((cn, sn, (lane % HEAD_DIM) < (ROT_DIM // 2)))
    return out


def _stack_heads(x, g):
    return jnp.concatenate(
        [x[:, (g * KV_REP + r) * HEAD_DIM:(g * KV_REP + r + 1) * HEAD_DIM] for r in range(KV_REP)], axis=0)


def _attn_prompt_body(q_ref, k_ref, v_ref, gq_ref, gk_ref, inv_ref, segq_ref, segk_ref, sink_ref,
                      o_ref, kw_ref, vw_ref, kprev_ref, vprev_ref, cos_ref, sin_ref):
    j = pl.program_id(1)
    w = WINDOW

    @pl.when(j == 0)
    def _():
        kprev_ref[...] = jnp.zeros_like(kprev_ref)
        vprev_ref[...] = jnp.zeros_like(vprev_ref)
        within = lax.broadcasted_iota(jnp.int32, (w, LANES), 0).astype(F32) * inv_ref[...]
        cos_ref[...] = jnp.cos(within)
        sin_ref[...] = jnp.sin(within)

    start = (j * w).astype(F32) * inv_ref[...]
    cs, ss = jnp.cos(start), jnp.sin(start)
    cw, sw = cos_ref[...], sin_ref[...]
    (ck, sk, fk), (cq, sq, fq) = _rope_tables(cs * cw - ss * sw, ss * cw + cs * sw, QA // LANES)
    q = _rope(_head_norm(q_ref[...], gq_ref[...], segq_ref[...]), cq, sq, fq)
    k = _rope(_head_norm(k_ref[...], gk_ref[...], segk_ref[...]), ck, sk, fk)
    v = v_ref[...]
    kp = kprev_ref[...]
    vp = vprev_ref[...]

    qi = lax.broadcasted_iota(jnp.int32, (KV_REP * w, w), 0) % w
    ki = lax.broadcasted_iota(jnp.int32, (KV_REP * w, w), 1)
    m_own = ki <= qi
    m_prev = (ki > qi) & (j > 0)
    rep = lax.broadcasted_iota(jnp.int32, (KV_REP * w, 1), 0) // w
    for g in range(N_KV_A):
        sl = slice(g * HEAD_DIM, (g + 1) * HEAD_DIM)
        qg = _stack_heads(q, g).astype(BF16)
        s_own = jnp.where(m_own, _dot_nt(qg, k[:, sl].astype(BF16)) * ATTN_SCALE, NEG_INF)
        s_prev = jnp.where(m_prev, _dot_nt(qg, kp[:, sl].astype(BF16)) * ATTN_SCALE, NEG_INF)
        sink = jnp.zeros((KV_REP * w, 1), F32)
        for r in range(KV_REP):
            sink = jnp.where(rep == r, sink_ref[g * KV_REP + r], sink)
        mx = jnp.maximum(jnp.maximum(jnp.max(s_own, axis=-1, keepdims=True),
                                     jnp.max(s_prev, axis=-1, keepdims=True)), sink)
        e_own = jnp.exp(s_own - mx)
        e_prev = jnp.exp(s_prev - mx)
        den = (jnp.sum(e_own, axis=-1, keepdims=True) + jnp.sum(e_prev, axis=-1, keepdims=True)
               + jnp.exp(sink - mx))
        o = (_dot(e_own.astype(BF16), v[:, sl].astype(BF16))
             + _dot(e_prev.astype(BF16), vp[:, sl].astype(BF16))) / den
        for r in range(KV_REP):
            hq = g * KV_REP + r
            o_ref[:, hq * HEAD_DIM:(hq + 1) * HEAD_DIM] = o[r * w:(r + 1) * w]

    kprev_ref[...] = k
    vprev_ref[...] = v

    @pl.when(j == pl.num_programs(1) - 1)
    def _():
        kw_ref[0] = k
        vw_ref[0] = v


def _attn_prompt(z, batch, seq, gq, gk, inv, segq, segk, sinks):
    nb = seq // WINDOW
    blk = lambda width, col: pl.BlockSpec((WINDOW, width), lambda b, j: (b * nb + j, col))
    return pl.pallas_call(
        _attn_prompt_body,
        grid=(batch, nb),
        in_specs=[blk(QA, 0), blk(KVA, COL_K), blk(KVA, COL_V), _full(gq.shape), _full(gk.shape),
                  _full(inv.shape), _full(segq.shape), _full(segk.shape),
                  pl.BlockSpec(memory_space=pltpu.SMEM)],
        out_specs=[pl.BlockSpec((WINDOW, QA), lambda b, j: (b * nb + j, 0)),
                   pl.BlockSpec((1, WINDOW, KVA), lambda b, j: (b, 0, 0)),
                   pl.BlockSpec((1, WINDOW, KVA), lambda b, j: (b, 0, 0))],
        out_shape=[jax.ShapeDtypeStruct((batch * seq, QA), F32),
                   jax.ShapeDtypeStruct((batch, WINDOW, KVA), F32),
                   jax.ShapeDtypeStruct((batch, WINDOW, KVA), F32)],
        scratch_shapes=[pltpu.VMEM((WINDOW, KVA), F32), pltpu.VMEM((WINDOW, KVA), F32),
                        pltpu.VMEM((WINDOW, LANES), F32), pltpu.VMEM((WINDOW, LANES), F32)],
        compiler_params=_params(("arbitrary", "arbitrary")),
        name="swa_prompt",
    )(z, z, z, gq, gk, inv, segq, segk, sinks)


def _attn_decode_body(q_ref, k_ref, v_ref, kc_ref, vc_ref, gq_ref, gk_ref, inv_ref, segq_ref, segk_ref,
                      sink_ref, o_ref, kn_ref, vn_ref, *, dec_seq):
    rows_n = DEC_GROUP * dec_seq
    rows = lax.broadcasted_iota(jnp.int32, (rows_n, LANES), 0)
    pos = (PAST_LEN + rows % dec_seq).astype(F32)
    ang = pos * inv_ref[...]
    (ck, sk, fk), (cq, sq, fq) = _rope_tables(jnp.cos(ang), jnp.sin(ang), QA // LANES)
    q = _rope(_head_norm(q_ref[...], gq_ref[...], segq_ref[...]), cq, sq, fq)
    k = _rope(_head_norm(k_ref[...], gk_ref[...], segk_ref[...]), ck, sk, fk)
    v = v_ref[...]
    kn_ref[...] = k
    vn_ref[...] = v

    nq = KV_REP * rows_n
    nc = DEC_GROUP * WINDOW

    def qrow(shape):
        r = lax.broadcasted_iota(jnp.int32, shape, 0) % rows_n
        return r // dec_seq, r % dec_seq

    bq, tq = qrow((nq, nc))
    col = lax.broadcasted_iota(jnp.int32, (nq, nc), 1)
    m_cache = (col // WINDOW == bq) & (col % WINDOW > tq)
    bq, tq = qrow((nq, rows_n))
    col = lax.broadcasted_iota(jnp.int32, (nq, rows_n), 1)
    m_new = (col // dec_seq == bq) & (col % dec_seq <= tq)
    rep = lax.broadcasted_iota(jnp.int32, (nq, 1), 0) // rows_n
    for g in range(N_KV_A):
        sl = slice(g * HEAD_DIM, (g + 1) * HEAD_DIM)
        qg = _stack_heads(q, g).astype(BF16)
        s_c = jnp.where(m_cache, _dot_nt(qg, kc_ref[:, sl].astype(BF16)) * ATTN_SCALE, NEG_INF)
        s_n = jnp.where(m_new, _dot_nt(qg, k[:, sl].astype(BF16)) * ATTN_SCALE, NEG_INF)
        sink = jnp.zeros((nq, 1), F32)
        for r in range(KV_REP):
            sink = jnp.where(rep == r, sink_ref[g * KV_REP + r], sink)
        mx = jnp.maximum(jnp.maximum(jnp.max(s_c, axis=-1, keepdims=True),
                                     jnp.max(s_n, axis=-1, keepdims=True)), sink)
        e_c = jnp.exp(s_c - mx)
        e_n = jnp.exp(s_n - mx)
        den = jnp.sum(e_c, axis=-1, keepdims=True) + jnp.sum(e_n, axis=-1, keepdims=True) + jnp.exp(sink - mx)
        o = (_dot(e_c.astype(BF16), vc_ref[:, sl].astype(BF16))
             + _dot(e_n.astype(BF16), v[:, sl].astype(BF16))) / den
        for r in range(KV_REP):
            hq = g * KV_REP + r
            o_ref[:, hq * HEAD_DIM:(hq + 1) * HEAD_DIM] = o[r * rows_n:(r + 1) * rows_n]


def _attn_decode(z, row0, dec_batch, dec_seq, kc, vc, gq, gk, inv, segq, segk, sinks):
    rows_n = DEC_GROUP * dec_seq
    r0 = row0 // rows_n
    blk = lambda width, col: pl.BlockSpec((rows_n, width), lambda i: (r0 + i, col))
    cache = pl.BlockSpec((DEC_GROUP * WINDOW, KVA), lambda i: (i, 0))
    n = dec_batch * dec_seq
    return pl.pallas_call(
        functools.partial(_attn_decode_body, dec_seq=dec_seq),
        grid=(dec_batch // DEC_GROUP,),
        in_specs=[blk(QA, 0), blk(KVA, COL_K), blk(KVA, COL_V), cache, cache, _full(gq.shape),
                  _full(gk.shape), _full(inv.shape), _full(segq.shape), _full(segk.shape),
                  pl.BlockSpec(memory_space=pltpu.SMEM)],
        out_specs=[pl.BlockSpec((rows_n, QA), lambda i: (i, 0)), pl.BlockSpec((rows_n, KVA), lambda i: (i, 0)),
                   pl.BlockSpec((rows_n, KVA), lambda i: (i, 0))],
        out_shape=[jax.ShapeDtypeStruct((n, QA), F32), jax.ShapeDtypeStruct((n, KVA), F32),
                   jax.ShapeDtypeStruct((n, KVA), F32)],
        compiler_params=_params(("arbitrary",)),
        name="swa_decode",
    )(z, z, z, kc, vc, gq, gk, inv, segq, segk, sinks)


def _hgrn_lower_bound(lb_ref):
    l = lb_ref[...]
    e = jnp.exp(l - jnp.max(l, axis=0, keepdims=True))
    return e[0:1] / jnp.sum(e, axis=0, keepdims=True)


def _hgrn_gates(qb, fb, lb):
    logf = jnp.log(lb + (1.0 - lb) * jax.nn.sigmoid(fb))
    kb = (1.0 - lb) * jax.nn.sigmoid(-fb)
    return _silu(qb), kb, logf


def _hgrn_out(o, g, gate):
    return _rms(o, g) * _silu(gate)


def _hgrn_chunk(qh, kb, ih, logf, st, tril):
    c = qh.shape[0]
    hi = logf.astype(BF16)
    rest = logf - hi.astype(F32)
    mid = rest.astype(BF16)
    b = _dot(tril, hi) + _dot(tril, mid) + _dot(tril, (rest - mid.astype(F32)).astype(BF16))
    ones = jnp.ones((DK_B, LANES), BF16)
    st_b = st.astype(BF16)
    ih_b = ih.astype(BF16)
    o_parts = []
    row = lax.broadcasted_iota(jnp.int32, (c, 1), 0)
    trow = lax.broadcasted_iota(jnp.int32, (HGRN_SUB, 1), 0)
    for blk in range(c // HGRN_SUB):
        t0 = blk * HGRN_SUB
        bi = b[t0:t0 + HGRN_SUB]
        qi = qh[t0:t0 + HGRN_SUB]
        if blk == 0:
            oi = _dot_nt((qi * jnp.exp(bi)).astype(BF16), st_b)
        else:
            base = b[t0 - 1:t0]
            qd = qi * jnp.exp(bi - base)
            kd = jnp.where(row < t0, kb * jnp.exp(jnp.minimum(base - b, 0.0)), 0.0)
            a_off = _dot_nt(qd.astype(BF16), kd.astype(BF16))
            oi = _dot_nt((qd * jnp.exp(base)).astype(BF16), st_b) + _dot(a_off.astype(BF16), ih_b)
        prods = []
        for s in range(HGRN_SUB):
            e = jnp.exp(jnp.where(trow >= s, bi - bi[s:s + 1], 0.0))
            prods.append(qi * e * kb[t0 + s:t0 + s + 1])
        a_diag = _dot(jnp.concatenate(prods, axis=0).astype(BF16), ones)
        for s in range(HGRN_SUB):
            a = a_diag[s * HGRN_SUB:(s + 1) * HGRN_SUB]
            oi = oi + jnp.where(trow >= s, a, 0.0) * ih[t0 + s:t0 + s + 1]
        o_parts.append(oi)
    last = b[c - 1:c]
    st_new = st * jnp.exp(last) + _dot(ih.T.astype(BF16), (kb * jnp.exp(last - b)).astype(BF16))
    return jnp.concatenate(o_parts, axis=0), st_new


def _hgrn_prompt_body(qb_ref, fb_ref, ib_ref, gb_ref, lb_ref, go_ref, tril_ref, o_ref, s_ref, st_ref):
    tb = pl.program_id(2)

    @pl.when(tb == 0)
    def _():
        st_ref[...] = jnp.zeros_like(st_ref)

    tril = tril_ref[...]
    heads = [slice(h * DK_B, (h + 1) * DK_B) for h in range(HGRN_HEADS_PER_STEP)]
    lbs = [_hgrn_lower_bound(lb_ref.at[:, hs]) for hs in heads]
    sts = [st_ref[h] for h in range(HGRN_HEADS_PER_STEP)]
    for c in range(HGRN_ROWS // HGRN_CHUNK):
        rs = slice(c * HGRN_CHUNK, (c + 1) * HGRN_CHUNK)
        for h, hs in enumerate(heads):
            qh, kb, logf = _hgrn_gates(qb_ref[rs, hs], fb_ref[rs, hs], lbs[h])
            o, sts[h] = _hgrn_chunk(qh, kb, ib_ref[rs, hs], logf, sts[h], tril)
            o_ref[rs, hs] = _hgrn_out(o, go_ref[...], gb_ref[rs, hs])
    for h in range(HGRN_HEADS_PER_STEP):
        st_ref[h] = sts[h]

    @pl.when(tb == pl.num_programs(2) - 1)
    def _():
        for h in range(HGRN_HEADS_PER_STEP):
            s_ref[0, h] = sts[h].T


def _hgrn_prompt(z, batch, seq, lb, go, tril):
    nt = seq // HGRN_ROWS
    hp = HGRN_HEADS_PER_STEP
    assert N_HEADS_B % hp == 0 and all(c % hp == 0 for c in (COL_QB, COL_FB, COL_IB, COL_GB))
    blk = lambda col: pl.BlockSpec((HGRN_ROWS, hp * DK_B), lambda b, h, t: (b * nt + t, col // hp + h))
    return pl.pallas_call(
        _hgrn_prompt_body,
        grid=(batch, N_HEADS_B // hp, nt),
        in_specs=[blk(COL_QB), blk(COL_FB), blk(COL_IB), blk(COL_GB),
                  pl.BlockSpec((lb.shape[0], hp * DK_B), lambda b, h, t: (0, h)), _full(go.shape), _full(tril.shape)],
        out_specs=[pl.BlockSpec((HGRN_ROWS, hp * DV_B), lambda b, h, t: (b * nt + t, h)),
                   pl.BlockSpec((1, hp, DK_B, DV_B), lambda b, h, t: (b, h, 0, 0))],
        out_shape=[jax.ShapeDtypeStruct((batch * seq, VB), F32),
                   jax.ShapeDtypeStruct((batch, N_HEADS_B, DK_B, DV_B), F32)],
        scratch_shapes=[pltpu.VMEM((hp, DV_B, DK_B), F32)],
        compiler_params=_params(("arbitrary", "arbitrary", "arbitrary")),
        name="hgrn_prompt",
    )(z, z, z, z, lb, go, tril)


def _pad_rows(x, rows):
    return jnp.concatenate([x, jnp.zeros((rows - x.shape[0], x.shape[1]), x.dtype)], axis=0)


def _hgrn_decode_body(z_ref, s0_ref, lb_ref, go_ref, tril_ref, o_ref, s_ref, *, dec_seq):
    rows_n = z_ref.shape[0]
    groups = rows_n // dec_seq
    row = lax.broadcasted_iota(jnp.int32, (rows_n, 1), 0)
    for h in range(N_HEADS_B):
        col = lambda c0: slice((c0 + h) * LANES, (c0 + h + 1) * LANES)
        hs = slice(h * DK_B, (h + 1) * DK_B)
        lb = _hgrn_lower_bound(lb_ref.at[:, hs])
        qh, kb, logf = _hgrn_gates(z_ref[:, col(COL_QB)], z_ref[:, col(COL_FB)], lb)
        ih = z_ref[:, col(COL_IB)]
        b = _dot(tril_ref[...], _pad_rows(logf, LANES), HI)[:rows_n]
        b_t = _pad_rows(b, LANES).T
        ih_pad = _pad_rows(ih, LANES)
        o = jnp.zeros((rows_n, DV_B), F32)
        for e in range(groups):
            mine = (row // dec_seq) == e
            s0 = s0_ref[e, h]
            o = o + _dot(jnp.where(mine, qh * jnp.exp(b), 0.0), s0, HI)
            for s in range(dec_seq):
                r = e * dec_seq + s
                live = mine & (row >= r)
                ex = jnp.exp(jnp.where(live, b - b[r:r + 1], 0.0))
                a = jnp.sum(qh * ex * kb[r:r + 1], axis=-1, keepdims=True)
                o = o + jnp.where(live, a, 0.0) * ih[r:r + 1]
            r_last = (e + 1) * dec_seq - 1
            last = b[r_last:r_last + 1]
            kd = jnp.where(mine, kb * jnp.exp(jnp.minimum(last - b, 0.0)), 0.0)
            s_ref[e, h] = s0 * jnp.exp(b_t[:, r_last:r_last + 1]) + _dot(_pad_rows(kd, LANES).T, ih_pad, HI)
        o_ref[:, hs] = _hgrn_out(o, go_ref[...], z_ref[:, col(COL_GB)])


def _hgrn_decode(z, row0, dec_batch, dec_seq, s0, lb, go, tril):
    groups = 8 // dec_seq
    rows_n = groups * dec_seq
    r0 = row0 // rows_n
    st = pl.BlockSpec((groups, N_HEADS_B, DK_B, DV_B), lambda i: (i, 0, 0, 0))
    return pl.pallas_call(
        functools.partial(_hgrn_decode_body, dec_seq=dec_seq),
        grid=(dec_batch // groups,),
        in_specs=[pl.BlockSpec((rows_n, z.shape[1]), lambda i: (r0 + i, 0)), st,
                  _full(lb.shape), _full(go.shape), _full(tril.shape)],
        out_specs=[pl.BlockSpec((rows_n, VB), lambda i: (i, 0)), st],
        out_shape=[jax.ShapeDtypeStruct((dec_batch * dec_seq, VB), F32),
                   jax.ShapeDtypeStruct(s0.shape, F32)],
        compiler_params=_params(("arbitrary",)),
        name="hgrn_decode",
    )(z, s0, lb, go, tril)


def _conv_prompt_body(cur_ref, prev_ref, w_ref, b_ref, c_ref, ext_ref, sh_ref):
    t = pl.program_id(1)
    ext_ref[:CONV_HALO, :] = jnp.where(t > 0, prev_ref[...], 0.0)
    ext_ref[CONV_HALO:, :] = cur_ref[...]
    lead = CONV_HALO - (CONV_W - 1)
    chunk = CONV_CHUNK

    def cols(ci, carry):
        cs = pl.ds(pl.multiple_of(ci * LANES, LANES), LANES)
        for r in range(CONV_ROWS // chunk):
            acc = jnp.zeros((chunk, LANES), F32) + b_ref[:, cs]
            for res in range(SUBLANES):
                taps = range(res, CONV_W, SUBLANES)
                span = chunk + taps[-1] - res
                sh_ref[:span, :] = ext_ref[pl.ds(r * chunk + lead + res, span), cs]
                for w in taps:
                    acc = acc + sh_ref[w - res:w - res + chunk, :] * w_ref[pl.ds(w, 1), cs]
            c_ref[pl.ds(r * chunk, chunk), cs] = acc
        return carry

    lax.fori_loop(0, D_MODEL // LANES, cols, 0)


def _conv_prompt(u, batch, seq, w, b):
    nt = seq // CONV_ROWS
    per = CONV_ROWS // CONV_HALO
    return pl.pallas_call(
        _conv_prompt_body,
        grid=(batch, nt),
        in_specs=[pl.BlockSpec((CONV_ROWS, D_MODEL), lambda bi, t: (bi * nt + t, 0)),
                  pl.BlockSpec((CONV_HALO, D_MODEL), lambda bi, t: (jnp.maximum((bi * nt + t) * per - 1, 0), 0)),
                  _full(w.shape), _full(b.shape)],
        out_specs=pl.BlockSpec((CONV_ROWS, D_MODEL), lambda bi, t: (bi * nt + t, 0)),
        out_shape=jax.ShapeDtypeStruct((batch * seq, D_MODEL), F32),
        scratch_shapes=[pltpu.VMEM((CONV_HALO + CONV_ROWS, D_MODEL), F32),
                        pltpu.VMEM((CONV_CHUNK + CONV_HALO, LANES), F32)],
        compiler_params=_params(("arbitrary", "arbitrary")),
        name="conv_prompt",
    )(u, u, w, b)


def _conv_decode_body(u_ref, st_ref, w_ref, wshift_ref, b_ref, c_ref, *, dec_seq):
    for e in range(DEC_GROUP):
        past = st_ref[e]
        for t in range(dec_seq):
            acc = jnp.sum(past * wshift_ref[t], axis=0, keepdims=True) + b_ref[...]
            for t2 in range(t + 1):
                wi = CONV_W - 1 - t + t2
                acc = acc + u_ref[e * dec_seq + t2:e * dec_seq + t2 + 1, :] * w_ref[wi:wi + 1, :]
            c_ref[e * dec_seq + t:e * dec_seq + t + 1, :] = acc


def _conv_decode(u, row0, dec_batch, dec_seq, state, w, wshift, b):
    rows_n = DEC_GROUP * dec_seq
    r0 = row0 // rows_n
    return pl.pallas_call(
        functools.partial(_conv_decode_body, dec_seq=dec_seq),
        grid=(dec_batch // DEC_GROUP,),
        in_specs=[pl.BlockSpec((rows_n, D_MODEL), lambda i: (r0 + i, 0)),
                  pl.BlockSpec((DEC_GROUP, CONV_W - 1, D_MODEL), lambda i: (i, 0, 0)),
                  _full(w.shape), _full(wshift.shape), _full(b.shape)],
        out_specs=pl.BlockSpec((rows_n, D_MODEL), lambda i: (i, 0)),
        out_shape=jax.ShapeDtypeStruct((dec_batch * dec_seq, D_MODEL), F32),
        compiler_params=_params(("arbitrary",)),
        name="conv_decode",
    )(u, state, w, wshift, b)


def _extract_top(s, idx, count, none_rank):
    work = s
    rank = jnp.full(s.shape, none_rank, F32)
    vals = []
    for r in range(count):
        m = jnp.max(work, axis=0, keepdims=True)
        first = jnp.min(jnp.where(work == m, idx, np.float32(1e9)), axis=0, keepdims=True)
        hit = idx == first
        rank = jnp.where(hit, np.float32(r), rank)
        work = jnp.where(hit, -jnp.inf, work)
        vals.append(m)
    return rank, vals


_PEER_CAND = [(a, b) for a in range(PEER_TOPK) for b in range(PEER_TOPK) if (a + 1) * (b + 1) <= PEER_TOPK]
_PEER_CAND_ROWS = -(-len(_PEER_CAND) // 8) * 8


def _extract_by_value(s, count, want_rank=True):
    work = s
    rank = jnp.full(s.shape, np.float32(count), F32) if want_rank else None
    vals = []
    for r in range(count):
        m = jnp.max(work, axis=0, keepdims=True)
        hit = work == m
        if want_rank:
            rank = jnp.where(hit, np.float32(r), rank)
        work = jnp.where(hit, -jnp.inf, work)
        vals.append(m)
    done = rank < count if want_rank else work == -jnp.inf
    return rank, vals, jnp.sum(jnp.where(done, 1.0, 0.0), axis=0, keepdims=True)


def _peer_select_body(q_ref, keys_ref, rank_ref, qe_ref, lq_ref, pe_ref):
    ts = q_ref.shape[0]
    kidx = lax.broadcasted_iota(jnp.int32, (N_KEYS, ts), 0).astype(F32)
    cidx = lax.broadcasted_iota(jnp.int32, (_PEER_CAND_ROWS, ts), 0).astype(F32)
    tidx = lax.broadcasted_iota(jnp.int32, (PEER_TOPK, ts), 0).astype(F32)
    bidx = lax.broadcasted_iota(jnp.int32, (SUBLANES, ts), 0).astype(F32)
    pad =jnp.full((_PEER_CAND_ROWS, ts), -jnp.inf, F32)

    def tables(h, s0, s1, exact):
        if exact:
            rank0, v0 = _extract_top(s0, kidx, PEER_TOPK, PEER_TOPK)
            rank1, v1 = _extract_top(s1, kidx, PEER_TOPK, PEER_TOPK)
        else:
            _, v0, n0 = _extract_by_value(s0, PEER_TOPK, want_rank=False)
            rank1, v1, n1 = _extract_by_value(s1, PEER_TOPK)
        if exact:
            pairs = list(_PEER_CAND)
            cand = pad
            for ci, (a, b) in enumerate(pairs):
                cand = jnp.where(cidx == ci, v0[a] + v1[b], cand)
            crank, _ = _extract_top(cand, cidx, PEER_TOPK, PEER_TOPK)
            off = None
        else:
            col0 = jnp.zeros((PEER_TOPK, ts), F32)
            col1 = jnp.zeros((PEER_TOPK, ts), F32)
            low1 = jnp.zeros((SUBLANES, ts), F32)
            for r in range(PEER_TOPK):
                col0 = jnp.where(tidx == r, v0[r], col0)
                col1 = jnp.where(tidx == r, v1[r], col1)
                if r < SUBLANES:
                    low1 = jnp.where(bidx == r, v1[r], low1)
            ninf = np.float32(-np.inf)
            groups = [v0[0] + col1, jnp.where(tidx == 0, ninf, col0 + v1[0])]
            pairs = [(0, b) for b in range(PEER_TOPK)] + [(a, 0) if a else None for a in range(PEER_TOPK)]
            rest = [p for p in _PEER_CAND if p[0] and p[1]]
            for a in sorted({p[0] for p in rest}):
                mine = [p for p in rest if p[0] == a]
                if len(mine) > 2:
                    groups.append(jnp.where((bidx >= 1) & (bidx <= len(mine)), v0[a] + low1, ninf))
                    pairs += [(a, b) if 1 <= b <= len(mine) else None for b in range(SUBLANES)]
            loose = [p for p in rest if p not in pairs]
            assert len(loose) <= SUBLANES
            tail = jnp.full((SUBLANES, ts), ninf, F32)
            for r, (a, b) in enumerate(loose):
                tail = jnp.where(bidx == r, v0[a] + v1[b], tail)
            groups.append(tail)
            pairs += loose + [None] * (SUBLANES - len(loose))
            assert sorted(p for p in pairs if p) == sorted(_PEER_CAND)
            cand = jnp.concatenate(groups, axis=0)
            crank, _, nc = _extract_by_value(cand, PEER_TOPK)
            off = jnp.max(jnp.abs(n0 - PEER_TOPK) + jnp.abs(n1 - PEER_TOPK) + jnp.abs(nc - PEER_TOPK))
        picked = jnp.where(crank < PEER_TOPK, 1.0, 0.0)
        e = picked * jnp.exp(cand - (v0[0] + v1[0]))
        z = jnp.sum(e, axis=0, keepdims=True)
        lq = jnp.zeros((N_KEYS, ts), F32)
        for a in range(PEER_TOPK):
            cnt = jnp.zeros((1, ts), F32)
            for ci, p in enumerate(pairs):
                if p is not None and p[0] == a:
                    cnt = cnt + picked[ci:ci + 1]
            lq = jnp.where((rank0 == a) if exact else (s0 == v0[a]), cnt, lq)
        rank_b = rank1.astype(BF16)
        qe_b = jnp.exp(s1 - v1[0]).astype(BF16)
        for k in range(N_KEYS // BF16_ROWS):
            rank_ref[h, k] = pltpu.bitcast(rank_b[k * BF16_ROWS:(k + 1) * BF16_ROWS], jnp.uint32)
            qe_ref[h, k] = pltpu.bitcast(qe_b[k * BF16_ROWS:(k + 1) * BF16_ROWS], jnp.uint32)
        lq_ref[h] = lq
        pe_ref[h] = jnp.exp(s0 - v0[0]) / z
        return off

    def head(h, carry):
        c0 = pl.multiple_of(h * 2 * PEER_DKH, 2 * PEER_DKH)
        q0 = q_ref[:, pl.ds(c0, PEER_DKH)].astype(BF16)
        q1 = q_ref[:, pl.ds(c0 + PEER_DKH, PEER_DKH)].astype(BF16)
        s0 = _dot_nt(keys_ref[h, 0], q0)
        s1 = _dot_nt(keys_ref[h, 1], q1)
        off = tables(h, s0, s1, exact=False)

        @pl.when(off > 0.5)
        def _():
            tables(h, s0, s1, exact=True)

        return carry

    lax.fori_loop(0, PEER_HEADS, head, 0)


def _peer_select(q, keys):
    n = q.shape[0]
    out = jax.ShapeDtypeStruct((PEER_HEADS, N_KEYS, n), F32)
    ospec = pl.BlockSpec((PEER_HEADS, N_KEYS, SEL_TILE), lambda i: (0, 0, i))
    packed = (PEER_HEADS, N_KEYS // BF16_ROWS, BF16_ROWS // 2)
    out_b = jax.ShapeDtypeStruct(packed + (n,), jnp.uint32)
    ospec_b = pl.BlockSpec(packed + (SEL_TILE,), lambda i: (0, 0, 0, i))
    return pl.pallas_call(
        _peer_select_body,
        grid=(n // SEL_TILE,),
        in_specs=[pl.BlockSpec((SEL_TILE, q.shape[1]), lambda i: (i, 0)), _full(keys.shape)],
        out_specs=[ospec_b, ospec_b, ospec, ospec],
        out_shape=[out_b, out_b, out, out],
        compiler_params=_params(("arbitrary",)),
        name="peer_select",
    )(q, keys)


def _peer_dense_body(ht_ref, u_ref, vt_ref, rank_ref, qe_ref, lq_ref, pe_ref, x_ref, *rest, prompt_tiles):
    *o_ref, acc_ref, a_ref, w_ref = rest
    step = pl.program_id(1)

    @pl.when(step == 0)
    def _():
        acc_ref[...] = jnp.zeros_like(acc_ref)

    @pl.when(step >= 0)
    def _():
        quarter = PEER_ROWS * N_KEYS // 4
        for qi in range(4):
            rows = slice(qi * quarter, (qi + 1) * quarter)
            a_ref[rows, :] = _dot(u_ref[rows, :], ht_ref[...])

    @pl.when(step >= 0)
    def _():
        sub = (N_KEYS // BF16_ROWS, BF16_ROWS, LANES)
        for il in range(PEER_ROWS):
            for ci in range(TOKEN_TILE // LANES):
                cs = slice(ci * LANES, (ci + 1) * LANES)
                gate = jnp.zeros(sub, BF16)
                for h in range(PEER_HEADS):
                    lq = jnp.broadcast_to(lq_ref[h, il:il + 1, cs], sub[1:]).astype(BF16)
                    pe = jnp.broadcast_to(pe_ref[h, il:il + 1, cs], sub[1:]).astype(BF16)
                    take = pltpu.bitcast(rank_ref[h, :, :, cs], BF16) < lq[None]
                    qe = pltpu.bitcast(qe_ref[h, :, :, cs], BF16)
                    gate = gate + jnp.where(take, qe, jnp.zeros(sub, BF16)) * pe[None]
                for k in range(sub[0]):
                    rs = slice(il * N_KEYS + k * BF16_ROWS, il * N_KEYS + (k + 1) * BF16_ROWS)
                    w_ref[rs, cs] = gate[k] * _gelu(a_ref[rs, cs].astype(BF16))
        acc_ref[...] += _dot(vt_ref[0], w_ref[...])

    last = step == pl.num_programs(1) - 1
    if prompt_tiles is None:
        @pl.when(last)
        def _():
            o_ref[0][...] = x_ref[...] + acc_ref[...].T
    else:
        tile = pl.program_id(0)
        o_prompt, o_sample = o_ref

        @pl.when(last & (tile < prompt_tiles))
        def _():
            o_prompt[...] = x_ref[...] + acc_ref[...].T

        @pl.when(last & (tile >= prompt_tiles))
        def _():
            o_sample[...] = x_ref[...] + acc_ref[...].T


def _peer_dense(ht, u, vt, rank, qe, lq, pe, x, n_prompt=None):
    n = x.shape[0]
    ex = PEER_ROWS * N_KEYS
    steps = N_KEYS // PEER_ROWS
    tok = pl.BlockSpec((PEER_HEADS, N_KEYS // BF16_ROWS, BF16_ROWS // 2, TOKEN_TILE), lambda t, e: (0, 0, 0, t))
    rowsel = pl.BlockSpec((PEER_HEADS, PEER_ROWS, TOKEN_TILE), lambda t, e: (0, e, t))
    if n_prompt is None:
        tp = None
        out_specs = pl.BlockSpec((TOKEN_TILE, D_MODEL), lambda t, e: (t, 0))
        out_shape = jax.ShapeDtypeStruct((n, D_MODEL), F32)
    else:
        tp = n_prompt // TOKEN_TILE
        out_specs = [pl.BlockSpec((TOKEN_TILE, D_MODEL), lambda t, e: (jnp.minimum(t, tp - 1), 0)),
                     pl.BlockSpec((TOKEN_TILE, D_MODEL), lambda t, e: (jnp.maximum(t - tp, 0), 0))]
        out_shape = [jax.ShapeDtypeStruct((n_prompt, D_MODEL), F32),
                     jax.ShapeDtypeStruct((n - n_prompt, D_MODEL), F32)]
    return pl.pallas_call(
        functools.partial(_peer_dense_body, prompt_tiles=tp),
        grid=(n // TOKEN_TILE, steps),
        in_specs=[pl.BlockSpec((D_MODEL, TOKEN_TILE), lambda t, e: (0, t)),
                  pl.BlockSpec((ex, D_MODEL), lambda t, e: (e, 0)),
                  pl.BlockSpec((1, D_MODEL, ex), lambda t, e: (e, 0, 0)),
                  tok, tok, rowsel, rowsel,
                  pl.BlockSpec((TOKEN_TILE, D_MODEL), lambda t, e: (t, 0))],
        out_specs=out_specs,
        out_shape=out_shape,
        scratch_shapes=[pltpu.VMEM((D_MODEL, TOKEN_TILE), F32), pltpu.VMEM((ex, TOKEN_TILE), F32),
                        pltpu.VMEM((ex, TOKEN_TILE), BF16)],
        compiler_params=_params(("arbitrary", "arbitrary")),
        name="peer_dense",
    )(ht, u, vt, rank, qe, lq, pe, x)


def _peer(x1, ht, q, keys, u, vt, n_prompt=None):
    rank, qe, lq, pe = _peer_select(q, keys)
    return _peer_dense(ht, u, vt, rank, qe, lq, pe, x1, n_prompt)


def _value_blocks(v):
    ex = PEER_ROWS * N_KEYS
    return v.astype(BF16).reshape(v.shape[0] // ex, ex, v.shape[1]).transpose(0, 2, 1)


def _block_diag_mean(width):
    idx = np.arange(width) // HEAD_DIM
    return jnp.asarray((idx[:, None] == idx[None, :]).astype(np.float32) / HEAD_DIM, BF16)


def kernel(x_prompt, x_sample, cache_swa_k, cache_swa_v, state_hgrn, state_conv, norm_mix_g, norm_ffn_g, w_in0, attn_q_norm_g, attn_k_norm_g, attn_sinks, hgrn_lb, hgrn_o_norm_g, w_out0, conv_w_pw1, conv_b_pw1, conv_w_dw, conv_b_dw, conv_ln_g, conv_ln_b, conv_w_pw2, peer_w_q, peer_keys, peer_u, peer_v):
    batch, seq, _ = x_prompt.shape
    dec_batch, dec_seq, _ = x_sample.shape
    n_p = batch * seq
    n_s = dec_batch * dec_seq
    assert seq % CONV_ROWS == 0 and seq % HGRN_ROWS == 0 and n_p % TOKEN_TILE == 0 and n_s % TOKEN_TILE == 0
    assert dec_batch % DEC_GROUP == 0 and 8 % dec_seq == 0 and dec_seq <= CONV_W - 1

    x = (x_prompt.reshape(n_p, D_MODEL), x_sample.reshape(n_s, D_MODEL))
    row = lambda v: v.reshape(1, -1).astype(F32)

    z = _in0(x, row(norm_mix_g[0]), w_in0.astype(BF16))
    d = np.arange(LANES) % HEAD_DIM
    inv = ROPE_THETA ** (-jnp.arange(0, ROT_DIM, 2, dtype=F32) / ROT_DIM)
    inv_lane = jnp.where(jnp.asarray(d < ROT_DIM), inv[d % (ROT_DIM // 2)], 0.0).reshape(1, LANES)
    gq = row(jnp.tile(attn_q_norm_g, N_HEADS_A))
    gk = row(jnp.tile(attn_k_norm_g, N_KV_A))
    segq, segk = _block_diag_mean(QA), _block_diag_mean(KVA)
    sinks = attn_sinks.astype(F32)
    attn_p, k_win_p, v_win_p = _attn_prompt(z, batch, seq, gq, gk, inv_lane, segq, segk, sinks)
    kc = cache_swa_k.reshape(dec_batch * WINDOW, KVA)
    vc = cache_swa_v.reshape(dec_batch * WINDOW, KVA)
    attn_s, k_new, v_new = _attn_decode(z, n_p, dec_batch, dec_seq, kc, vc, gq, gk, inv_lane, segq, segk, sinks)

    go = row(hgrn_o_norm_g)
    tril = jnp.asarray(np.tril(np.ones((HGRN_CHUNK, HGRN_CHUNK), np.float32)), BF16)
    hg_p, s_p = _hgrn_prompt(z, batch, seq, hgrn_lb, go, tril)
    r = np.arange(LANES)
    tril_dec = jnp.asarray(((r[:, None] // dec_seq == r[None, :] // dec_seq) & (r[None, :] <= r[:, None]))
                           .astype(np.float32))
    hg_s, s_s = _hgrn_decode(z, n_p, dec_batch, dec_seq, state_hgrn, hgrn_lb, go, tril_dec)

    x1, ht, q = _mix_out(_out0_body, [(attn_p, attn_s), (hg_p, hg_s)], x, [w_out0.astype(BF16)], row(norm_ffn_g[0]),
                         peer_w_q[0].astype(BF16), "mix0_out")
    x2 = _peer(x1, ht, q, peer_keys[0].astype(BF16), peer_u[0].astype(BF16), _value_blocks(peer_v[0]))

    u = _glu(x2, row(norm_mix_g[1]), conv_w_pw1.astype(BF16), row(conv_b_pw1))
    wdw = jnp.concatenate([conv_w_dw, jnp.zeros((1, D_MODEL), F32)], axis=0)
    bdw = row(conv_b_dw)
    c_p = _conv_prompt(u, batch, seq, wdw, bdw)
    wshift = jnp.stack([jnp.concatenate([jnp.zeros((t, D_MODEL), F32), conv_w_dw[:CONV_W - 1 - t]], axis=0)
                        for t in range(dec_seq)])
    c_s = _conv_decode(u, n_p, dec_batch, dec_seq, state_conv, wdw, wshift, bdw)
    x3, ht, q = _mix_out(_post1_body, [(c_p, c_s)], x2, [row(conv_ln_g), row(conv_ln_b), conv_w_pw2.astype(BF16)],
                         row(norm_ffn_g[1]), peer_w_q[1].astype(BF16), "mix1_out")
    y_p, y_s = _peer(x3, ht, q, peer_keys[1].astype(BF16), peer_u[1].astype(BF16),
                     _value_blocks(peer_v[1]), n_p)

    kv = lambda t: t.reshape(t.shape[0], WINDOW, N_KV_A, HEAD_DIM)
    k_win_s = jnp.concatenate([cache_swa_k[:, dec_seq:], k_new.reshape(dec_batch, dec_seq, N_KV_A, HEAD_DIM)], axis=1)
    v_win_s = jnp.concatenate([cache_swa_v[:, dec_seq:], v_new.reshape(dec_batch, dec_seq, N_KV_A, HEAD_DIM)], axis=1)
    u_p = u[:n_p].reshape(batch, seq, D_MODEL)
    u_s = u[n_p:].reshape(dec_batch, dec_seq, D_MODEL)
    conv_buf_p = u_p[:, seq - (CONV_W - 1):]
    conv_buf_s = jnp.concatenate([state_conv[:, dec_seq:], u_s], axis=1)
    return (y_p.reshape(batch, seq, D_MODEL), y_s.reshape(dec_batch, dec_seq, D_MODEL),
            kv(k_win_p), kv(v_win_p), s_p, conv_buf_p, k_win_s, v_win_s, s_s, conv_buf_s)
```

```python
import functools

import numpy as np
import jax
import jax.numpy as jnp
from jax import lax
from jax.experimental import pallas as pl
from jax.experimental.pallas import tpu as pltpu

F32 = jnp.float32
BF16 = jnp.bfloat16
HI = lax.Precision.HIGHEST

D_MODEL = 1024
PAST_LEN = 8192
HEAD_DIM = 64
N_HEADS_A = 8
N_KV_A = 2
KV_REP = N_HEADS_A // N_KV_A
WINDOW = 128
ROT_DIM = HEAD_DIM // 4
ROPE_THETA = 500000.0
ATTN_SCALE = HEAD_DIM ** -0.5
NEG_INF = -1e30
N_HEADS_B = 4
DK_B = 128
DV_B = 128
CONV_W = 31
N_KEYS = 128
PEER_HEADS = 8
PEER_TOPK = 16
PEER_DKH = 128
NORM_EPS = 1e-6

QA = N_HEADS_A * HEAD_DIM
KVA = N_KV_A * HEAD_DIM
QB = N_HEADS_B * DK_B
VB = N_HEADS_B * DV_B
IN0_WIDTH = QA + 2 * KVA + 2 * QB + 2 * VB
COL_K = QA // 128
COL_V = (QA + KVA) // 128
COL_QB = (QA + 2 * KVA) // 128
COL_FB = COL_QB + QB // 128
COL_IB = COL_FB + QB // 128
COL_GB = COL_IB + VB // 128

LANES = 128
SUBLANES = 8
TOKEN_TILE = 512
SEL_TILE = 256
HGRN_CHUNK = 64
HGRN_SUB = 16
HGRN_ROWS = 256
HGRN_HEADS_PER_STEP = 2
CONV_ROWS = 512
CONV_HALO = 32
CONV_CHUNK = 128
PEER_ROWS = 16
BF16_ROWS = 16
DEC_GROUP = 8
VMEM_LIMIT = 48 * 1024 * 1024


def _dot(a, b, prec=None):
    return jnp.dot(a, b, preferred_element_type=F32, precision=prec)


def _dot_nt(a, b, prec=None):
    return lax.dot_general(a, b, (((1,), (1,)), ((), ())), preferred_element_type=F32, precision=prec)


def _rms(x, g):
    return x * lax.rsqrt(jnp.mean(x * x, axis=-1, keepdims=True) + NORM_EPS) * g


def _silu(x):
    return x * jax.nn.sigmoid(x)


def _gelu_doubled(x):
    return x * (1.0 + lax.erf(x * 0.7071067811865476))


def _params(sem, flags=None):
    return pltpu.CompilerParams(dimension_semantics=sem, vmem_limit_bytes=VMEM_LIMIT, flags=flags)


def _full(shape):
    n = len(shape)
    return pl.BlockSpec(shape, lambda *_: (0,) * n)


def _row_inputs(arrs):
    specs, ops, layout = [], [], []
    for a in arrs:
        if isinstance(a, tuple):
            tp = a[0].shape[0] // TOKEN_TILE
            width = a[0].shape[1]
            specs += [pl.BlockSpec((TOKEN_TILE, width), lambda i, tp=tp: (jnp.minimum(i, tp - 1), 0)),
                      pl.BlockSpec((TOKEN_TILE, width), lambda i, tp=tp: (jnp.maximum(i - tp, 0), 0))]
            ops += list(a)
            layout.append(tp)
        else:
            specs.append(pl.BlockSpec((TOKEN_TILE, a.shape[1]), lambda i: (i, 0)))
            ops.append(a)
            layout.append(None)
    return specs, ops, tuple(layout)


def _row_values(layout, refs):
    i = pl.program_id(0)
    vals, k = [], 0
    for tp in layout:
        if tp is None:
            vals.append(refs[k][...])
            k += 1
        else:
            vals.append(jnp.where(i < tp, refs[k][...], refs[k + 1][...]))
            k += 2
    return vals, refs[k:]


def _rows_of(a):
    return a[0].shape[0] + a[1].shape[0] if isinstance(a, tuple) else a.shape[0]


def _in0_body(*refs, layout):
    (x,), (g_ref, w_ref, z_ref) = _row_values(layout, refs)
    h = _rms(x, g_ref[...]).astype(BF16)
    z_ref[...] = _dot(h, w_ref[...])


def _in0(x, g, w):
    n = _rows_of(x)
    width = w.shape[1]
    specs, ops, layout = _row_inputs([x])
    return pl.pallas_call(
        functools.partial(_in0_body, layout=layout),
        grid=(n // TOKEN_TILE,),
        in_specs=specs + [_full((1, D_MODEL)), _full((D_MODEL, width))],
        out_specs=pl.BlockSpec((TOKEN_TILE, width), lambda i: (i, 0)),
        out_shape=jax.ShapeDtypeStruct((n, width), F32),
        compiler_params=_params(("arbitrary",)),
        name="in0_proj",
    )(*ops, g, w)


def _glu_body(x_ref, g_ref, w_ref, b_ref, u_ref):
    h = _rms(x_ref[...], g_ref[...]).astype(BF16)
    a = _dot(h, w_ref[...]) + b_ref[...]
    u_ref[...] = a[:, :D_MODEL] * jax.nn.sigmoid(a[:, D_MODEL:])


def _glu(x, g, w, b):
    n = x.shape[0]
    return pl.pallas_call(
        _glu_body,
        grid=(n // TOKEN_TILE,),
        in_specs=[pl.BlockSpec((TOKEN_TILE, D_MODEL), lambda i: (i, 0)), _full((1, D_MODEL)),
                  _full((D_MODEL, 2 * D_MODEL)), _full((1, 2 * D_MODEL))],
        out_specs=pl.BlockSpec((TOKEN_TILE, D_MODEL), lambda i: (i, 0)),
        out_shape=jax.ShapeDtypeStruct((n, D_MODEL), F32),
        compiler_params=_params(("arbitrary",)),
        name="conv_glu",
    )(x, g, w, b)


def _ffn_query(x1, gf_ref, wq_ref, x1_ref, ht_ref, q_ref):
    x1_ref[...] = x1
    h2 = _rms(x1, gf_ref[...])
    ht_ref[...] = h2.T.astype(BF16)
    q_ref[...] = _dot(h2.astype(BF16), wq_ref[...])


def _out0_body(*refs, layout):
    (attn, hg, x), (w_ref, gf_ref, wq_ref, x1_ref, ht_ref, q_ref) = _row_values(layout, refs)
    m = _dot(attn.astype(BF16), w_ref[:QA, :]) + _dot(hg.astype(BF16), w_ref[QA:, :])
    _ffn_query(x + m, gf_ref, wq_ref, x1_ref, ht_ref, q_ref)


def _post1_body(*refs, layout):
    (c, x), (lg_ref, lb_ref, w_ref, gf_ref, wq_ref, x1_ref, ht_ref, q_ref) = _row_values(layout, refs)
    mu = jnp.mean(c, axis=-1, keepdims=True)
    d = c - mu
    var = jnp.mean(d * d, axis=-1, keepdims=True)
    ln = d * lax.rsqrt(var + NORM_EPS) * lg_ref[...] + lb_ref[...]
    y = _dot(_silu(ln).astype(BF16), w_ref[...])
    _ffn_query(x + y, gf_ref, wq_ref, x1_ref, ht_ref, q_ref)


def _mix_out(body, acts, x, consts, gf, wq, name):
    n = _rows_of(x)
    qw = wq.shape[1]
    row = lambda w: pl.BlockSpec((TOKEN_TILE, w), lambda i: (i, 0))
    specs, ops, layout = _row_inputs(list(acts) + [x])
    return pl.pallas_call(
        functools.partial(body, layout=layout),
        grid=(n // TOKEN_TILE,),
        in_specs=specs + [_full(c.shape) for c in consts] + [_full(gf.shape), _full(wq.shape)],
        out_specs=[row(D_MODEL), pl.BlockSpec((D_MODEL, TOKEN_TILE), lambda i: (0, i)), row(qw)],
        out_shape=[jax.ShapeDtypeStruct((n, D_MODEL), F32), jax.ShapeDtypeStruct((D_MODEL, n), BF16),
                   jax.ShapeDtypeStruct((n, qw), F32)],
        compiler_params=_params(("arbitrary",)),
        name=name,
    )(*ops, *consts, gf, wq)


def _head_norm(x, g, seg):
    xx = x * x
    hi = xx.astype(BF16)
    rest = xx - hi.astype(F32)
    mid = rest.astype(BF16)
    ms = _dot(hi, seg) + _dot(mid, seg) + _dot((rest - mid.astype(F32)).astype(BF16), seg)
    return x * lax.rsqrt(ms + NORM_EPS) * g


def _rope(x, cos, sin, first_half):
    half = ROT_DIM // 2
    width = x.shape[1]
    up = pltpu.roll(x, width - half, axis=1)
    dn = pltpu.roll(x, half, axis=1)
    return x * cos + jnp.where(first_half, -up, dn) * sin


def _rope_tables(c, s, reps):
    out = []
    for n in (1, reps):
        cn, sn = (jnp.concatenate([t] * n, axis=1) if n > 1 else t for t in (c, s))
        lane = lax.broadcasted_iota(jnp.int32, cn.shape, 1)
        out.append((cn, sn, (lane % HEAD_DIM) < (ROT_DIM // 2)))
    return out


def _stack_heads(x, g):
    return jnp.concatenate(
        [x[:, (g * KV_REP + r) * HEAD_DIM:(g * KV_REP + r + 1) * HEAD_DIM] for r in range(KV_REP)], axis=0)


def _attn_prompt_body(q_ref, k_ref, v_ref, gq_ref, gk_ref, inv_ref, segq_ref, segk_ref, sink_ref,
                      o_ref, kw_ref, vw_ref, kprev_ref, vprev_ref, cos_ref, sin_ref):
    j = pl.program_id(1)
    w = WINDOW

    @pl.when(j == 0)
    def _():
        kprev_ref[...] = jnp.zeros_like(kprev_ref)
        vprev_ref[...] = jnp.zeros_like(vprev_ref)
        within = lax.broadcasted_iota(jnp.int32, (w, LANES), 0).astype(F32) * inv_ref[...]
        cos_ref[...] = jnp.cos(within)
        sin_ref[...] = jnp.sin(within)

    start = (j * w).astype(F32) * inv_ref[...]
    cs, ss = jnp.cos(start), jnp.sin(start)
    cw, sw = cos_ref[...], sin_ref[...]
    (ck, sk, fk), (cq, sq, fq) = _rope_tables(cs * cw - ss * sw, ss * cw + cs * sw, QA // LANES)
    q = _rope(_head_norm(q_ref[...], gq_ref[...], segq_ref[...]), cq, sq, fq)
    k = _rope(_head_norm(k_ref[...], gk_ref[...], segk_ref[...]), ck, sk, fk)
    v = v_ref[...]
    kp = kprev_ref[...]
    vp = vprev_ref[...]

    qi = lax.broadcasted_iota(jnp.int32, (KV_REP * w, w), 0) % w
    ki = lax.broadcasted_iota(jnp.int32, (KV_REP * w, w), 1)
    m_own = ki <= qi
    m_prev = (ki > qi) & (j > 0)
    rep = lax.broadcasted_iota(jnp.int32, (KV_REP * w, 1), 0) // w
    for g in range(N_KV_A):
        sl = slice(g * HEAD_DIM, (g + 1) * HEAD_DIM)
        qg = _stack_heads(q, g).astype(BF16)
        s_own = jnp.where(m_own, _dot_nt(qg, k[:, sl].astype(BF16)) * ATTN_SCALE, NEG_INF)
        s_prev = jnp.where(m_prev, _dot_nt(qg, kp[:, sl].astype(BF16)) * ATTN_SCALE, NEG_INF)
        sink = jnp.zeros((KV_REP * w, 1), F32)
        for r in range(KV_REP):
            sink = jnp.where(rep == r, sink_ref[g * KV_REP + r], sink)
        mx = jnp.maximum(jnp.maximum(jnp.max(s_own, axis=-1, keepdims=True),
                                     jnp.max(s_prev, axis=-1, keepdims=True)), sink)
        e_own = jnp.exp(s_own - mx)
        e_prev = jnp.exp(s_prev - mx)
        den = (jnp.sum(e_own, axis=-1, keepdims=True) + jnp.sum(e_prev, axis=-1, keepdims=True)
               + jnp.exp(sink - mx))
        o = (_dot(e_own.astype(BF16), v[:, sl].astype(BF16))
             + _dot(e_prev.astype(BF16), vp[:, sl].astype(BF16))) / den
        for r in range(KV_REP):
            hq = g * KV_REP + r
            o_ref[:, hq * HEAD_DIM:(hq + 1) * HEAD_DIM] = o[r * w:(r + 1) * w]

    kprev_ref[...] = k
    vprev_ref[...] = v

    @pl.when(j == pl.num_programs(1) - 1)
    def _():
        kw_ref[0] = k
        vw_ref[0] = v


def _attn_prompt(z, batch, seq, gq, gk, inv, segq, segk, sinks):
    nb = seq // WINDOW
    blk = lambda width, col: pl.BlockSpec((WINDOW, width), lambda b, j: (b * nb + j, col))
    return pl.pallas_call(
        _attn_prompt_body,
        grid=(batch, nb),
        in_specs=[blk(QA, 0), blk(KVA, COL_K), blk(KVA, COL_V), _full(gq.shape), _full(gk.shape),
                  _full(inv.shape), _full(segq.shape), _full(segk.shape),
                  pl.BlockSpec(memory_space=pltpu.SMEM)],
        out_specs=[pl.BlockSpec((WINDOW, QA), lambda b, j: (b * nb + j, 0)),
                   pl.BlockSpec((1, WINDOW, KVA), lambda b, j: (b, 0, 0)),
                   pl.BlockSpec((1, WINDOW, KVA), lambda b, j: (b, 0, 0))],
        out_shape=[jax.ShapeDtypeStruct((batch * seq, QA), F32),
                   jax.ShapeDtypeStruct((batch, WINDOW, KVA), F32),
                   jax.ShapeDtypeStruct((batch, WINDOW, KVA), F32)],
        scratch_shapes=[pltpu.VMEM((WINDOW, KVA), F32), pltpu.VMEM((WINDOW, KVA), F32),
                        pltpu.VMEM((WINDOW, LANES), F32), pltpu.VMEM((WINDOW, LANES), F32)],
        compiler_params=_params(("arbitrary", "arbitrary")),
        name="swa_prompt",
    )(z, z, z, gq, gk, inv, segq, segk, sinks)


def _attn_decode_body(q_ref, k_ref, v_ref, kc_ref, vc_ref, gq_ref, gk_ref, inv_ref, segq_ref, segk_ref,
                      sink_ref, o_ref, kn_ref, vn_ref, *, dec_seq):
    rows_n = DEC_GROUP * dec_seq
    rows = lax.broadcasted_iota(jnp.int32, (rows_n, LANES), 0)
    pos = (PAST_LEN + rows % dec_seq).astype(F32)
    ang = pos * inv_ref[...]
    (ck, sk, fk), (cq, sq, fq) = _rope_tables(jnp.cos(ang), jnp.sin(ang), QA // LANES)
    q = _rope(_head_norm(q_ref[...], gq_ref[...], segq_ref[...]), cq, sq, fq)
    k = _rope(_head_norm(k_ref[...], gk_ref[...], segk_ref[...]), ck, sk, fk)
    v = v_ref[...]
    kn_ref[...] = k
    vn_ref[...] = v

    nq = KV_REP * rows_n
    nc = DEC_GROUP * WINDOW

    def qrow(shape):
        r = lax.broadcasted_iota(jnp.int32, shape, 0) % rows_n
        return r // dec_seq, r % dec_seq

    bq, tq = qrow((nq, nc))
    col = lax.broadcasted_iota(jnp.int32, (nq, nc), 1)
    m_cache = (col // WINDOW == bq) & (col % WINDOW > tq)
    bq, tq = qrow((nq, rows_n))
    col = lax.broadcasted_iota(jnp.int32, (nq, rows_n), 1)
    m_new = (col // dec_seq == bq) & (col % dec_seq <= tq)
    rep = lax.broadcasted_iota(jnp.int32, (nq, 1), 0) // rows_n
    for g in range(N_KV_A):
        sl = slice(g * HEAD_DIM, (g + 1) * HEAD_DIM)
        qg = _stack_heads(q, g).astype(BF16)
        s_c = jnp.where(m_cache, _dot_nt(qg, kc_ref[:, sl].astype(BF16)) * ATTN_SCALE, NEG_INF)
        s_n = jnp.where(m_new, _dot_nt(qg, k[:, sl].astype(BF16)) * ATTN_SCALE, NEG_INF)
        sink = jnp.zeros((nq, 1), F32)
        for r in range(KV_REP):
            sink = jnp.where(rep == r, sink_ref[g * KV_REP + r], sink)
        mx = jnp.maximum(jnp.maximum(jnp.max(s_c, axis=-1, keepdims=True),
                                     jnp.max(s_n, axis=-1, keepdims=True)), sink)
        e_c = jnp.exp(s_c - mx)
        e_n = jnp.exp(s_n - mx)
        den = jnp.sum(e_c, axis=-1, keepdims=True) + jnp.sum(e_n, axis=-1, keepdims=True) + jnp.exp(sink - mx)
        o = (_dot(e_c.astype(BF16), vc_ref[:, sl].astype(BF16))
             + _dot(e_n.astype(BF16), v[:, sl].astype(BF16))) / den
        for r in range(KV_REP):
            hq = g * KV_REP + r
            o_ref[:, hq * HEAD_DIM:(hq + 1) * HEAD_DIM] = o[r * rows_n:(r + 1) * rows_n]


def _attn_decode(z, row0, dec_batch, dec_seq, kc, vc, gq, gk, inv, segq, segk, sinks):
    rows_n = DEC_GROUP * dec_seq
    r0 = row0 // rows_n
    blk = lambda width, col: pl.BlockSpec((rows_n, width), lambda i: (r0 + i, col))
    cache = pl.BlockSpec((DEC_GROUP * WINDOW, KVA), lambda i: (i, 0))
    n = dec_batch * dec_seq
    return pl.pallas_call(
        functools.partial(_attn_decode_body, dec_seq=dec_seq),
        grid=(dec_batch // DEC_GROUP,),
        in_specs=[blk(QA, 0), blk(KVA, COL_K), blk(KVA, COL_V), cache, cache, _full(gq.shape),
                  _full(gk.shape), _full(inv.shape), _full(segq.shape), _full(segk.shape),
                  pl.BlockSpec(memory_space=pltpu.SMEM)],
        out_specs=[pl.BlockSpec((rows_n, QA), lambda i: (i, 0)), pl.BlockSpec((rows_n, KVA), lambda i: (i, 0)),
                   pl.BlockSpec((rows_n, KVA), lambda i: (i, 0))],
        out_shape=[jax.ShapeDtypeStruct((n, QA), F32), jax.ShapeDtypeStruct((n, KVA), F32),
                   jax.ShapeDtypeStruct((n, KVA), F32)],
        compiler_params=_params(("arbitrary",)),
        name="swa_decode",
    )(z, z, z, kc, vc, gq, gk, inv, segq, segk, sinks)


def _hgrn_lower_bound(lb_ref):
    l = lb_ref[...]
    e = jnp.exp(l - jnp.max(l, axis=0, keepdims=True))
    return e[0:1] / jnp.sum(e, axis=0, keepdims=True)


def _hgrn_gates(qb, fb, lb):
    logf = jnp.log(lb + (1.0 - lb) * jax.nn.sigmoid(fb))
    kb = (1.0 - lb) * jax.nn.sigmoid(-fb)
    return _silu(qb), kb, logf


def _hgrn_out(o, g, gate):
    return _rms(o, g) * _silu(gate)


def _hgrn_chunk(qh, kb, ih, logf, st, tril):
    c = qh.shape[0]
    hi = logf.astype(BF16)
    rest = logf - hi.astype(F32)
    mid = rest.astype(BF16)
    b = _dot(tril, hi) + _dot(tril, mid) + _dot(tril, (rest - mid.astype(F32)).astype(BF16))
    ones = jnp.ones((DK_B, LANES), BF16)
    st_b = st.astype(BF16)
    ih_b = ih.astype(BF16)
    o_parts = []
    row = lax.broadcasted_iota(jnp.int32, (c, 1), 0)
    trow = lax.broadcasted_iota(jnp.int32, (HGRN_SUB, 1), 0)
    for blk in range(c // HGRN_SUB):
        t0 = blk * HGRN_SUB
        bi = b[t0:t0 + HGRN_SUB]
        qi = qh[t0:t0 + HGRN_SUB]
        if blk == 0:
            oi = _dot_nt((qi * jnp.exp(bi)).astype(BF16), st_b)
        else:
            base = b[t0 - 1:t0]
            qd = qi * jnp.exp(bi - base)
            kd = jnp.where(row < t0, kb * jnp.exp(jnp.minimum(base - b, 0.0)), 0.0)
            a_off = _dot_nt(qd.astype(BF16), kd.astype(BF16))
            oi = _dot_nt((qd * jnp.exp(base)).astype(BF16), st_b) + _dot(a_off.astype(BF16), ih_b)
        prods = []
        for s in range(HGRN_SUB):
            e = jnp.exp(jnp.where(trow >= s, bi - bi[s:s + 1], 0.0))
            prods.append(qi * e * kb[t0 + s:t0 + s + 1])
        a_diag = _dot(jnp.concatenate(prods, axis=0).astype(BF16), ones)
        for s in range(HGRN_SUB):
            a = a_diag[s * HGRN_SUB:(s + 1) * HGRN_SUB]
            oi = oi + jnp.where(trow >= s, a, 0.0) * ih[t0 + s:t0 + s + 1]
        o_parts.append(oi)
    last = b[c - 1:c]
    st_new = st * jnp.exp(last) + _dot(ih.T.astype(BF16), (kb * jnp.exp(last - b)).astype(BF16))
    return jnp.concatenate(o_parts, axis=0), st_new


def _hgrn_prompt_body(qb_ref, fb_ref, ib_ref, gb_ref, lb_ref, go_ref, tril_ref, o_ref, s_ref, st_ref):
    tb = pl.program_id(2)

    @pl.when(tb == 0)
    def _():
        st_ref[...] = jnp.zeros_like(st_ref)

    tril = tril_ref[...]
    heads = [slice(h * DK_B, (h + 1) * DK_B) for h in range(HGRN_HEADS_PER_STEP)]
    lbs = [_hgrn_lower_bound(lb_ref.at[:, hs]) for hs in heads]
    sts = [st_ref[h] for h in range(HGRN_HEADS_PER_STEP)]
    for c in range(HGRN_ROWS // HGRN_CHUNK):
        rs = slice(c * HGRN_CHUNK, (c + 1) * HGRN_CHUNK)
        for h, hs in enumerate(heads):
            qh, kb, logf = _hgrn_gates(qb_ref[rs, hs], fb_ref[rs, hs], lbs[h])
            o, sts[h] = _hgrn_chunk(qh, kb, ib_ref[rs, hs], logf, sts[h], tril)
            o_ref[rs, hs] = _hgrn_out(o, go_ref[...], gb_ref[rs, hs])
    for h in range(HGRN_HEADS_PER_STEP):
        st_ref[h] = sts[h]

    @pl.when(tb == pl.num_programs(2) - 1)
    def _():
        for h in range(HGRN_HEADS_PER_STEP):
            s_ref[0, h] = sts[h].T


def _hgrn_prompt(z, batch, seq, lb, go, tril):
    nt = seq // HGRN_ROWS
    hp = HGRN_HEADS_PER_STEP
    assert N_HEADS_B % hp == 0 and all(c % hp == 0 for c in (COL_QB, COL_FB, COL_IB, COL_GB))
    blk = lambda col: pl.BlockSpec((HGRN_ROWS, hp * DK_B), lambda b, h, t: (b * nt + t, col // hp + h))
    return pl.pallas_call(
        _hgrn_prompt_body,
        grid=(batch, N_HEADS_B // hp, nt),
        in_specs=[blk(COL_QB), blk(COL_FB), blk(COL_IB), blk(COL_GB),
                  pl.BlockSpec((lb.shape[0], hp * DK_B), lambda b, h, t: (0, h)), _full(go.shape), _full(tril.shape)],
        out_specs=[pl.BlockSpec((HGRN_ROWS, hp * DV_B), lambda b, h, t: (b * nt + t, h)),
                   pl.BlockSpec((1, hp, DK_B, DV_B), lambda b, h, t: (b, h, 0, 0))],
        out_shape=[jax.ShapeDtypeStruct((batch * seq, VB), F32),
                   jax.ShapeDtypeStruct((batch, N_HEADS_B, DK_B, DV_B), F32)],
        scratch_shapes=[pltpu.VMEM((hp, DV_B, DK_B), F32)],
        compiler_params=_params(("arbitrary", "arbitrary", "arbitrary")),
        name="hgrn_prompt",
    )(z, z, z, z, lb, go, tril)


def _pad_rows(x, rows):
    return jnp.concatenate([x, jnp.zeros((rows - x.shape[0], x.shape[1]), x.dtype)], axis=0)


def _hgrn_decode_body(z_ref, s0_ref, lb_ref, go_ref, tril_ref, o_ref, s_ref, *, dec_seq):
    rows_n = z_ref.shape[0]
    groups = rows_n // dec_seq
    row = lax.broadcasted_iota(jnp.int32, (rows_n, 1), 0)
    for h in range(N_HEADS_B):
        col = lambda c0: slice((c0 + h) * LANES, (c0 + h + 1) * LANES)
        hs = slice(h * DK_B, (h + 1) * DK_B)
        lb = _hgrn_lower_bound(lb_ref.at[:, hs])
        qh, kb, logf = _hgrn_gates(z_ref[:, col(COL_QB)], z_ref[:, col(COL_FB)], lb)
        ih = z_ref[:, col(COL_IB)]
        b = _dot(tril_ref[...], _pad_rows(logf, LANES), HI)[:rows_n]
        b_t = _pad_rows(b, LANES).T
        ih_pad = _pad_rows(ih, LANES)
        o = jnp.zeros((rows_n, DV_B), F32)
        for e in range(groups):
            mine = (row // dec_seq) == e
            s0 = s0_ref[e, h]
            o = o + _dot(jnp.where(mine, qh * jnp.exp(b), 0.0), s0, HI)
            for s in range(dec_seq):
                r = e * dec_seq + s
                live = mine & (row >= r)
                ex = jnp.exp(jnp.where(live, b - b[r:r + 1], 0.0))
                a = jnp.sum(qh * ex * kb[r:r + 1], axis=-1, keepdims=True)
                o = o + jnp.where(live, a, 0.0) * ih[r:r + 1]
            r_last = (e + 1) * dec_seq - 1
            last = b[r_last:r_last + 1]
            kd = jnp.where(mine, kb * jnp.exp(jnp.minimum(last - b, 0.0)), 0.0)
            s_ref[e, h] = s0 * jnp.exp(b_t[:, r_last:r_last + 1]) + _dot(_pad_rows(kd, LANES).T, ih_pad, HI)
        o_ref[:, hs] = _hgrn_out(o, go_ref[...], z_ref[:, col(COL_GB)])


def _hgrn_decode(z, row0, dec_batch, dec_seq, s0, lb, go, tril):
    groups = 8 // dec_seq
    rows_n = groups * dec_seq
    r0 = row0 // rows_n
    st = pl.BlockSpec((groups, N_HEADS_B, DK_B, DV_B), lambda i: (i, 0, 0, 0))
    return pl.pallas_call(
        functools.partial(_hgrn_decode_body, dec_seq=dec_seq),
        grid=(dec_batch // groups,),
        in_specs=[pl.BlockSpec((rows_n, z.shape[1]), lambda i: (r0 + i, 0)), st,
                  _full(lb.shape), _full(go.shape), _full(tril.shape)],
        out_specs=[pl.BlockSpec((rows_n, VB), lambda i: (i, 0)), st],
        out_shape=[jax.ShapeDtypeStruct((dec_batch * dec_seq, VB), F32),
                   jax.ShapeDtypeStruct(s0.shape, F32)],
        compiler_params=_params(("arbitrary",)),
        name="hgrn_decode",
    )(z, s0, lb, go, tril)


def _conv_prompt_body(cur_ref, prev_ref, w_ref, b_ref, c_ref, ext_ref, sh_ref):
    t = pl.program_id(1)
    ext_ref[:CONV_HALO, :] = jnp.where(t > 0, prev_ref[...], 0.0)
    ext_ref[CONV_HALO:, :] = cur_ref[...]
    lead = CONV_HALO - (CONV_W - 1)
    chunk = CONV_CHUNK

    def cols(ci, carry):
        cs = pl.ds(pl.multiple_of(ci * LANES, LANES), LANES)
        for r in range(CONV_ROWS // chunk):
            acc = jnp.zeros((chunk, LANES), F32) + b_ref[:, cs]
            for res in range(SUBLANES):
                taps = range(res, CONV_W, SUBLANES)
                span = chunk + taps[-1] - res
                sh_ref[:span, :] = ext_ref[pl.ds(r * chunk + lead + res, span), cs]
                for w in taps:
                    acc = acc + sh_ref[w - res:w - res + chunk, :] * w_ref[pl.ds(w, 1), cs]
            c_ref[pl.ds(r * chunk, chunk), cs] = acc
        return carry

    lax.fori_loop(0, D_MODEL // LANES, cols, 0)


def _conv_prompt(u, batch, seq, w, b):
    nt = seq // CONV_ROWS
    per = CONV_ROWS // CONV_HALO
    return pl.pallas_call(
        _conv_prompt_body,
        grid=(batch, nt),
        in_specs=[pl.BlockSpec((CONV_ROWS, D_MODEL), lambda bi, t: (bi * nt + t, 0)),
                  pl.BlockSpec((CONV_HALO, D_MODEL), lambda bi, t: (jnp.maximum((bi * nt + t) * per - 1, 0), 0)),
                  _full(w.shape), _full(b.shape)],
        out_specs=pl.BlockSpec((CONV_ROWS, D_MODEL), lambda bi, t: (bi * nt + t, 0)),
        out_shape=jax.ShapeDtypeStruct((batch * seq, D_MODEL), F32),
        scratch_shapes=[pltpu.VMEM((CONV_HALO + CONV_ROWS, D_MODEL), F32),
                        pltpu.VMEM((CONV_CHUNK + CONV_HALO, LANES), F32)],
        compiler_params=_params(("arbitrary", "arbitrary")),
        name="conv_prompt",
    )(u, u, w, b)


def _conv_decode_body(u_ref, st_ref, w_ref, wshift_ref, b_ref, c_ref, *, dec_seq):
    for e in range(DEC_GROUP):
        past = st_ref[e]
        for t in range(dec_seq):
            acc = jnp.sum(past * wshift_ref[t], axis=0, keepdims=True) + b_ref[...]
            for t2 in range(t + 1):
                wi = CONV_W - 1 - t + t2
                acc = acc + u_ref[e * dec_seq + t2:e * dec_seq + t2 + 1, :] * w_ref[wi:wi + 1, :]
            c_ref[e * dec_seq + t:e * dec_seq + t + 1, :] = acc


def _conv_decode(u, row0, dec_batch, dec_seq, state, w, wshift, b):
    rows_n = DEC_GROUP * dec_seq
    r0 = row0 // rows_n
    return pl.pallas_call(
        functools.partial(_conv_decode_body, dec_seq=dec_seq),
        grid=(dec_batch // DEC_GROUP,),
        in_specs=[pl.BlockSpec((rows_n, D_MODEL), lambda i: (r0 + i, 0)),
                  pl.BlockSpec((DEC_GROUP, CONV_W - 1, D_MODEL), lambda i: (i, 0, 0)),
                  _full(w.shape), _full(wshift.shape), _full(b.shape)],
        out_specs=pl.BlockSpec((rows_n, D_MODEL), lambda i: (i, 0)),
        out_shape=jax.ShapeDtypeStruct((dec_batch * dec_seq, D_MODEL), F32),
        compiler_params=_params(("arbitrary",)),
        name="conv_decode",
    )(u, state, w, wshift, b)


def _extract_top(s, idx, count, none_rank):
    work = s
    rank = jnp.full(s.shape, none_rank, F32)
    vals = []
    for r in range(count):
        m = jnp.max(work, axis=0, keepdims=True)
        first = jnp.min(jnp.where(work == m, idx, np.float32(1e9)), axis=0, keepdims=True)
        hit = idx == first
        rank = jnp.where(hit, np.float32(r), rank)
        work = jnp.where(hit, -jnp.inf, work)
        vals.append(m)
    return rank, vals


_PEER_CAND = [(a, b) for a in range(PEER_TOPK) for b in range(PEER_TOPK) if (a + 1) * (b + 1) <= PEER_TOPK]
_PEER_CAND_ROWS = -(-len(_PEER_CAND) // 8) * 8


def _extract_by_value(s, count, want_rank=True):
    work = s
    rank = jnp.full(s.shape, np.float32(count), F32) if want_rank else None
    vals = []
    for r in range(count):
        m = jnp.max(work, axis=0, keepdims=True)
        hit = work == m
        if want_rank:
            rank = jnp.where(hit, np.float32(r), rank)
        work = jnp.where(hit, -jnp.inf, work)
        vals.append(m)
    done = rank < count if want_rank else work == -jnp.inf
    return rank, vals, jnp.sum(jnp.where(done, 1.0, 0.0), axis=0, keepdims=True)


def _peer_select_body(q_ref, keys_ref, rank_ref, qe_ref, lq_ref, pe_ref):
    ts = q_ref.shape[0]
    kidx = lax.broadcasted_iota(jnp.int32, (N_KEYS, ts), 0).astype(F32)
    cidx = lax.broadcasted_iota(jnp.int32, (_PEER_CAND_ROWS, ts), 0).astype(F32)
    tidx = lax.broadcasted_iota(jnp.int32, (PEER_TOPK, ts), 0).astype(F32)
    bidx = lax.broadcasted_iota(jnp.int32, (SUBLANES, ts), 0).astype(F32)
    pad =jnp.full((_PEER_CAND_ROWS, ts), -jnp.inf, F32)

    def tables(h, s0, s1, exact):
        if exact:
            rank0, v0 = _extract_top(s0, kidx, PEER_TOPK, PEER_TOPK)
            rank1, v1 = _extract_top(s1, kidx, PEER_TOPK, PEER_TOPK)
        else:
            _, v0, n0 = _extract_by_value(s0, PEER_TOPK, want_rank=False)
            rank1, v1, n1 = _extract_by_value(s1, PEER_TOPK)
        if exact:
            pairs = list(_PEER_CAND)
            cand = pad
            for ci, (a, b) in enumerate(pairs):
                cand = jnp.where(cidx == ci, v0[a] + v1[b], cand)
            crank, _ = _extract_top(cand, cidx, PEER_TOPK, PEER_TOPK)
            off = None
        else:
            col0 = jnp.zeros((PEER_TOPK, ts), F32)
            col1 = jnp.zeros((PEER_TOPK, ts), F32)
            low1 = jnp.zeros((SUBLANES, ts), F32)
            for r in range(PEER_TOPK):
                col0 = jnp.where(tidx == r, v0[r], col0)
                col1 = jnp.where(tidx == r, v1[r], col1)
                if r < SUBLANES:
                    low1 = jnp.where(bidx == r, v1[r], low1)
            ninf = np.float32(-np.inf)
            groups = [v0[0] + col1, jnp.where(tidx == 0, ninf, col0 + v1[0])]
            pairs = [(0, b) for b in range(PEER_TOPK)] + [(a, 0) if a else None for a in range(PEER_TOPK)]
            rest = [p for p in _PEER_CAND if p[0] and p[1]]
            for a in sorted({p[0] for p in rest}):
                mine = [p for p in rest if p[0] == a]
                if len(mine) > 2:
                    groups.append(jnp.where((bidx >= 1) & (bidx <= len(mine)), v0[a] + low1, ninf))
                    pairs += [(a, b) if 1 <= b <= len(mine) else None for b in range(SUBLANES)]
            loose = [p for p in rest if p not in pairs]
            assert len(loose) <= SUBLANES
            tail = jnp.full((SUBLANES, ts), ninf, F32)
            for r, (a, b) in enumerate(loose):
                tail = jnp.where(bidx == r, v0[a] + v1[b], tail)
            groups.append(tail)
            pairs += loose + [None] * (SUBLANES - len(loose))
            assert sorted(p for p in pairs if p) == sorted(_PEER_CAND)
            cand = jnp.concatenate(groups, axis=0)
            crank, _, nc = _extract_by_value(cand, PEER_TOPK)
            off = jnp.max(jnp.abs(n0 - PEER_TOPK) + jnp.abs(n1 - PEER_TOPK) + jnp.abs(nc - PEER_TOPK))
        picked = jnp.where(crank < PEER_TOPK, 1.0, 0.0)
        e = picked * jnp.exp(cand - (v0[0] + v1[0]))
        z = jnp.sum(e, axis=0, keepdims=True)
        lq = jnp.zeros((N_KEYS, ts), F32)
        for a in range(PEER_TOPK):
            cnt = jnp.zeros((1, ts), F32)
            for ci, p in enumerate(pairs):
                if p is not None and p[0] == a:
                    cnt = cnt + picked[ci:ci + 1]
            lq = jnp.where((rank0 == a) if exact else (s0 == v0[a]), cnt, lq)
        rank_b = rank1.astype(BF16)
        qe_b = jnp.exp(s1 - v1[0]).astype(BF16)
        for k in range(N_KEYS // BF16_ROWS):
            rank_ref[h, k] = pltpu.bitcast(rank_b[k * BF16_ROWS:(k + 1) * BF16_ROWS], jnp.uint32)
            qe_ref[h, k] = pltpu.bitcast(qe_b[k * BF16_ROWS:(k + 1) * BF16_ROWS], jnp.uint32)
        lq_ref[h] = lq
        pe_ref[h] = jnp.exp(s0 - v0[0]) * (0.5 / z)
        return off

    def head(h, carry):
        c0 = pl.multiple_of(h * 2 * PEER_DKH, 2 * PEER_DKH)
        q0 = q_ref[:, pl.ds(c0, PEER_DKH)].astype(BF16)
        q1 = q_ref[:, pl.ds(c0 + PEER_DKH, PEER_DKH)].astype(BF16)
        s0 = _dot_nt(keys_ref[h, 0], q0)
        s1 = _dot_nt(keys_ref[h, 1], q1)
        off = tables(h, s0, s1, exact=False)

        @pl.when(off > 0.5)
        def _():
            tables(h, s0, s1, exact=True)

        return carry

    lax.fori_loop(0, PEER_HEADS, head, 0)


def _peer_select(q, keys):
    n = q.shape[0]
    out = jax.ShapeDtypeStruct((PEER_HEADS, N_KEYS, n), F32)
    ospec = pl.BlockSpec((PEER_HEADS, N_KEYS, SEL_TILE), lambda i: (0, 0, i))
    packed = (PEER_HEADS, N_KEYS // BF16_ROWS, BF16_ROWS // 2)
    out_b = jax.ShapeDtypeStruct(packed + (n,), jnp.uint32)
    ospec_b = pl.BlockSpec(packed + (SEL_TILE,), lambda i: (0, 0, 0, i))
    return pl.pallas_call(
        _peer_select_body,
        grid=(n // SEL_TILE,),
        in_specs=[pl.BlockSpec((SEL_TILE, q.shape[1]), lambda i: (i, 0)), _full(keys.shape)],
        out_specs=[ospec_b, ospec_b, ospec, ospec],
        out_shape=[out_b, out_b, out, out],
        compiler_params=_params(("arbitrary",)),
        name="peer_select",
    )(q, keys)


def _peer_dense_body(ht_ref, u_ref, vt_ref, rank_ref, qe_ref, lq_ref, pe_ref, x_ref, *rest, prompt_tiles):
    *o_ref, acc_ref, a_ref, w_ref = rest
    step = pl.program_id(1)

    @pl.when(step == 0)
    def _():
        acc_ref[...] = jnp.zeros_like(acc_ref)

    @pl.when(step >= 0)
    def _():
        quarter = PEER_ROWS * N_KEYS // 4
        for qi in range(4):
            rows = slice(qi * quarter, (qi + 1) * quarter)
            a_ref[rows, :] = _dot(u_ref[rows, :], ht_ref[...])

    @pl.when(step >= 0)
    def _():
        sub = (N_KEYS // BF16_ROWS, BF16_ROWS, LANES)
        for il in range(PEER_ROWS):
            for ci in range(TOKEN_TILE // LANES):
                cs = slice(ci * LANES, (ci + 1) * LANES)
                gate = jnp.zeros(sub, BF16)
                for h in range(PEER_HEADS):
                    lq = jnp.broadcast_to(lq_ref[h, il:il + 1, cs], sub[1:]).astype(BF16)
                    pe = jnp.broadcast_to(pe_ref[h, il:il + 1, cs], sub[1:]).astype(BF16)
                    take = pltpu.bitcast(rank_ref[h, :, :, cs], BF16) < lq[None]
                    qe = pltpu.bitcast(qe_ref[h, :, :, cs], BF16)
                    gate = gate + jnp.where(take, qe, jnp.zeros(sub, BF16)) * pe[None]
                for k in range(sub[0]):
                    rs = slice(il * N_KEYS + k * BF16_ROWS, il * N_KEYS + (k + 1) * BF16_ROWS)
                    w_ref[rs, cs] = gate[k] * _gelu_doubled(a_ref[rs, cs].astype(BF16))
        acc_ref[...] += _dot(vt_ref[0], w_ref[...])

    last = step == pl.num_programs(1) - 1
    if prompt_tiles is None:
        @pl.when(last)
        def _():
            o_ref[0][...] = x_ref[...] + acc_ref[...].T
    else:
        tile = pl.program_id(0)
        o_prompt, o_sample = o_ref

        @pl.when(last & (tile < prompt_tiles))
        def _():
            o_prompt[...] = x_ref[...] + acc_ref[...].T

        @pl.when(last & (tile >= prompt_tiles))
        def _():
            o_sample[...] = x_ref[...] + acc_ref[...].T


def _peer_dense(ht, u, vt, rank, qe, lq, pe, x, n_prompt=None):
    n = x.shape[0]
    ex = PEER_ROWS * N_KEYS
    steps = N_KEYS // PEER_ROWS
    tok = pl.BlockSpec((PEER_HEADS, N_KEYS // BF16_ROWS, BF16_ROWS // 2, TOKEN_TILE), lambda t, e: (0, 0, 0, t))
    rowsel = pl.BlockSpec((PEER_HEADS, PEER_ROWS, TOKEN_TILE), lambda t, e: (0, e, t))
    if n_prompt is None:
        tp = None
        out_specs = pl.BlockSpec((TOKEN_TILE, D_MODEL), lambda t, e: (t, 0))
        out_shape = jax.ShapeDtypeStruct((n, D_MODEL), F32)
    else:
        tp = n_prompt // TOKEN_TILE
        out_specs = [pl.BlockSpec((TOKEN_TILE, D_MODEL), lambda t, e: (jnp.minimum(t, tp - 1), 0)),
                     pl.BlockSpec((TOKEN_TILE, D_MODEL), lambda t, e: (jnp.maximum(t - tp, 0), 0))]
        out_shape = [jax.ShapeDtypeStruct((n_prompt, D_MODEL), F32),
                     jax.ShapeDtypeStruct((n - n_prompt, D_MODEL), F32)]
    return pl.pallas_call(
        functools.partial(_peer_dense_body, prompt_tiles=tp),
        grid=(n // TOKEN_TILE, steps),
        in_specs=[pl.BlockSpec((D_MODEL, TOKEN_TILE), lambda t, e: (0, t)),
                  pl.BlockSpec((ex, D_MODEL), lambda t, e: (e, 0)),
                  pl.BlockSpec((1, D_MODEL, ex), lambda t, e: (e, 0, 0)),
                  tok, tok, rowsel, rowsel,
                  pl.BlockSpec((TOKEN_TILE, D_MODEL), lambda t, e: (t, 0))],
        out_specs=out_specs,
        out_shape=out_shape,
        scratch_shapes=[pltpu.VMEM((D_MODEL, TOKEN_TILE), F32), pltpu.VMEM((ex, TOKEN_TILE), F32),
                        pltpu.VMEM((ex, TOKEN_TILE), BF16)],
        compiler_params=_params(("arbitrary", "arbitrary")),
        name="peer_dense",
    )(ht, u, vt, rank, qe, lq, pe, x)


def _peer(x1, ht, q, keys, u, vt, n_prompt=None):
    rank, qe, lq, pe = _peer_select(q, keys)
    return _peer_dense(ht, u, vt, rank, qe, lq, pe, x1, n_prompt)


def _value_blocks(v):
    ex = PEER_ROWS * N_KEYS
    return v.astype(BF16).reshape(v.shape[0] // ex, ex, v.shape[1]).transpose(0, 2, 1)


def _block_diag_mean(width):
    idx = np.arange(width) // HEAD_DIM
    return jnp.asarray((idx[:, None] == idx[None, :]).astype(np.float32) / HEAD_DIM, BF16)


def kernel(x_prompt, x_sample, cache_swa_k, cache_swa_v, state_hgrn, state_conv, norm_mix_g, norm_ffn_g, w_in0, attn_q_norm_g, attn_k_norm_g, attn_sinks, hgrn_lb, hgrn_o_norm_g, w_out0, conv_w_pw1, conv_b_pw1, conv_w_dw, conv_b_dw, conv_ln_g, conv_ln_b, conv_w_pw2, peer_w_q, peer_keys, peer_u, peer_v):
    batch, seq, _ = x_prompt.shape
    dec_batch, dec_seq, _ = x_sample.shape
    n_p = batch * seq
    n_s = dec_batch * dec_seq
    assert seq % CONV_ROWS == 0 and seq % HGRN_ROWS == 0 and n_p % TOKEN_TILE == 0 and n_s % TOKEN_TILE == 0
    assert dec_batch % DEC_GROUP == 0 and 8 % dec_seq == 0 and dec_seq <= CONV_W - 1

    x = (x_prompt.reshape(n_p, D_MODEL), x_sample.reshape(n_s, D_MODEL))
    row = lambda v: v.reshape(1, -1).astype(F32)

    z = _in0(x, row(norm_mix_g[0]), w_in0.astype(BF16))
    d = np.arange(LANES) % HEAD_DIM
    inv = ROPE_THETA ** (-jnp.arange(0, ROT_DIM, 2, dtype=F32) / ROT_DIM)
    inv_lane = jnp.where(jnp.asarray(d < ROT_DIM), inv[d % (ROT_DIM // 2)], 0.0).reshape(1, LANES)
    gq = row(jnp.tile(attn_q_norm_g, N_HEADS_A))
    gk = row(jnp.tile(attn_k_norm_g, N_KV_A))
    segq, segk = _block_diag_mean(QA), _block_diag_mean(KVA)
    sinks = attn_sinks.astype(F32)
    attn_p, k_win_p, v_win_p = _attn_prompt(z, batch, seq, gq, gk, inv_lane, segq, segk, sinks)
    kc = cache_swa_k.reshape(dec_batch * WINDOW, KVA)
    vc = cache_swa_v.reshape(dec_batch * WINDOW, KVA)
    attn_s, k_new, v_new = _attn_decode(z, n_p, dec_batch, dec_seq, kc, vc, gq, gk, inv_lane, segq, segk, sinks)

    go = row(hgrn_o_norm_g)
    tril = jnp.asarray(np.tril(np.ones((HGRN_CHUNK, HGRN_CHUNK), np.float32)), BF16)
    hg_p, s_p = _hgrn_prompt(z, batch, seq, hgrn_lb, go, tril)
    r = np.arange(LANES)
    tril_dec = jnp.asarray(((r[:, None] // dec_seq == r[None, :] // dec_seq) & (r[None, :] <= r[:, None]))
                           .astype(np.float32))
    hg_s, s_s = _hgrn_decode(z, n_p, dec_batch, dec_seq, state_hgrn, hgrn_lb, go, tril_dec)

    x1, ht, q = _mix_out(_out0_body, [(attn_p, attn_s), (hg_p, hg_s)], x, [w_out0.astype(BF16)], row(norm_ffn_g[0]),
                         peer_w_q[0].astype(BF16), "mix0_out")
    x2 = _peer(x1, ht, q, peer_keys[0].astype(BF16), peer_u[0].astype(BF16), _value_blocks(peer_v[0]))

    u = _glu(x2, row(norm_mix_g[1]), conv_w_pw1.astype(BF16), row(conv_b_pw1))
    wdw = jnp.concatenate([conv_w_dw, jnp.zeros((1, D_MODEL), F32)], axis=0)
    bdw = row(conv_b_dw)
    c_p = _conv_prompt(u, batch, seq, wdw, bdw)
    wshift = jnp.stack([jnp.concatenate([jnp.zeros((t, D_MODEL), F32), conv_w_dw[:CONV_W - 1 - t]], axis=0)
                        for t in range(dec_seq)])
    c_s = _conv_decode(u, n_p, dec_batch, dec_seq, state_conv, wdw, wshift, bdw)
    x3, ht, q = _mix_out(_post1_body, [(c_p, c_s)], x2, [row(conv_ln_g), row(conv_ln_b), conv_w_pw2.astype(BF16)],
                         row(norm_ffn_g[1]), peer_w_q[1].astype(BF16), "mix1_out")
    y_p, y_s = _peer(x3, ht, q, peer_keys[1].astype(BF16), peer_u[1].astype(BF16),
                     _value_blocks(peer_v[1]), n_p)

    kv = lambda t: t.reshape(t.shape[0], WINDOW, N_KV_A, HEAD_DIM)
    k_win_s = jnp.concatenate([cache_swa_k[:, dec_seq:], k_new.reshape(dec_batch, dec_seq, N_KV_A, HEAD_DIM)], axis=1)
    v_win_s = jnp.concatenate([cache_swa_v[:, dec_seq:], v_new.reshape(dec_batch, dec_seq, N_KV_A, HEAD_DIM)], axis=1)
    u_p = u[:n_p].reshape(batch, seq, D_MODEL)
    u_s = u[n_p:].reshape(dec_batch, dec_seq, D_MODEL)
    conv_buf_p = u_p[:, seq - (CONV_W - 1):]
    conv_buf_s = jnp.concatenate([state_conv[:, dec_seq:], u_s], axis=1)
    return (y_p.reshape(batch, seq, D_MODEL), y_s.reshape(dec_batch, dec_seq, D_MODEL),
            kv(k_win_p), kv(v_win_p), s_p, conv_buf_p, k_win_s, v_win_s, s_s, conv_buf_s)
```

```python
import functools

import numpy as np
import jax
import jax.numpy as jnp
from jax import lax
from jax.experimental import pallas as pl
from jax.experimental.pallas import tpu as pltpu

F32 = jnp.float32
BF16 = jnp.bfloat16
HI = lax.Precision.HIGHEST

D_MODEL = 1024
PAST_LEN = 8192
HEAD_DIM = 64
N_HEADS_A = 8
N_KV_A = 2
KV_REP = N_HEADS_A // N_KV_A
WINDOW = 128
ROT_DIM = HEAD_DIM // 4
ROPE_THETA = 500000.0
ATTN_SCALE = HEAD_DIM ** -0.5
NEG_INF = -1e30
N_HEADS_B = 4
DK_B = 128
DV_B = 128
CONV_W = 31
N_KEYS = 128
PEER_HEADS = 8
PEER_TOPK = 16
PEER_DKH = 128
NORM_EPS = 1e-6

QA = N_HEADS_A * HEAD_DIM
KVA = N_KV_A * HEAD_DIM
QB = N_HEADS_B * DK_B
VB = N_HEADS_B * DV_B
IN0_WIDTH = QA + 2 * KVA + 2 * QB + 2 * VB
COL_K = QA // 128
COL_V = (QA + KVA) // 128
COL_QB = (QA + 2 * KVA) // 128
COL_FB = COL_QB + QB // 128
COL_IB = COL_FB + QB // 128
COL_GB = COL_IB + VB // 128

LANES = 128
SUBLANES = 8
TOKEN_TILE = 512
SEL_TILE = 256
HGRN_CHUNK = 64
HGRN_SUB = 16
HGRN_ROWS = 512
HGRN_HEADS_PER_STEP = 2
CONV_ROWS = 512
CONV_HALO = 32
CONV_CHUNK = 128
PEER_ROWS = 16
BF16_ROWS = 16
DEC_GROUP = 8
VMEM_LIMIT = 48 * 1024 * 1024


def _dot(a, b, prec=None):
    return jnp.dot(a, b, preferred_element_type=F32, precision=prec)


def _dot_nt(a, b, prec=None):
    return lax.dot_general(a, b, (((1,), (1,)), ((), ())), preferred_element_type=F32, precision=prec)


def _rms(x, g):
    return x * lax.rsqrt(jnp.mean(x * x, axis=-1, keepdims=True) + NORM_EPS) * g


def _silu(x):
    return x * jax.nn.sigmoid(x)


def _gelu(x):
    return 0.5 * x * (1.0 + lax.erf(x * 0.7071067811865476))


def _params(sem, flags=None):
    return pltpu.CompilerParams(dimension_semantics=sem, vmem_limit_bytes=VMEM_LIMIT, flags=flags)


def _full(shape):
    n = len(shape)
    return pl.BlockSpec(shape, lambda *_: (0,) * n)


def _row_inputs(arrs):
    specs, ops, layout = [], [], []
    for a in arrs:
        if isinstance(a, tuple):
            tp = a[0].shape[0] // TOKEN_TILE
            width = a[0].shape[1]
            specs += [pl.BlockSpec((TOKEN_TILE, width), lambda i, tp=tp: (jnp.minimum(i, tp - 1), 0)),
                      pl.BlockSpec((TOKEN_TILE, width), lambda i, tp=tp: (jnp.maximum(i - tp, 0), 0))]
            ops += list(a)
            layout.append(tp)
        else:
            specs.append(pl.BlockSpec((TOKEN_TILE, a.shape[1]), lambda i: (i, 0)))
            ops.append(a)
            layout.append(None)
    return specs, ops, tuple(layout)


def _row_values(layout, refs):
    i = pl.program_id(0)
    vals, k = [], 0
    for tp in layout:
        if tp is None:
            vals.append(refs[k][...])
            k += 1
        else:
            vals.append(jnp.where(i < tp, refs[k][...], refs[k + 1][...]))
            k += 2
    return vals, refs[k:]


def _rows_of(a):
    return a[0].shape[0] + a[1].shape[0] if isinstance(a, tuple) else a.shape[0]


def _in0_body(*refs, layout):
    (x,), (g_ref, w_ref, z_ref) = _row_values(layout, refs)
    h = _rms(x, g_ref[...]).astype(BF16)
    z_ref[...] = _dot(h, w_ref[...])


def _in0(x, g, w):
    n = _rows_of(x)
    width = w.shape[1]
    specs, ops, layout = _row_inputs([x])
    return pl.pallas_call(
        functools.partial(_in0_body, layout=layout),
        grid=(n // TOKEN_TILE,),
        in_specs=specs + [_full((1, D_MODEL)), _full((D_MODEL, width))],
        out_specs=pl.BlockSpec((TOKEN_TILE, width), lambda i: (i, 0)),
        out_shape=jax.ShapeDtypeStruct((n, width), F32),
        compiler_params=_params(("arbitrary",)),
        name="in0_proj",
    )(*ops, g, w)


def _glu_body(x_ref, g_ref, w_ref, b_ref, u_ref):
    h = _rms(x_ref[...], g_ref[...]).astype(BF16)
    a = _dot(h, w_ref[...]) + b_ref[...]
    u_ref[...] = a[:, :D_MODEL] * jax.nn.sigmoid(a[:, D_MODEL:])


def _glu(x, g, w, b):
    n = x.shape[0]
    return pl.pallas_call(
        _glu_body,
        grid=(n // TOKEN_TILE,),
        in_specs=[pl.BlockSpec((TOKEN_TILE, D_MODEL), lambda i: (i, 0)), _full((1, D_MODEL)),
                  _full((D_MODEL, 2 * D_MODEL)), _full((1, 2 * D_MODEL))],
        out_specs=pl.BlockSpec((TOKEN_TILE, D_MODEL), lambda i: (i, 0)),
        out_shape=jax.ShapeDtypeStruct((n, D_MODEL), F32),
        compiler_params=_params(("arbitrary",)),
        name="conv_glu",
    )(x, g, w, b)


def _ffn_query(x1, gf_ref, wq_ref, x1_ref, ht_ref, q_ref):
    x1_ref[...] = x1
    h2 = _rms(x1, gf_ref[...])
    ht_ref[...] = h2.T.astype(BF16)
    q_ref[...] = _dot(h2.astype(BF16), wq_ref[...])


def _out0_body(*refs, layout):
    (attn, hg, x), (w_ref, gf_ref, wq_ref, x1_ref, ht_ref, q_ref) = _row_values(layout, refs)
    m = _dot(attn.astype(BF16), w_ref[:QA, :]) + _dot(hg.astype(BF16), w_ref[QA:, :])
    _ffn_query(x + m, gf_ref, wq_ref, x1_ref, ht_ref, q_ref)


def _post1_body(*refs, layout):
    (c, x), (lg_ref, lb_ref, w_ref, gf_ref, wq_ref, x1_ref, ht_ref, q_ref) = _row_values(layout, refs)
    mu = jnp.mean(c, axis=-1, keepdims=True)
    d = c - mu
    var = jnp.mean(d * d, axis=-1, keepdims=True)
    ln = d * lax.rsqrt(var + NORM_EPS) * lg_ref[...] + lb_ref[...]
    y = _dot(_silu(ln).astype(BF16), w_ref[...])
    _ffn_query(x + y, gf_ref, wq_ref, x1_ref, ht_ref, q_ref)


def _mix_out(body, acts, x, consts, gf, wq, name):
    n = _rows_of(x)
    qw = wq.shape[1]
    row = lambda w: pl.BlockSpec((TOKEN_TILE, w), lambda i: (i, 0))
    specs, ops, layout = _row_inputs(list(acts) + [x])
    return pl.pallas_call(
        functools.partial(body, layout=layout),
        grid=(n // TOKEN_TILE,),
        in_specs=specs + [_full(c.shape) for c in consts] + [_full(gf.shape), _full(wq.shape)],
        out_specs=[row(D_MODEL), pl.BlockSpec((D_MODEL, TOKEN_TILE), lambda i: (0, i)), row(qw)],
        out_shape=[jax.ShapeDtypeStruct((n, D_MODEL), F32), jax.ShapeDtypeStruct((D_MODEL, n), BF16),
                   jax.ShapeDtypeStruct((n, qw), F32)],
        compiler_params=_params(("arbitrary",)),
        name=name,
    )(*ops, *consts, gf, wq)


def _head_norm(x, g, seg):
    xx = x * x
    hi = xx.astype(BF16)
    rest = xx - hi.astype(F32)
    mid = rest.astype(BF16)
    ms = _dot(hi, seg) + _dot(mid, seg) + _dot((rest - mid.astype(F32)).astype(BF16), seg)
    return x * lax.rsqrt(ms + NORM_EPS) * g


def _rope(x, cos, sin, first_half):
    half = ROT_DIM // 2
    width = x.shape[1]
    up = pltpu.roll(x, width - half, axis=1)
    dn = pltpu.roll(x, half, axis=1)
    return x * cos + jnp.where(first_half, -up, dn) * sin


def _rope_tables(c, s, reps):
    out = []
    for n in (1, reps):
        cn, sn = (jnp.concatenate([t] * n, axis=1) if n > 1 else t for t in (c, s))
        lane = lax.broadcasted_iota(jnp.int32, cn.shape, 1)
        out.append((cn, sn, (lane % HEAD_DIM) < (ROT_DIM // 2)))
    return out


def _stack_heads(x, g):
    return jnp.concatenate(
        [x[:, (g * KV_REP + r) * HEAD_DIM:(g * KV_REP + r + 1) * HEAD_DIM] for r in range(KV_REP)], axis=0)


def _attn_prompt_body(q_ref, k_ref, v_ref, gq_ref, gk_ref, inv_ref, segq_ref, segk_ref, sink_ref,
                      o_ref, kw_ref, vw_ref, kprev_ref, vprev_ref, cos_ref, sin_ref):
    j = pl.program_id(1)
    w = WINDOW

    @pl.when(j == 0)
    def _():
        kprev_ref[...] = jnp.zeros_like(kprev_ref)
        vprev_ref[...] = jnp.zeros_like(vprev_ref)
        within = lax.broadcasted_iota(jnp.int32, (w, LANES), 0).astype(F32) * inv_ref[...]
        cos_ref[...] = jnp.cos(within)
        sin_ref[...] = jnp.sin(within)

    start = (j * w).astype(F32) * inv_ref[...]
    cs, ss = jnp.cos(start), jnp.sin(start)
    cw, sw = cos_ref[...], sin_ref[...]
    (ck, sk, fk), (cq, sq, fq) = _rope_tables(cs * cw - ss * sw, ss * cw + cs * sw, QA // LANES)
    q = _rope(_head_norm(q_ref[...], gq_ref[...], segq_ref[...]), cq, sq, fq)
    k = _rope(_head_norm(k_ref[...], gk_ref[...], segk_ref[...]), ck, sk, fk)
    v = v_ref[...]
    kp = kprev_ref[...]
    vp = vprev_ref[...]

    qi = lax.broadcasted_iota(jnp.int32, (KV_REP * w, w), 0) % w
    ki = lax.broadcasted_iota(jnp.int32, (KV_REP * w, w), 1)
    m_own = ki <= qi
    m_prev = (ki > qi) & (j > 0)
    rep = lax.broadcasted_iota(jnp.int32, (KV_REP * w, 1), 0) // w
    for g in range(N_KV_A):
        sl = slice(g * HEAD_DIM, (g + 1) * HEAD_DIM)
        qg = _stack_heads(q, g).astype(BF16)
        s_own = jnp.where(m_own, _dot_nt(qg, k[:, sl].astype(BF16)) * ATTN_SCALE, NEG_INF)
        s_prev = jnp.where(m_prev, _dot_nt(qg, kp[:, sl].astype(BF16)) * ATTN_SCALE, NEG_INF)
        sink = jnp.zeros((KV_REP * w, 1), F32)
        for r in range(KV_REP):
            sink = jnp.where(rep == r, sink_ref[g * KV_REP + r], sink)
        mx = jnp.maximum(jnp.maximum(jnp.max(s_own, axis=-1, keepdims=True),
                                     jnp.max(s_prev, axis=-1, keepdims=True)), sink)
        e_own = jnp.exp(s_own - mx)
        e_prev = jnp.exp(s_prev - mx)
        den = (jnp.sum(e_own, axis=-1, keepdims=True) + jnp.sum(e_prev, axis=-1, keepdims=True)
               + jnp.exp(sink - mx))
        o = (_dot(e_own.astype(BF16), v[:, sl].astype(BF16))
             + _dot(e_prev.astype(BF16), vp[:, sl].astype(BF16))) / den
        for r in range(KV_REP):
            hq = g * KV_REP + r
            o_ref[:, hq * HEAD_DIM:(hq + 1) * HEAD_DIM] = o[r * w:(r + 1) * w]

    kprev_ref[...] = k
    vprev_ref[...] = v

    @pl.when(j == pl.num_programs(1) - 1)
    def _():
        kw_ref[0] = k
        vw_ref[0] = v


def _attn_prompt(z, batch, seq, gq, gk, inv, segq, segk, sinks):
    nb = seq // WINDOW
    blk = lambda width, col: pl.BlockSpec((WINDOW, width), lambda b, j: (b * nb + j, col))
    return pl.pallas_call(
        _attn_prompt_body,
        grid=(batch, nb),
        in_specs=[blk(QA, 0), blk(KVA, COL_K), blk(KVA, COL_V), _full(gq.shape), _full(gk.shape),
                  _full(inv.shape), _full(segq.shape), _full(segk.shape),
                  pl.BlockSpec(memory_space=pltpu.SMEM)],
        out_specs=[pl.BlockSpec((WINDOW, QA), lambda b, j: (b * nb + j, 0)),
                   pl.BlockSpec((1, WINDOW, KVA), lambda b, j: (b, 0, 0)),
                   pl.BlockSpec((1, WINDOW, KVA), lambda b, j: (b, 0, 0))],
        out_shape=[jax.ShapeDtypeStruct((batch * seq, QA), F32),
                   jax.ShapeDtypeStruct((batch, WINDOW, KVA), F32),
                   jax.ShapeDtypeStruct((batch, WINDOW, KVA), F32)],
        scratch_shapes=[pltpu.VMEM((WINDOW, KVA), F32), pltpu.VMEM((WINDOW, KVA), F32),
                        pltpu.VMEM((WINDOW, LANES), F32), pltpu.VMEM((WINDOW, LANES), F32)],
        compiler_params=_params(("arbitrary", "arbitrary")),
        name="swa_prompt",
    )(z, z, z, gq, gk, inv, segq, segk, sinks)


def _attn_decode_body(q_ref, k_ref, v_ref, kc_ref, vc_ref, gq_ref, gk_ref, inv_ref, segq_ref, segk_ref,
                      sink_ref, o_ref, kn_ref, vn_ref, *, dec_seq):
    rows_n = DEC_GROUP * dec_seq
    rows = lax.broadcasted_iota(jnp.int32, (rows_n, LANES), 0)
    pos = (PAST_LEN + rows % dec_seq).astype(F32)
    ang = pos * inv_ref[...]
    (ck, sk, fk), (cq, sq, fq) = _rope_tables(jnp.cos(ang), jnp.sin(ang), QA // LANES)
    q = _rope(_head_norm(q_ref[...], gq_ref[...], segq_ref[...]), cq, sq, fq)
    k = _rope(_head_norm(k_ref[...], gk_ref[...], segk_ref[...]), ck, sk, fk)
    v = v_ref[...]
    kn_ref[...] = k
    vn_ref[...] = v

    nq = KV_REP * rows_n
    nc = DEC_GROUP * WINDOW

    def qrow(shape):
        r = lax.broadcasted_iota(jnp.int32, shape, 0) % rows_n
        return r // dec_seq, r % dec_seq

    bq, tq = qrow((nq, nc))
    col = lax.broadcasted_iota(jnp.int32, (nq, nc), 1)
    m_cache = (col // WINDOW == bq) & (col % WINDOW > tq)
    bq, tq = qrow((nq, rows_n))
    col = lax.broadcasted_iota(jnp.int32, (nq, rows_n), 1)
    m_new = (col // dec_seq == bq) & (col % dec_seq <= tq)
    rep = lax.broadcasted_iota(jnp.int32, (nq, 1), 0) // rows_n
    for g in range(N_KV_A):
        sl = slice(g * HEAD_DIM, (g + 1) * HEAD_DIM)
        qg = _stack_heads(q, g).astype(BF16)
        s_c = jnp.where(m_cache, _dot_nt(qg, kc_ref[:, sl].astype(BF16)) * ATTN_SCALE, NEG_INF)
        s_n = jnp.where(m_new, _dot_nt(qg, k[:, sl].astype(BF16)) * ATTN_SCALE, NEG_INF)
        sink = jnp.zeros((nq, 1), F32)
        for r in range(KV_REP):
            sink = jnp.where(rep == r, sink_ref[g * KV_REP + r], sink)
        mx = jnp.maximum(jnp.maximum(jnp.max(s_c, axis=-1, keepdims=True),
                                     jnp.max(s_n, axis=-1, keepdims=True)), sink)
        e_c = jnp.exp(s_c - mx)
        e_n = jnp.exp(s_n - mx)
        den = jnp.sum(e_c, axis=-1, keepdims=True) + jnp.sum(e_n, axis=-1, keepdims=True) + jnp.exp(sink - mx)
        o = (_dot(e_c.astype(BF16), vc_ref[:, sl].astype(BF16))
             + _dot(e_n.astype(BF16), v[:, sl].astype(BF16))) / den
        for r in range(KV_REP):
            hq = g * KV_REP + r
            o_ref[:, hq * HEAD_DIM:(hq + 1) * HEAD_DIM] = o[r * rows_n:(r + 1) * rows_n]


def _attn_decode(z, row0, dec_batch, dec_seq, kc, vc, gq, gk, inv, segq, segk, sinks):
    rows_n = DEC_GROUP * dec_seq
    r0 = row0 // rows_n
    blk = lambda width, col: pl.BlockSpec((rows_n, width), lambda i: (r0 + i, col))
    cache = pl.BlockSpec((DEC_GROUP * WINDOW, KVA), lambda i: (i, 0))
    n = dec_batch * dec_seq
    return pl.pallas_call(
        functools.partial(_attn_decode_body, dec_seq=dec_seq),
        grid=(dec_batch // DEC_GROUP,),
        in_specs=[blk(QA, 0), blk(KVA, COL_K), blk(KVA, COL_V), cache, cache, _full(gq.shape),
                  _full(gk.shape), _full(inv.shape), _full(segq.shape), _full(segk.shape),
                  pl.BlockSpec(memory_space=pltpu.SMEM)],
        out_specs=[pl.BlockSpec((rows_n, QA), lambda i: (i, 0)), pl.BlockSpec((rows_n, KVA), lambda i: (i, 0)),
                   pl.BlockSpec((rows_n, KVA), lambda i: (i, 0))],
        out_shape=[jax.ShapeDtypeStruct((n, QA), F32), jax.ShapeDtypeStruct((n, KVA), F32),
                   jax.ShapeDtypeStruct((n, KVA), F32)],
        compiler_params=_params(("arbitrary",)),
        name="swa_decode",
    )(z, z, z, kc, vc, gq, gk, inv, segq, segk, sinks)


def _hgrn_lower_bound(lb_ref):
    l = lb_ref[...]
    e = jnp.exp(l - jnp.max(l, axis=0, keepdims=True))
    return e[0:1] / jnp.sum(e, axis=0, keepdims=True)


def _hgrn_gates(qb, fb, lb):
    logf = jnp.log(lb + (1.0 - lb) * jax.nn.sigmoid(fb))
    kb = (1.0 - lb) * jax.nn.sigmoid(-fb)
    return _silu(qb), kb, logf


def _hgrn_out(o, g, gate):
    return _rms(o, g) * _silu(gate)


def _hgrn_chunk(qh, kb, ih, logf, st, tril):
    c = qh.shape[0]
    hi = logf.astype(BF16)
    rest = logf - hi.astype(F32)
    mid = rest.astype(BF16)
    b = _dot(tril, hi) + _dot(tril, mid) + _dot(tril, (rest - mid.astype(F32)).astype(BF16))
    ones = jnp.ones((DK_B, LANES), BF16)
    st_b = st.astype(BF16)
    ih_b = ih.astype(BF16)
    o_parts = []
    row = lax.broadcasted_iota(jnp.int32, (c, 1), 0)
    trow = lax.broadcasted_iota(jnp.int32, (HGRN_SUB, 1), 0)
    for blk in range(c // HGRN_SUB):
        t0 = blk * HGRN_SUB
        bi = b[t0:t0 + HGRN_SUB]
        qi = qh[t0:t0 + HGRN_SUB]
        if blk == 0:
            oi = _dot_nt((qi * jnp.exp(bi)).astype(BF16), st_b)
        else:
            base = b[t0 - 1:t0]
            qd = qi * jnp.exp(bi - base)
            kd = jnp.where(row < t0, kb * jnp.exp(jnp.minimum(base - b, 0.0)), 0.0)
            a_off = _dot_nt(qd.astype(BF16), kd.astype(BF16))
            oi = _dot_nt((qd * jnp.exp(base)).astype(BF16), st_b) + _dot(a_off.astype(BF16), ih_b)
        prods = []
        for s in range(HGRN_SUB):
            e = jnp.exp(jnp.where(trow >= s, bi - bi[s:s + 1], 0.0))
            prods.append(qi * e * kb[t0 + s:t0 + s + 1])
        a_diag = _dot(jnp.concatenate(prods, axis=0).astype(BF16), ones)
        for s in range(HGRN_SUB):
            a = a_diag[s * HGRN_SUB:(s + 1) * HGRN_SUB]
            oi = oi + jnp.where(trow >= s, a, 0.0) * ih[t0 + s:t0 + s + 1]
        o_parts.append(oi)
    last = b[c - 1:c]
    st_new = st * jnp.exp(last) + _dot(ih.T.astype(BF16), (kb * jnp.exp(last - b)).astype(BF16))
    return jnp.concatenate(o_parts, axis=0), st_new


def _hgrn_prompt_body(qb_ref, fb_ref, ib_ref, gb_ref, lb_ref, go_ref, tril_ref, o_ref, s_ref, st_ref):
    tb = pl.program_id(2)

    @pl.when(tb == 0)
    def _():
        st_ref[...] = jnp.zeros_like(st_ref)

    tril = tril_ref[...]
    heads = [slice(h * DK_B, (h + 1) * DK_B) for h in range(HGRN_HEADS_PER_STEP)]
    lbs = [_hgrn_lower_bound(lb_ref.at[:, hs]) for hs in heads]
    sts = [st_ref[h] for h in range(HGRN_HEADS_PER_STEP)]
    for c in range(HGRN_ROWS // HGRN_CHUNK):
        rs = slice(c * HGRN_CHUNK, (c + 1) * HGRN_CHUNK)
        for h, hs in enumerate(heads):
            qh, kb, logf = _hgrn_gates(qb_ref[rs, hs], fb_ref[rs, hs], lbs[h])
            o, sts[h] = _hgrn_chunk(qh, kb, ib_ref[rs, hs], logf, sts[h], tril)
            o_ref[rs, hs] = _hgrn_out(o, go_ref[...], gb_ref[rs, hs])
    for h in range(HGRN_HEADS_PER_STEP):
        st_ref[h] = sts[h]

    @pl.when(tb == pl.num_programs(2) - 1)
    def _():
        for h in range(HGRN_HEADS_PER_STEP):
            s_ref[0, h] = sts[h].T


def _hgrn_prompt(z, batch, seq, lb, go, tril):
    nt = seq // HGRN_ROWS
    hp = HGRN_HEADS_PER_STEP
    assert N_HEADS_B % hp == 0 and all(c % hp == 0 for c in (COL_QB, COL_FB, COL_IB, COL_GB))
    blk = lambda col: pl.BlockSpec((HGRN_ROWS, hp * DK_B), lambda b, h, t: (b * nt + t, col // hp + h))
    return pl.pallas_call(
        _hgrn_prompt_body,
        grid=(batch, N_HEADS_B // hp, nt),
        in_specs=[blk(COL_QB), blk(COL_FB), blk(COL_IB), blk(COL_GB),
                  pl.BlockSpec((lb.shape[0], hp * DK_B), lambda b, h, t: (0, h)), _full(go.shape), _full(tril.shape)],
        out_specs=[pl.BlockSpec((HGRN_ROWS, hp * DV_B), lambda b, h, t: (b * nt + t, h)),
                   pl.BlockSpec((1, hp, DK_B, DV_B), lambda b, h, t: (b, h, 0, 0))],
        out_shape=[jax.ShapeDtypeStruct((batch * seq, VB), F32),
                   jax.ShapeDtypeStruct((batch, N_HEADS_B, DK_B, DV_B), F32)],
        scratch_shapes=[pltpu.VMEM((hp, DV_B, DK_B), F32)],
        compiler_params=_params(("arbitrary", "arbitrary", "arbitrary")),
        name="hgrn_prompt",
    )(z, z, z, z, lb, go, tril)


def _pad_rows(x, rows):
    return jnp.concatenate([x, jnp.zeros((rows - x.shape[0], x.shape[1]), x.dtype)], axis=0)


def _hgrn_decode_body(z_ref, s0_ref, lb_ref, go_ref, tril_ref, o_ref, s_ref, *, dec_seq):
    rows_n = z_ref.shape[0]
    groups = rows_n // dec_seq
    row = lax.broadcasted_iota(jnp.int32, (rows_n, 1), 0)
    for h in range(N_HEADS_B):
        col = lambda c0: slice((c0 + h) * LANES, (c0 + h + 1) * LANES)
        hs = slice(h * DK_B, (h + 1) * DK_B)
        lb = _hgrn_lower_bound(lb_ref.at[:, hs])
        qh, kb, logf = _hgrn_gates(z_ref[:, col(COL_QB)], z_ref[:, col(COL_FB)], lb)
        ih = z_ref[:, col(COL_IB)]
        b = _dot(tril_ref[...], _pad_rows(logf, LANES), HI)[:rows_n]
        b_t = _pad_rows(b, LANES).T
        ih_pad = _pad_rows(ih, LANES)
        o = jnp.zeros((rows_n, DV_B), F32)
        for e in range(groups):
            mine = (row // dec_seq) == e
            s0 = s0_ref[e, h]
            o = o + _dot(jnp.where(mine, qh * jnp.exp(b), 0.0), s0, HI)
            for s in range(dec_seq):
                r = e * dec_seq + s
                live = mine & (row >= r)
                ex = jnp.exp(jnp.where(live, b - b[r:r + 1], 0.0))
                a = jnp.sum(qh * ex * kb[r:r + 1], axis=-1, keepdims=True)
                o = o + jnp.where(live, a, 0.0) * ih[r:r + 1]
            r_last = (e + 1) * dec_seq - 1
            last = b[r_last:r_last + 1]
            kd = jnp.where(mine, kb * jnp.exp(jnp.minimum(last - b, 0.0)), 0.0)
            s_ref[e, h] = s0 * jnp.exp(b_t[:, r_last:r_last + 1]) + _dot(_pad_rows(kd, LANES).T, ih_pad, HI)
        o_ref[:, hs] = _hgrn_out(o, go_ref[...], z_ref[:, col(COL_GB)])


def _hgrn_decode(z, row0, dec_batch, dec_seq, s0, lb, go, tril):
    groups = 8 // dec_seq
    rows_n = groups * dec_seq
    r0 = row0 // rows_n
    st = pl.BlockSpec((groups, N_HEADS_B, DK_B, DV_B), lambda i: (i, 0, 0, 0))
    return pl.pallas_call(
        functools.partial(_hgrn_decode_body, dec_seq=dec_seq),
        grid=(dec_batch // groups,),
        in_specs=[pl.BlockSpec((rows_n, z.shape[1]), lambda i: (r0 + i, 0)), st,
                  _full(lb.shape), _full(go.shape), _full(tril.shape)],
        out_specs=[pl.BlockSpec((rows_n, VB), lambda i: (i, 0)), st],
        out_shape=[jax.ShapeDtypeStruct((dec_batch * dec_seq, VB), F32),
                   jax.ShapeDtypeStruct(s0.shape, F32)],
        compiler_params=_params(("arbitrary",)),
        name="hgrn_decode",
    )(z, s0, lb, go, tril)


def _conv_prompt_body(cur_ref, prev_ref, w_ref, b_ref, c_ref, ext_ref, sh_ref):
    t = pl.program_id(1)
    ext_ref[:CONV_HALO, :] = jnp.where(t > 0, prev_ref[...], 0.0)
    ext_ref[CONV_HALO:, :] = cur_ref[...]
    lead = CONV_HALO - (CONV_W - 1)
    chunk = CONV_CHUNK

    def cols(ci, carry):
        cs = pl.ds(pl.multiple_of(ci * LANES, LANES), LANES)
        for r in range(CONV_ROWS // chunk):
            acc = jnp.zeros((chunk, LANES), F32) + b_ref[:, cs]
            for res in range(SUBLANES):
                taps = range(res, CONV_W, SUBLANES)
                span = chunk + taps[-1] - res
                sh_ref[:span, :] = ext_ref[pl.ds(r * chunk + lead + res, span), cs]
                for w in taps:
                    acc = acc + sh_ref[w - res:w - res + chunk, :] * w_ref[pl.ds(w, 1), cs]
            c_ref[pl.ds(r * chunk, chunk), cs] = acc
        return carry

    lax.fori_loop(0, D_MODEL // LANES, cols, 0)


def _conv_prompt(u, batch, seq, w, b):
    nt = seq // CONV_ROWS
    per = CONV_ROWS // CONV_HALO
    return pl.pallas_call(
        _conv_prompt_body,
        grid=(batch, nt),
        in_specs=[pl.BlockSpec((CONV_ROWS, D_MODEL), lambda bi, t: (bi * nt + t, 0)),
                  pl.BlockSpec((CONV_HALO, D_MODEL), lambda bi, t: (jnp.maximum((bi * nt + t) * per - 1, 0), 0)),
                  _full(w.shape), _full(b.shape)],
        out_specs=pl.BlockSpec((CONV_ROWS, D_MODEL), lambda bi, t: (bi * nt + t, 0)),
        out_shape=jax.ShapeDtypeStruct((batch * seq, D_MODEL), F32),
        scratch_shapes=[pltpu.VMEM((CONV_HALO + CONV_ROWS, D_MODEL), F32),
                        pltpu.VMEM((CONV_CHUNK + CONV_HALO, LANES), F32)],
        compiler_params=_params(("arbitrary", "arbitrary")),
        name="conv_prompt",
    )(u, u, w, b)


def _conv_decode_body(u_ref, st_ref, w_ref, wshift_ref, b_ref, c_ref, *, dec_seq):
    for e in range(DEC_GROUP):
        past = st_ref[e]
        for t in range(dec_seq):
            acc = jnp.sum(past * wshift_ref[t], axis=0, keepdims=True) + b_ref[...]
            for t2 in range(t + 1):
                wi = CONV_W - 1 - t + t2
                acc = acc + u_ref[e * dec_seq + t2:e * dec_seq + t2 + 1, :] * w_ref[wi:wi + 1, :]
            c_ref[e * dec_seq + t:e * dec_seq + t + 1, :] = acc


def _conv_decode(u, row0, dec_batch, dec_seq, state, w, wshift, b):
    rows_n = DEC_GROUP * dec_seq
    r0 = row0 // rows_n
    return pl.pallas_call(
        functools.partial(_conv_decode_body, dec_seq=dec_seq),
        grid=(dec_batch // DEC_GROUP,),
        in_specs=[pl.BlockSpec((rows_n, D_MODEL), lambda i: (r0 + i, 0)),
                  pl.BlockSpec((DEC_GROUP, CONV_W - 1, D_MODEL), lambda i: (i, 0, 0)),
                  _full(w.shape), _full(wshift.shape), _full(b.shape)],
        out_specs=pl.BlockSpec((rows_n, D_MODEL), lambda i: (i, 0)),
        out_shape=jax.ShapeDtypeStruct((dec_batch * dec_seq, D_MODEL), F32),
        compiler_params=_params(("arbitrary",)),
        name="conv_decode",
    )(u, state, w, wshift, b)


def _extract_top(s, idx, count, none_rank):
    work = s
    rank = jnp.full(s.shape, none_rank, F32)
    vals = []
    for r in range(count):
        m = jnp.max(work, axis=0, keepdims=True)
        first = jnp.min(jnp.where(work == m, idx, np.float32(1e9)), axis=0, keepdims=True)
        hit = idx == first
        rank = jnp.where(hit, np.float32(r), rank)
        work = jnp.where(hit, -jnp.inf, work)
        vals.append(m)
    return rank, vals


_PEER_CAND = [(a, b) for a in range(PEER_TOPK) for b in range(PEER_TOPK) if (a + 1) * (b + 1) <= PEER_TOPK]
_PEER_CAND_ROWS = -(-len(_PEER_CAND) // 8) * 8


def _extract_by_value(s, count, want_rank=True):
    work = s
    rank = jnp.full(s.shape, np.float32(count), F32) if want_rank else None
    vals = []
    for r in range(count):
        m = jnp.max(work, axis=0, keepdims=True)
        hit = work == m
        if want_rank:
            rank = jnp.where(hit, np.float32(r), rank)
        work = jnp.where(hit, -jnp.inf, work)
        vals.append(m)
    done = rank < count if want_rank else work == -jnp.inf
    return rank, vals, jnp.sum(jnp.where(done, 1.0, 0.0), axis=0, keepdims=True)


def _peer_select_body(q_ref, keys_ref, rank_ref, qe_ref, lq_ref, pe_ref):
    ts = q_ref.shape[0]
    kidx = lax.broadcasted_iota(jnp.int32, (N_KEYS, ts), 0).astype(F32)
    cidx = lax.broadcasted_iota(jnp.int32, (_PEER_CAND_ROWS, ts), 0).astype(F32)
    tidx = lax.broadcasted_iota(jnp.int32, (PEER_TOPK, ts), 0).astype(F32)
    bidx = lax.broadcasted_iota(jnp.int32, (SUBLANES, ts), 0).astype(F32)
    pad =jnp.full((_PEER_CAND_ROWS, ts), -jnp.inf, F32)

    def tables(h, s0, s1, exact):
        if exact:
            rank0, v0 = _extract_top(s0, kidx, PEER_TOPK, PEER_TOPK)
            rank1, v1 = _extract_top(s1, kidx, PEER_TOPK, PEER_TOPK)
        else:
            _, v0, n0 = _extract_by_value(s0, PEER_TOPK, want_rank=False)
            rank1, v1, n1 = _extract_by_value(s1, PEER_TOPK)
        if exact:
            pairs = list(_PEER_CAND)
            cand = pad
            for ci, (a, b) in enumerate(pairs):
                cand = jnp.where(cidx == ci, v0[a] + v1[b], cand)
            crank, _ = _extract_top(cand, cidx, PEER_TOPK, PEER_TOPK)
            off = None
        else:
            col0 = jnp.zeros((PEER_TOPK, ts), F32)
            col1 = jnp.zeros((PEER_TOPK, ts), F32)
            low1 = jnp.zeros((SUBLANES, ts), F32)
            for r in range(PEER_TOPK):
                col0 = jnp.where(tidx == r, v0[r], col0)
                col1 = jnp.where(tidx == r, v1[r], col1)
                if r < SUBLANES:
                    low1 = jnp.where(bidx == r, v1[r], low1)
            ninf = np.float32(-np.inf)
            groups = [v0[0] + col1, jnp.where(tidx == 0, ninf, col0 + v1[0])]
            pairs = [(0, b) for b in range(PEER_TOPK)] + [(a, 0) if a else None for a in range(PEER_TOPK)]
            rest = [p for p in _PEER_CAND if p[0] and p[1]]
            for a in sorted({p[0] for p in rest}):
                mine = [p for p in rest if p[0] == a]
                if len(mine) > 2:
                    groups.append(jnp.where((bidx >= 1) & (bidx <= len(mine)), v0[a] + low1, ninf))
                    pairs += [(a, b) if 1 <= b <= len(mine) else None for b in range(SUBLANES)]
            loose = [p for p in rest if p not in pairs]
            assert len(loose) <= SUBLANES
            tail = jnp.full((SUBLANES, ts), ninf, F32)
            for r, (a, b) in enumerate(loose):
                tail = jnp.where(bidx == r, v0[a] + v1[b], tail)
            groups.append(tail)
            pairs += loose + [None] * (SUBLANES - len(loose))
            assert sorted(p for p in pairs if p) == sorted(_PEER_CAND)
            cand = jnp.concatenate(groups, axis=0)
            crank, _, nc = _extract_by_value(cand, PEER_TOPK)
            off = jnp.max(jnp.abs(n0 - PEER_TOPK) + jnp.abs(n1 - PEER_TOPK) + jnp.abs(nc - PEER_TOPK))
        picked = jnp.where(crank < PEER_TOPK, 1.0, 0.0)
        e = picked * jnp.exp(cand - (v0[0] + v1[0]))
        z = jnp.sum(e, axis=0, keepdims=True)
        lq = jnp.zeros((N_KEYS, ts), F32)
        for a in range(PEER_TOPK):
            cnt = jnp.zeros((1, ts), F32)
            for ci, p in enumerate(pairs):
                if p is not None and p[0] == a:
                    cnt = cnt + picked[ci:ci + 1]
            lq = jnp.where((rank0 == a) if exact else (s0 == v0[a]), cnt, lq)
        rank_b = rank1.astype(BF16)
        qe_b = jnp.exp(s1 - v1[0]).astype(BF16)
        for k in range(N_KEYS // BF16_ROWS):
            rank_ref[h, k] = pltpu.bitcast(rank_b[k * BF16_ROWS:(k + 1) * BF16_ROWS], jnp.uint32)
            qe_ref[h, k] = pltpu.bitcast(qe_b[k * BF16_ROWS:(k + 1) * BF16_ROWS], jnp.uint32)
        lq_ref[h] = lq
        pe_ref[h] = jnp.exp(s0 - v0[0]) / z
        return off

    def head(h, carry):
        c0 = pl.multiple_of(h * 2 * PEER_DKH, 2 * PEER_DKH)
        q0 = q_ref[:, pl.ds(c0, PEER_DKH)].astype(BF16)
        q1 = q_ref[:, pl.ds(c0 + PEER_DKH, PEER_DKH)].astype(BF16)
        s0 = _dot_nt(keys_ref[h, 0], q0)
        s1 = _dot_nt(keys_ref[h, 1], q1)
        off = tables(h, s0, s1, exact=False)

        @pl.when(off > 0.5)
        def _():
            tables(h, s0, s1, exact=True)

        return carry

    lax.fori_loop(0, PEER_HEADS, head, 0)


def _peer_select(q, keys):
    n = q.shape[0]
    out = jax.ShapeDtypeStruct((PEER_HEADS, N_KEYS, n), F32)
    ospec = pl.BlockSpec((PEER_HEADS, N_KEYS, SEL_TILE), lambda i: (0, 0, i))
    packed = (PEER_HEADS, N_KEYS // BF16_ROWS, BF16_ROWS // 2)
    out_b = jax.ShapeDtypeStruct(packed + (n,), jnp.uint32)
    ospec_b = pl.BlockSpec(packed + (SEL_TILE,), lambda i: (0, 0, 0, i))
    return pl.pallas_call(
        _peer_select_body,
        grid=(n // SEL_TILE,),
        in_specs=[pl.BlockSpec((SEL_TILE, q.shape[1]), lambda i: (i, 0)), _full(keys.shape)],
        out_specs=[ospec_b, ospec_b, ospec, ospec],
        out_shape=[out_b, out_b, out, out],
        compiler_params=_params(("arbitrary",)),
        name="peer_select",
    )(q, keys)


def _peer_dense_body(ht_ref, u_ref, vt_ref, rank_ref, qe_ref, lq_ref, pe_ref, x_ref, *rest, prompt_tiles):
    *o_ref, acc_ref, a_ref, w_ref = rest
    step = pl.program_id(1)

    @pl.when(step == 0)
    def _():
        acc_ref[...] = jnp.zeros_like(acc_ref)

    @pl.when(step >= 0)
    def _():
        quarter = PEER_ROWS * N_KEYS // 4
        for qi in range(4):
            rows = slice(qi * quarter, (qi + 1) * quarter)
            a_ref[rows, :] = _dot(u_ref[rows, :], ht_ref[...])

    @pl.when(step >= 0)
    def _():
        sub = (N_KEYS // BF16_ROWS, BF16_ROWS, LANES)
        for il in range(PEER_ROWS):
            for ci in range(TOKEN_TILE // LANES):
                cs = slice(ci * LANES, (ci + 1) * LANES)
                gate = jnp.zeros(sub, BF16)
                for h in range(PEER_HEADS):
                    lq = jnp.broadcast_to(lq_ref[h, il:il + 1, cs], sub[1:]).astype(BF16)
                    pe = jnp.broadcast_to(pe_ref[h, il:il + 1, cs], sub[1:]).astype(BF16)
                    take = pltpu.bitcast(rank_ref[h, :, :, cs], BF16) < lq[None]
                    qe = pltpu.bitcast(qe_ref[h, :, :, cs], BF16)
                    gate = gate + jnp.where(take, qe, jnp.zeros(sub, BF16)) * pe[None]
                for k in range(sub[0]):
                    rs = slice(il * N_KEYS + k * BF16_ROWS, il * N_KEYS + (k + 1) * BF16_ROWS)
                    w_ref[rs, cs] = gate[k] * _gelu(a_ref[rs, cs].astype(BF16))
        acc_ref[...] += _dot(vt_ref[0], w_ref[...])

    last = step == pl.num_programs(1) - 1
    if prompt_tiles is None:
        @pl.when(last)
        def _():
            o_ref[0][...] = x_ref[...] + acc_ref[...].T
    else:
        tile = pl.program_id(0)
        o_prompt, o_sample = o_ref

        @pl.when(last & (tile < prompt_tiles))
        def _():
            o_prompt[...] = x_ref[...] + acc_ref[...].T

        @pl.when(last & (tile >= prompt_tiles))
        def _():
            o_sample[...] = x_ref[...] + acc_ref[...].T


def _peer_dense(ht, u, vt, rank, qe, lq, pe, x, n_prompt=None):
    n = x.shape[0]
    ex = PEER_ROWS * N_KEYS
    steps = N_KEYS // PEER_ROWS
    tok = pl.BlockSpec((PEER_HEADS, N_KEYS // BF16_ROWS, BF16_ROWS // 2, TOKEN_TILE), lambda t, e: (0, 0, 0, t))
    rowsel = pl.BlockSpec((PEER_HEADS, PEER_ROWS, TOKEN_TILE), lambda t, e: (0, e, t))
    if n_prompt is None:
        tp = None
        out_specs = pl.BlockSpec((TOKEN_TILE, D_MODEL), lambda t, e: (t, 0))
        out_shape = jax.ShapeDtypeStruct((n, D_MODEL), F32)
    else:
        tp = n_prompt // TOKEN_TILE
        out_specs = [pl.BlockSpec((TOKEN_TILE, D_MODEL), lambda t, e: (jnp.minimum(t, tp - 1), 0)),
                     pl.BlockSpec((TOKEN_TILE, D_MODEL), lambda t, e: (jnp.maximum(t - tp, 0), 0))]
        out_shape = [jax.ShapeDtypeStruct((n_prompt, D_MODEL), F32),
                     jax.ShapeDtypeStruct((n - n_prompt, D_MODEL), F32)]
    return pl.pallas_call(
        functools.partial(_peer_dense_body, prompt_tiles=tp),
        grid=(n // TOKEN_TILE, steps),
        in_specs=[pl.BlockSpec((D_MODEL, TOKEN_TILE), lambda t, e: (0, t)),
                  pl.BlockSpec((ex, D_MODEL), lambda t, e: (e, 0)),
                  pl.BlockSpec((1, D_MODEL, ex), lambda t, e: (e, 0, 0)),
                  tok, tok, rowsel, rowsel,
                  pl.BlockSpec((TOKEN_TILE, D_MODEL), lambda t, e: (t, 0))],
        out_specs=out_specs,
        out_shape=out_shape,
        scratch_shapes=[pltpu.VMEM((D_MODEL, TOKEN_TILE), F32), pltpu.VMEM((ex, TOKEN_TILE), F32),
                        pltpu.VMEM((ex, TOKEN_TILE), BF16)],
        compiler_params=_params(("arbitrary", "arbitrary")),
        name="peer_dense",
    )(ht, u, vt, rank, qe, lq, pe, x)


def _peer(x1, ht, q, keys, u, vt, n_prompt=None):
    rank, qe, lq, pe = _peer_select(q, keys)
    return _peer_dense(ht, u, vt, rank, qe, lq, pe, x1, n_prompt)


def _value_blocks(v):
    ex = PEER_ROWS * N_KEYS
    return v.astype(BF16).reshape(v.shape[0] // ex, ex, v.shape[1]).transpose(0, 2, 1)


def _block_diag_mean(width):
    idx = np.arange(width) // HEAD_DIM
    return jnp.asarray((idx[:, None] == idx[None, :]).astype(np.float32) / HEAD_DIM, BF16)


def kernel(x_prompt, x_sample, cache_swa_k, cache_swa_v, state_hgrn, state_conv, norm_mix_g, norm_ffn_g, w_in0, attn_q_norm_g, attn_k_norm_g, attn_sinks, hgrn_lb, hgrn_o_norm_g, w_out0, conv_w_pw1, conv_b_pw1, conv_w_dw, conv_b_dw, conv_ln_g, conv_ln_b, conv_w_pw2, peer_w_q, peer_keys, peer_u, peer_v):
    batch, seq, _ = x_prompt.shape
    dec_batch, dec_seq, _ = x_sample.shape
    n_p = batch * seq
    n_s = dec_batch * dec_seq
    assert seq % CONV_ROWS == 0 and seq % HGRN_ROWS == 0 and n_p % TOKEN_TILE == 0 and n_s % TOKEN_TILE == 0
    assert dec_batch % DEC_GROUP == 0 and 8 % dec_seq == 0 and dec_seq <= CONV_W - 1

    x = (x_prompt.reshape(n_p, D_MODEL), x_sample.reshape(n_s, D_MODEL))
    row = lambda v: v.reshape(1, -1).astype(F32)

    z = _in0(x, row(norm_mix_g[0]), w_in0.astype(BF16))
    d = np.arange(LANES) % HEAD_DIM
    inv = ROPE_THETA ** (-jnp.arange(0, ROT_DIM, 2, dtype=F32) / ROT_DIM)
    inv_lane = jnp.where(jnp.asarray(d < ROT_DIM), inv[d % (ROT_DIM // 2)], 0.0).reshape(1, LANES)
    gq = row(jnp.tile(attn_q_norm_g, N_HEADS_A))
    gk = row(jnp.tile(attn_k_norm_g, N_KV_A))
    segq, segk = _block_diag_mean(QA), _block_diag_mean(KVA)
    sinks = attn_sinks.astype(F32)
    attn_p, k_win_p, v_win_p = _attn_prompt(z, batch, seq, gq, gk, inv_lane, segq, segk, sinks)
    kc = cache_swa_k.reshape(dec_batch * WINDOW, KVA)
    vc = cache_swa_v.reshape(dec_batch * WINDOW, KVA)
    attn_s, k_new, v_new = _attn_decode(z, n_p, dec_batch, dec_seq, kc, vc, gq, gk, inv_lane, segq, segk, sinks)

    go = row(hgrn_o_norm_g)
    tril = jnp.asarray(np.tril(np.ones((HGRN_CHUNK, HGRN_CHUNK), np.float32)), BF16)
    hg_p, s_p = _hgrn_prompt(z, batch, seq, hgrn_lb, go, tril)
    r = np.arange(LANES)
    tril_dec = jnp.asarray(((r[:, None] // dec_seq == r[None, :] // dec_seq) & (r[None, :] <= r[:, None]))
                           .astype(np.float32))
    hg_s, s_s = _hgrn_decode(z, n_p, dec_batch, dec_seq, state_hgrn, hgrn_lb, go, tril_dec)

    x1, ht, q = _mix_out(_out0_body, [(attn_p, attn_s), (hg_p, hg_s)], x, [w_out0.astype(BF16)], row(norm_ffn_g[0]),
                         peer_w_q[0].astype(BF16), "mix0_out")
    x2 = _peer(x1, ht, q, peer_keys[0].astype(BF16), peer_u[0].astype(BF16), _value_blocks(peer_v[0]))

    u = _glu(x2, row(norm_mix_g[1]), conv_w_pw1.astype(BF16), row(conv_b_pw1))
    wdw = jnp.concatenate([conv_w_dw, jnp.zeros((1, D_MODEL), F32)], axis=0)
    bdw = row(conv_b_dw)
    c_p = _conv_prompt(u, batch, seq, wdw, bdw)
    wshift = jnp.stack([jnp.concatenate([jnp.zeros((t, D_MODEL), F32), conv_w_dw[:CONV_W - 1 - t]], axis=0)
                        for t in range(dec_seq)])
    c_s = _conv_decode(u, n_p, dec_batch, dec_seq, state_conv, wdw, wshift, bdw)
    x3, ht, q = _mix_out(_post1_body, [(c_p, c_s)], x2, [row(conv_ln_g), row(conv_ln_b), conv_w_pw2.astype(BF16)],
                         row(norm_ffn_g[1]), peer_w_q[1].astype(BF16), "mix1_out")
    y_p, y_s = _peer(x3, ht, q, peer_keys[1].astype(BF16), peer_u[1].astype(BF16),
                     _value_blocks(peer_v[1]), n_p)

    kv = lambda t: t.reshape(t.shape[0], WINDOW, N_KV_A, HEAD_DIM)
    k_win_s = jnp.concatenate([cache_swa_k[:, dec_seq:], k_new.reshape(dec_batch, dec_seq, N_KV_A, HEAD_DIM)], axis=1)
    v_win_s = jnp.concatenate([cache_swa_v[:, dec_seq:], v_new.reshape(dec_batch, dec_seq, N_KV_A, HEAD_DIM)], axis=1)
    u_p = u[:n_p].reshape(batch, seq, D_MODEL)
    u_s = u[n_p:].reshape(dec_batch, dec_seq, D_MODEL)
    conv_buf_p = u_p[:, seq - (CONV_W - 1):]
    conv_buf_s = jnp.concatenate([state_conv[:, dec_seq:], u_s], axis=1)
    return (y_p.reshape(batch, seq, D_MODEL), y_s.reshape(dec_batch, dec_seq, D_MODEL),
            kv(k_win_p), kv(v_win_p), s_p, conv_buf_p, k_win_s, v_win_s, s_s, conv_buf_s)
```
